```python
import math
import jax
import jax.numpy as jnp
from jax import lax
import numpy as np

D_MODEL = 1024
BATCH = 16
SEQ = 2048
DEPTH = 2

CTX_LEN = 256
GRID_W = 64
D_MIX = D_MODEL
W_A = D_MIX // 4
W_B = D_MIX // 4
W_C = D_MIX // 4
W_D = D_MIX - W_A - W_B - W_C
MLSTM_HEADS = 4
MLSTM_HD = W_A // MLSTM_HEADS
MLSTM_CHUNK = 64
SGU_GROUPS = 4
SGU_GD = W_B // SGU_GROUPS
SGU_CHUNK = 128
CONV_WIDTH = 31
CONV_PAD = CONV_WIDTH // 2
S5_P = 16
S5_GROUPS = W_D // S5_P
S5_N = 64
D_FF = -(-8 * D_MODEL // (3 * 256)) * 256
ALPHA = (2 * DEPTH) ** 0.25
BETA = (8 * DEPTH) ** -0.25
LN_EPS = 1e-5
OFF_B = 4 * W_A + 4 * MLSTM_HEADS
OFF_C = OFF_B + 2 * W_B
OFF_D = OFF_C + 2 * W_C
P_IN = OFF_D + W_D

kernel_name = 'hybrid_parallel_mlstm_sgu_conv_s5_dit'


def layer_norm(x, w, b):
    xf = x.astype(jnp.float32)
    mu = jnp.mean(xf, -1, keepdims=True)
    var = jnp.mean(jnp.square(xf - mu), -1, keepdims=True)
    return ((xf - mu) * lax.rsqrt(var + LN_EPS) * w + b).astype(x.dtype)


def swiglu(h, w_in, w_out):
    g, u = jnp.split(h @ w_in, 2, axis=-1)
    return (jax.nn.silu(g) * u) @ w_out


def mlstm_split(z, gate_bias):
    bsz, t, _ = z.shape
    heads = lambda a: a.reshape(bsz, t, MLSTM_HEADS, MLSTM_HD).transpose(0, 2, 1, 3)
    q, k, v, o = (z[..., i * W_A:(i + 1) * W_A] for i in range(4))
    g = (z[..., 4 * W_A:] + gate_bias).astype(jnp.float32)
    g = g.reshape(bsz, t, 4, MLSTM_HEADS).transpose(2, 0, 3, 1)
    gates = (g[0], jax.nn.log_sigmoid(g[1]), g[2], jax.nn.log_sigmoid(g[3]))
    return heads(q), heads(k) * MLSTM_HD ** -0.5, heads(v), o, gates


def mlstm_zero_state(bsz):
    return (jnp.zeros((bsz, MLSTM_HEADS, MLSTM_HD, MLSTM_HD), jnp.float32),
            jnp.zeros((bsz, MLSTM_HEADS, MLSTM_HD), jnp.float32),
            jnp.zeros((bsz, MLSTM_HEADS), jnp.float32))


def mlstm_context_state(k, v, ig, lf):
    k, v = k.astype(jnp.float32), v.astype(jnp.float32)
    cum = jnp.cumsum(lf, axis=-1)
    logw = cum[..., -1:] - cum + ig
    m = jnp.max(logw, axis=-1)
    e = jnp.exp(logw - m[..., None])
    C = jnp.einsum('bht,bhtv,bhtk->bhvk', e, v, k)
    n = jnp.einsum('bht,bhtk->bhk', e, k)
    return (C, n, m)


def mlstm_chunkwise(q, k, v, ig, lf, state):
    bsz, nh, t, d = q.shape
    nc = t // MLSTM_CHUNK
    chunks = lambda a: jnp.moveaxis(a.astype(jnp.float32).reshape(bsz, nh, nc, MLSTM_CHUNK, *a.shape[3:]), 2, 0)
    xs = tuple(chunks(a) for a in (q, k, v, ig, lf))
    lower = jnp.tril(jnp.ones((MLSTM_CHUNK, MLSTM_CHUNK), bool))

    def step(carry, inp):
        C, n, m = carry
        qc, kc, vc, ic, fc = inp
        b = jnp.cumsum(fc, axis=-1)
        logw = jnp.where(lower, b[..., :, None] - b[..., None, :] + ic[..., None, :], -jnp.inf)
        inter = b + m[..., None]
        m_t = jnp.maximum(inter, jnp.max(logw, axis=-1))
        w_inter = jnp.exp(inter - m_t)
        s = jnp.einsum('bhtd,bhsd->bhts', qc, kc) * jnp.exp(logw - m_t[..., None])
        num = jnp.einsum('bhts,bhsd->bhtd', s, vc) + w_inter[..., None] * jnp.einsum('bhvk,bhtk->bhtv', C, qc)
        den = jnp.sum(s, axis=-1) + w_inter * jnp.einsum('bhk,bhtk->bht', n, qc)
        h = num / jnp.maximum(jnp.abs(den), jnp.exp(-m_t))[..., None]
        b_end = b[..., -1]
        logu = b_end[..., None] - b + ic
        m_new = jnp.maximum(b_end + m, jnp.max(logu, axis=-1))
        e = jnp.exp(logu - m_new[..., None])
        decay = jnp.exp(b_end + m - m_new)
        C = decay[..., None, None] * C + jnp.einsum('bhs,bhsv,bhsk->bhvk', e, vc, kc)
        n = decay[..., None] * n + jnp.einsum('bhs,bhsk->bhk', e, kc)
        return (C, n, m_new), h

    state, h = lax.scan(step, state, xs)
    return jnp.moveaxis(h, 0, 2).reshape(bsz, nh, t, d), state


def mlstm_readout(h, o, norm_w):
    bsz, _, t, _ = h.shape
    h = h.transpose(0, 2, 1, 3)
    mu = jnp.mean(h, -1, keepdims=True)
    var = jnp.mean(jnp.square(h - mu), -1, keepdims=True)
    hn = (h - mu) * lax.rsqrt(var + LN_EPS) * norm_w.reshape(MLSTM_HEADS, MLSTM_HD)
    return (hn.reshape(bsz, t, W_A) * jax.nn.sigmoid(o.astype(jnp.float32))).astype(o.dtype)


def mlstm_mixer(zl, zc, gate_bias, norm_w, need_ctx):
    ql, kl, vl, ol, gl = mlstm_split(zl, gate_bias)
    qc, kc, vc, oc, gc = mlstm_split(zc, gate_bias)
    hl_dirs, hc_dirs = [], []
    for d in range(2):
        orient = (lambda a: jnp.flip(a, axis=2)) if d else (lambda a: a)
        if need_ctx:
            hc, state = mlstm_chunkwise(orient(qc), orient(kc), orient(vc), orient(gc[2 * d]),
                                        orient(gc[2 * d + 1]), mlstm_zero_state(zc.shape[0]))
            hc_dirs.append(orient(hc))
        else:
            state = mlstm_context_state(orient(kc), orient(vc), orient(gc[2 * d]), orient(gc[2 * d + 1]))
        hl, _ = mlstm_chunkwise(orient(ql), orient(kl), orient(vl), orient(gl[2 * d]), orient(gl[2 * d + 1]), state)
        hl_dirs.append(orient(hl))
    yl = mlstm_readout(hl_dirs[0] + hl_dirs[1], ol, norm_w)
    yc = mlstm_readout(hc_dirs[0] + hc_dirs[1], oc, norm_w) if need_ctx else None
    return yl, yc


def sgu_mixer(z, ln_w, ln_b, w_s, b_s):
    bsz, t, _ = z.shape
    u = jax.nn.gelu(z[..., :W_B])
    v = layer_norm(jax.nn.gelu(z[..., W_B:]), ln_w, ln_b)
    v = v.reshape(bsz, t // SGU_CHUNK, SGU_CHUNK, SGU_GROUPS, SGU_GD)
    v = jnp.einsum('gts,bnsgc->bntgc', w_s, v) + b_s.T[:, :, None]
    return u * v.reshape(bsz, t, W_B)


def conv_mixer(z, w, b, ln_w, ln_b, rows):
    a = z[..., :W_C] * jax.nn.sigmoid(z[..., W_C:])
    bsz, t, ch = a.shape
    a = a.reshape(bsz * rows, t // rows, ch)
    a = lax.conv_general_dilated(a, w[:, None, :], window_strides=(1,), padding=[(CONV_PAD, CONV_PAD)],
                                 dimension_numbers=('NWC', 'WIO', 'NWC'), feature_group_count=ch)
    a = a.reshape(bsz, t, ch) + b
    return jax.nn.silu(layer_norm(a, ln_w, ln_b))


def s5_discretize(a_re, a_im, log_dt, b_re, b_im):
    lam = lax.complex(a_re.astype(jnp.float32), a_im.astype(jnp.float32))
    lam_dt = lam * jnp.exp(log_dt.astype(jnp.float32))[:, None]
    a_bar = jnp.exp(lam_dt)
    b_bar = ((a_bar - 1.0) / lam)[..., None] * lax.complex(b_re.astype(jnp.float32), b_im.astype(jnp.float32))
    return lam_dt, a_bar, b_bar


def s5_scan(bu, a_bar):
    def combine(left, right):
        a_l, b_l = left
        a_r, b_r = right
        return a_r * a_l, a_r * b_l + b_r
    a = jnp.broadcast_to(a_bar, (1,) + bu.shape[1:])
    return lax.associative_scan(combine, (a, bu), axis=1)[1]


def s5_final_state(bu, lam_dt):
    t = bu.shape[1]
    lags = jnp.arange(t - 1, -1, -1, dtype=jnp.float32)
    return jnp.einsum('tgn,btgn->bgn', jnp.exp(lags[:, None, None] * lam_dt), bu)


def s5_mixer(zl, zc, a_re, a_im, log_dt, b_re, b_im, c_re, c_im, d_skip, glu_w, glu_b, need_ctx):
    grouped = lambda z: z.astype(jnp.float32).reshape(z.shape[0], z.shape[1], S5_GROUPS, S5_P)
    ul, uc = grouped(zl), grouped(zc)
    xl_dirs, xc_dirs = [], []
    for d in range(2):
        orient = (lambda a: jnp.flip(a, axis=1)) if d else (lambda a: a)
        lam_dt, a_bar, b_bar = s5_discretize(a_re[d], a_im[d], log_dt[d], b_re, b_im)
        bul = orient(jnp.einsum('btgp,gnp->btgn', ul, b_bar))
        buc = orient(jnp.einsum('btgp,gnp->btgn', uc, b_bar))
        if need_ctx:
            sc = s5_scan(buc, a_bar)
            h0 = sc[:, -1]
            xc_dirs.append(orient(sc))
        else:
            h0 = s5_final_state(buc, lam_dt)
        sl = s5_scan(bul.at[:, 0].add(a_bar * h0), a_bar)
        xl_dirs.append(orient(sl))
    c_mat = lax.complex(c_re.astype(jnp.float32), c_im.astype(jnp.float32))
    d_mat = d_skip.astype(jnp.float32).reshape(S5_GROUPS, S5_P)

    def readout(states, u, dtype):
        y = jnp.real(jnp.einsum('gpn,btgn->btgp', c_mat, states)) + d_mat * u
        y = jax.nn.gelu(y.reshape(u.shape[0], u.shape[1], W_D))
        return (y * jax.nn.sigmoid(y @ glu_w.astype(jnp.float32) + glu_b.astype(jnp.float32))).astype(dtype)

    yl = readout(xl_dirs[0] + xl_dirs[1], ul, zl.dtype)
    yc = readout(xc_dirs[0] + xc_dirs[1], uc, zc.dtype) if need_ctx else None
    return yl, yc


def setup_inputs(seed: int = 0) -> dict:
    key = jax.random.key(seed)
    ks = iter(jax.random.split(key, 48))
    f32 = jnp.float32
    nrm = lambda shape, scale: jax.random.normal(next(ks), shape, f32) * scale
    L, D = DEPTH, D_MODEL
    x = nrm((BATCH, SEQ, D), 1.0)
    c = nrm((BATCH, D), 1.0)
    ctx = nrm((BATCH, CTX_LEN, D), 1.0)
    c_ctx = nrm((D,), 1.0)
    w_mod = nrm((L, D, 6 * D), 0.5 * D ** -0.5)
    b_mod = nrm((L, 6 * D), 0.02)
    w_in = nrm((L, D, P_IN), D ** -0.5)
    ig_b = nrm((L, 2, 1, MLSTM_HEADS), 0.1)
    fg_b = jnp.linspace(3.0, 6.0, MLSTM_HEADS, dtype=f32) + nrm((L, 2, 1, MLSTM_HEADS), 0.1)
    mlstm_gate_bias = jnp.concatenate([ig_b, fg_b], axis=2).reshape(L, 4 * MLSTM_HEADS)
    mlstm_norm_w = 1.0 + nrm((L, W_A), 0.02)
    sgu_ln_w = 1.0 + nrm((L, W_B), 0.02)
    sgu_ln_b = nrm((L, W_B), 0.02)
    sgu_w = nrm((L, SGU_GROUPS, SGU_CHUNK, SGU_CHUNK), SGU_CHUNK ** -0.5)
    sgu_b = 1.0 + nrm((L, SGU_GROUPS, SGU_CHUNK), 0.02)
    conv_w = nrm((L, CONV_WIDTH, W_C), CONV_WIDTH ** -0.5)
    conv_b = nrm((L, W_C), 0.02)
    conv_ln_w = 1.0 + nrm((L, W_C), 0.02)
    conv_ln_b = nrm((L, W_C), 0.02)
    s5_a_re = -0.5 + nrm((L, 2, S5_GROUPS, S5_N), 0.01)
    s5_a_im = math.pi * jnp.arange(S5_N, dtype=f32) + nrm((L, 2, S5_GROUPS, S5_N), 0.01)
    s5_log_dt = jax.random.uniform(next(ks), (L, 2, S5_GROUPS), f32, math.log(1e-3), math.log(1e-1))
    s5_b_re = nrm((L, S5_GROUPS, S5_N, S5_P), (2 * S5_P) ** -0.5)
    s5_b_im = nrm((L, S5_GROUPS, S5_N, S5_P), (2 * S5_P) ** -0.5)
    s5_c_re = nrm((L, S5_GROUPS, S5_P, S5_N), (2 * S5_N) ** -0.5)
    s5_c_im = nrm((L, S5_GROUPS, S5_P, S5_N), (2 * S5_N) ** -0.5)
    s5_d = nrm((L, W_D), 1.0)
    s5_glu_w = nrm((L, W_D, W_D), W_D ** -0.5)
    s5_glu_b = nrm((L, W_D), 0.02)
    w_out = nrm((L, D_MIX, D), BETA * D_MIX ** -0.5)
    ln1_w = 1.0 + nrm((L, D), 0.02)
    ln1_b = nrm((L, D), 0.02)
    w_ffn_in = nrm((L, D, 2 * D_FF), D ** -0.5)
    w_ffn_out = nrm((L, D_FF, D), BETA * D_FF ** -0.5)
    ln2_w = 1.0 + nrm((L, D), 0.02)
    ln2_b = nrm((L, D), 0.02)
    return {'x': x, 'c': c, 'ctx': ctx, 'c_ctx': c_ctx, 'w_mod': w_mod, 'b_mod': b_mod, 'w_in': w_in,
            'mlstm_gate_bias': mlstm_gate_bias, 'mlstm_norm_w': mlstm_norm_w,
            'sgu_ln_w': sgu_ln_w, 'sgu_ln_b': sgu_ln_b, 'sgu_w': sgu_w, 'sgu_b': sgu_b,
            'conv_w': conv_w, 'conv_b': conv_b, 'conv_ln_w': conv_ln_w, 'conv_ln_b': conv_ln_b,
            's5_a_re': s5_a_re, 's5_a_im': s5_a_im, 's5_log_dt': s5_log_dt, 's5_b_re': s5_b_re,
            's5_b_im': s5_b_im, 's5_c_re': s5_c_re, 's5_c_im': s5_c_im, 's5_d': s5_d,
            's5_glu_w': s5_glu_w, 's5_glu_b': s5_glu_b, 'w_out': w_out, 'ln1_w': ln1_w, 'ln1_b': ln1_b,
            'w_ffn_in': w_ffn_in, 'w_ffn_out': w_ffn_out, 'ln2_w': ln2_w, 'ln2_b': ln2_b}


def reference(x, c, ctx, c_ctx, w_mod, b_mod, w_in, mlstm_gate_bias, mlstm_norm_w,
              sgu_ln_w, sgu_ln_b, sgu_w, sgu_b, conv_w, conv_b, conv_ln_w, conv_ln_b,
              s5_a_re, s5_a_im, s5_log_dt, s5_b_re, s5_b_im, s5_c_re, s5_c_im, s5_d,
              s5_glu_w, s5_glu_b, w_out, ln1_w, ln1_b, w_ffn_in, w_ffn_out, ln2_w, ln2_b):
    rows = x.shape[1] // GRID_W
    xc = ctx
    for l in range(DEPTH):
        need_ctx = l < DEPTH - 1
        w_in_l = w_in[l]
        mod = jax.nn.silu(c) @ w_mod[l] + b_mod[l]
        mod_c = jax.nn.silu(c_ctx) @ w_mod[l] + b_mod[l]
        sh1, sc1, g1, sh2, sc2, g2 = jnp.split(mod[:, None, :], 6, axis=-1)
        sh1c, sc1c, g1c, sh2c, sc2c, g2c = jnp.split(mod_c, 6, axis=-1)
        h = x * (1.0 + sc1) + sh1
        hc = xc * (1.0 + sc1c) + sh1c
        z = h @ w_in_l
        zc_a = hc @ w_in_l[:, :OFF_B]
        zc_d = hc @ w_in_l[:, OFF_D:]
        y_a, yc_a = mlstm_mixer(z[..., :OFF_B], zc_a, mlstm_gate_bias[l], mlstm_norm_w[l], need_ctx)
        y_b = sgu_mixer(z[..., OFF_B:OFF_C], sgu_ln_w[l], sgu_ln_b[l], sgu_w[l], sgu_b[l])
        y_c = conv_mixer(z[..., OFF_C:OFF_D], conv_w[l], conv_b[l], conv_ln_w[l], conv_ln_b[l], rows)
        y_d, yc_d = s5_mixer(z[..., OFF_D:], zc_d, s5_a_re[l], s5_a_im[l], s5_log_dt[l], s5_b_re[l], s5_b_im[l],
                             s5_c_re[l], s5_c_im[l], s5_d[l], s5_glu_w[l], s5_glu_b[l], need_ctx)
        y = jnp.concatenate([y_a, y_b, y_c, y_d], axis=-1) @ w_out[l]
        x = layer_norm(ALPHA * x + g1 * y, ln1_w[l], ln1_b[l])
        x = layer_norm(ALPHA * x + g2 * swiglu(x * (1.0 + sc2) + sh2, w_ffn_in[l], w_ffn_out[l]), ln2_w[l], ln2_b[l])
        if need_ctx:
            yc_b = sgu_mixer(hc @ w_in_l[:, OFF_B:OFF_C], sgu_ln_w[l], sgu_ln_b[l], sgu_w[l], sgu_b[l])
            yc_c = conv_mixer(hc @ w_in_l[:, OFF_C:OFF_D], conv_w[l], conv_b[l], conv_ln_w[l], conv_ln_b[l], 1)
            yc = jnp.concatenate([yc_a, yc_b, yc_c, yc_d], axis=-1) @ w_out[l]
            xc = layer_norm(ALPHA * xc + g1c * yc, ln1_w[l], ln1_b[l])
            xc = layer_norm(ALPHA * xc + g2c * swiglu(xc * (1.0 + sc2c) + sh2c, w_ffn_in[l], w_ffn_out[l]),
                            ln2_w[l], ln2_b[l])
    return x
```

```python
import functools
import math

import jax
import jax.numpy as jnp
from jax import lax
from jax.experimental import pallas as pl
from jax.experimental.pallas import tpu as pltpu

F32 = jnp.float32
BF16 = jnp.bfloat16

D_MODEL = 1024
DEPTH = 2
CTX_LEN = 256
GRID_W = 64
W_MIX = 256
HEADS = 4
HEAD_DIM = W_MIX // HEADS
SGU_GROUPS = 4
SGU_GD = W_MIX // SGU_GROUPS
SGU_CHUNK = 128
CONV_WIDTH = 31
CONV_PAD = CONV_WIDTH // 2
S5_P = 16
S5_GROUPS = W_MIX // S5_P
S5_N = 64
S5_CHUNK = 128
S5_FLAT = S5_P * S5_CHUNK
D_FF = 2816
FF_CHUNK = 256
ALPHA = (2 * DEPTH) ** 0.25
LN_EPS = 1e-5
TILE = 256
N_GATES = 4 * HEADS
GATE_PAD = 128
C_QKV = 0
C_O = 3 * W_MIX
C_B = 4 * W_MIX
C_C = C_B + 2 * W_MIX
C_D = C_C + 2 * W_MIX
C_G = C_D + W_MIX
P_COLS = C_G + GATE_PAD
VMEM_LIMIT = 56 * 1024 * 1024


def _dot(a, b):
    return jnp.dot(a, b, preferred_element_type=F32)


def _dot_nt(a, b):
    return lax.dot_general(a, b, (((1,), (1,)), ((), ())), preferred_element_type=F32)


def _dot_tn(a, b):
    return lax.dot_general(a, b, (((0,), (0,)), ((), ())), preferred_element_type=F32)


def _split3(x):
    hi = x.astype(BF16)
    r = x - hi.astype(F32)
    mid = r.astype(BF16)
    lo = (r - mid.astype(F32)).astype(BF16)
    return hi, mid, lo


def _dot3_l(x, w_bf16):
    hi, mid, lo = _split3(x)
    return _dot(hi, w_bf16) + _dot(mid, w_bf16) + _dot(lo, w_bf16)


def _dot3_r(w_bf16, x):
    hi, mid, lo = _split3(x)
    return _dot(w_bf16, hi) + _dot(w_bf16, mid) + _dot(w_bf16, lo)


def _layer_norm(x, w, b):
    mu = jnp.mean(x, axis=-1, keepdims=True)
    d = x - mu
    var = jnp.mean(d * d, axis=-1, keepdims=True)
    return d * lax.rsqrt(var + LN_EPS) * w + b


def _sigmoid(x):
    return jax.nn.sigmoid(x)


def _silu(x):
    return x * jax.nn.sigmoid(x)


def _gelu(x):
    return jax.nn.gelu(x, approximate=True)


def _log_sigmoid(x):
    return jnp.minimum(x, 0.0) - jnp.log1p(jnp.exp(-jnp.abs(x)))


def _params(*sem):
    return pltpu.CompilerParams(dimension_semantics=sem, vmem_limit_bytes=VMEM_LIMIT)


def _const_spec(shape):
    nd = len(shape)
    return pl.BlockSpec(shape, lambda *_: (0,) * nd, pipeline_mode=pl.Buffered(1))


def _mod_kernel(c_ref, w_ref, b_ref, o_ref):
    s = _silu(c_ref[...])
    s_hi = s.astype(BF16)
    s_lo = (s - s_hi.astype(F32)).astype(BF16)
    w = w_ref[0]
    w_hi = w.astype(BF16)
    w_lo = (w - w_hi.astype(F32)).astype(BF16)
    o_ref[0] = _dot(s_hi, w_hi) + _dot(s_hi, w_lo) + _dot(s_lo, w_hi) + b_ref[0]


def _modulation(cvec, w_mod, b_mod):
    depth, d, n = w_mod.shape
    rows = cvec.shape[0]
    tn = 1024
    return pl.pallas_call(
        _mod_kernel,
        grid=(depth, n // tn),
        in_specs=[pl.BlockSpec((rows, d), lambda l, j: (0, 0)),
                  pl.BlockSpec((1, d, tn), lambda l, j: (l, 0, j)),
                  pl.BlockSpec((1, 1, tn), lambda l, j: (l, 0, j))],
        out_specs=pl.BlockSpec((1, rows, tn), lambda l, j: (l, 0, j)),
        out_shape=jax.ShapeDtypeStruct((depth, rows, n), F32),
        compiler_params=_params("arbitrary", "arbitrary"),
        name="modulation",
    )(cvec, w_mod, b_mod.reshape(depth, 1, n))


def _inproj_kernel(x_ref, mod_ref, w_ref, wgt_ref, qkv_ref, o_ref, zb_ref, zc_ref, zd_ref,
                   ut_ref, g_ref, gt_ref):
    x = x_ref[0]
    sh1 = mod_ref[0, 0:1, :]
    sc1 = mod_ref[0, 1:2, :]
    h = (x * (1.0 + sc1) + sh1).astype(BF16)
    zq = _dot(h, w_ref[:, 0:W_MIX])
    zk = _dot(h, w_ref[:, W_MIX:2 * W_MIX]) * (HEAD_DIM ** -0.5)
    zv = _dot(h, w_ref[:, 2 * W_MIX:3 * W_MIX])
    qkv_ref[0, :, 0:W_MIX] = zq.astype(BF16)
    qkv_ref[0, :, W_MIX:2 * W_MIX] = zk.astype(BF16)
    qkv_ref[0, :, 2 * W_MIX:3 * W_MIX] = zv.astype(BF16)
    o_ref[0] = _dot(h, w_ref[:, C_O:C_B])
    zb_ref[0] = _dot(h, w_ref[:, C_B:C_C])
    zc_ref[0] = _dot(h, w_ref[:, C_C:C_D])
    zd = _dot(h, w_ref[:, C_D:C_G])
    zd_ref[0] = zd
    zdt = zd.T
    for cl in range(TILE // S5_CHUNK):
        blk = zdt[:, cl * S5_CHUNK:(cl + 1) * S5_CHUNK]
        ut_ref[:, cl, 0, :, :] = blk.reshape(S5_GROUPS, S5_P, S5_CHUNK).astype(BF16)
    g_ref[0] = _dot(h, w_ref[:, C_G:P_COLS])
    gt_ref[0] = _dot_nt(wgt_ref[...], h)


def _in_projection(xcat, mod, w_r, wg_t):
    bsz, s, d = xcat.shape
    nt = s // TILE
    nc5 = s // S5_CHUNK
    cpt = TILE // S5_CHUNK
    mod_row = lambda b, j: (jnp.where(j == 0, bsz, b), 0, 0)
    tok = lambda w: pl.BlockSpec((1, TILE, w), lambda b, j: (b, j, 0))
    out_shape = (
        jax.ShapeDtypeStruct((bsz, s, 3 * W_MIX), BF16),
        jax.ShapeDtypeStruct((bsz, s, W_MIX), F32),
        jax.ShapeDtypeStruct((bsz, s, 2 * W_MIX), F32),
        jax.ShapeDtypeStruct((bsz, s, 2 * W_MIX), F32),
        jax.ShapeDtypeStruct((bsz, s, W_MIX), F32),
        jax.ShapeDtypeStruct((S5_GROUPS, nc5, bsz, S5_P, S5_CHUNK), BF16),
        jax.ShapeDtypeStruct((bsz, s, GATE_PAD), F32),
        jax.ShapeDtypeStruct((bsz, N_GATES, s), F32),
    )
    out_specs = (
        tok(3 * W_MIX), tok(W_MIX), tok(2 * W_MIX), tok(2 * W_MIX), tok(W_MIX),
        pl.BlockSpec((S5_GROUPS, cpt, 1, S5_P, S5_CHUNK), lambda b, j: (0, j, b, 0, 0)),
        tok(GATE_PAD),
        pl.BlockSpec((1, N_GATES, TILE), lambda b, j: (b, 0, j)),
    )
    return pl.pallas_call(
        _inproj_kernel,
        grid=(bsz, nt),
        in_specs=[pl.BlockSpec((1, TILE, d), lambda b, j: (b, j, 0)),
                  pl.BlockSpec((1, 6, d), mod_row),
                  _const_spec(w_r.shape),
                  _const_spec(wg_t.shape)],
        out_specs=out_specs,
        out_shape=out_shape,
        compiler_params=_params("arbitrary", "arbitrary"),
        name="in_projection",
    )(xcat, mod, w_r, wg_t)


def _mlstm_kernel(qkv_ref, o_ref, g_ref, gt_ref, gb_ref, gbt_ref, nw_ref, y_ref,
                  hsum_ref, lg_ref, cum_ref, lgt_ref, cumt_ref, ct_ref, nbd_ref, m_ref):
    L = TILE
    n_chunks = qkv_ref.shape[1] // L
    row = lax.broadcasted_iota(jnp.int32, (L, L), 0)
    col = lax.broadcasted_iota(jnp.int32, (L, L), 1)
    tri = (col <= row).astype(BF16)
    lane_g = lax.broadcasted_iota(jnp.int32, (1, GATE_PAD), 1)
    is_f_lane = (lane_g % 8) >= HEADS
    sub_g = lax.broadcasted_iota(jnp.int32, (N_GATES, 1), 0)
    is_f_sub = (sub_g % 8) >= HEADS
    lane_w = lax.broadcasted_iota(jnp.int32, (1, W_MIX), 1) // HEAD_DIM
    sub_w = lax.broadcasted_iota(jnp.int32, (W_MIX, 1), 0) // HEAD_DIM
    bd_mask = sub_w == lane_w
    avg = jnp.where(bd_mask, 1.0 / HEAD_DIM, 0.0).astype(BF16)
    ones_lw = jnp.ones((L, W_MIX), BF16)

    def gate_prologue(c, carry):
        r0 = pl.multiple_of(c * L, L)
        graw = g_ref[0, pl.ds(r0, L), :] + gb_ref[...]
        lg = jnp.where(is_f_lane, _log_sigmoid(graw), graw)
        lg_ref[pl.ds(r0, L), :] = lg
        cum_ref[pl.ds(r0, L), :] = _dot3_r(tri, lg)
        grawt = gt_ref[0, :, pl.ds(r0, L)] + gbt_ref[...]
        lgt = jnp.where(is_f_sub, _log_sigmoid(grawt), grawt)
        lgt_ref[:, pl.ds(r0, L)] = lgt
        hi, mid, lo = _split3(lgt)
        cumt_ref[:, pl.ds(r0, L)] = _dot_nt(hi, tri) + _dot_nt(mid, tri) + _dot_nt(lo, tri)
        return carry

    lax.fori_loop(0, n_chunks, gate_prologue, 0)

    def chunk(c, d):
        r0 = pl.multiple_of(c * L, L)
        q = qkv_ref[0, pl.ds(r0, L), 0:W_MIX]
        k = qkv_ref[0, pl.ds(r0, L), W_MIX:2 * W_MIX]
        v = qkv_ref[0, pl.ds(r0, L), 2 * W_MIX:3 * W_MIX]
        lg = lg_ref[pl.ds(r0, L), :]
        cum = cum_ref[pl.ds(r0, L), :]
        lgt = lgt_ref[:, pl.ds(r0, L)]
        cumt = cumt_ref[:, pl.ds(r0, L)]
        tot = cum[L - 1:L, :]
        if d == 1:
            cum = tot - cum + lg
            cumt = cumt[:, L - 1:L] - cumt + lgt
        mask = (col <= row) if d == 0 else (col >= row)
        m_vec = m_ref[...]
        num = jnp.zeros((L, W_MIX), F32)
        w_inter_full = jnp.zeros((L, W_MIX), F32)
        den_full = jnp.zeros((L, W_MIX), F32)
        floor_full = jnp.zeros((L, W_MIX), F32)
        e_full = jnp.zeros((L, W_MIX), F32)
        decay_col = jnp.zeros((W_MIX, 1), F32)
        m_new_vec = m_vec
        for h in range(HEADS):
            ic = d * 8 + h
            fc = d * 8 + HEADS + h
            head_lanes = lane_w == h
            cc = cum[:, fc:fc + 1]
            m_h = m_vec[:, fc:fc + 1]
            rowv = lgt[ic:ic + 1, :] - cumt[fc:fc + 1, :]
            logw = jnp.where(mask, cc + rowv, -jnp.inf)
            rmax = jnp.max(logw, axis=-1, keepdims=True)
            inter = cc + m_h
            m_t = jnp.maximum(inter, rmax)
            dm = jnp.exp(logw - m_t)
            w_inter = jnp.exp(inter - m_t)
            kh = jnp.where(head_lanes, k, jnp.zeros_like(k))
            vh = jnp.where(head_lanes, v, jnp.zeros_like(v))
            sc = _dot_nt(q, kh) * dm
            den_h = jnp.sum(sc, axis=-1, keepdims=True)
            num = num + _dot(sc.astype(BF16), vh)
            w_inter_full = jnp.where(head_lanes, w_inter, w_inter_full)
            den_full = jnp.where(head_lanes, den_h, den_full)
            floor_full = jnp.where(head_lanes, jnp.exp(-m_t), floor_full)
            tot_h = tot[:, fc:fc + 1]
            logu = tot_h - cc + lg[:, ic:ic + 1]
            m_new = jnp.maximum(tot_h + m_h, jnp.max(logu, axis=0, keepdims=True))
            e_full = jnp.where(head_lanes, jnp.exp(logu - m_new), e_full)
            decay_col = jnp.where(sub_w == h, jnp.exp(tot_h + m_h - m_new), decay_col)
            m_new_vec = jnp.where(lane_g == fc, m_new, m_new_vec)
        ct = ct_ref[...]
        nbd = nbd_ref[...]
        num = num + w_inter_full * _dot(q, ct.astype(BF16))
        den = den_full + w_inter_full * _dot(q, nbd.astype(BF16))
        hval = num / jnp.maximum(jnp.abs(den), floor_full)
        ke = (k.astype(F32) * e_full).astype(BF16)
        ct_ref[...] = decay_col * ct + jnp.where(bd_mask, _dot_tn(ke, v), 0.0)
        nbd_ref[...] = decay_col * nbd + jnp.where(bd_mask, _dot_tn(ke, ones_lw), 0.0)
        m_ref[...] = m_new_vec
        if d == 0:
            hsum_ref[pl.ds(r0, L), :] = hval
        else:
            hs = hsum_ref[pl.ds(r0, L), :] + hval
            mu = _dot3_l(hs, avg)
            dl = hs - mu
            var = _dot3_l(dl * dl, avg)
            hn = dl * lax.rsqrt(var + LN_EPS) * nw_ref[...]
            y = hn * _sigmoid(o_ref[0, pl.ds(r0, L), :])
            y_ref[0, pl.ds(r0, L), :] = y.astype(y_ref.dtype)

    def reset_state():
        ct_ref[...] = jnp.zeros_like(ct_ref)
        nbd_ref[...] = jnp.zeros_like(nbd_ref)
        m_ref[...] = jnp.zeros_like(m_ref)

    reset_state()

    def fwd_body(c, carry):
        chunk(c, 0)
        return carry

    lax.fori_loop(0, n_chunks, fwd_body, 0)
    reset_state()
    chunk(0, 1)

    def bwd_body(i, carry):
        chunk(n_chunks - 1 - i, 1)
        return carry

    lax.fori_loop(0, n_chunks - 1, bwd_body, 0)


def _mlstm(qkv, o, g, gt, gate_bias, norm_w):
    bsz, s, _ = qkv.shape
    gb = jnp.zeros((1, GATE_PAD), F32).at[0, :N_GATES].set(gate_bias)
    gbt = gate_bias.reshape(N_GATES, 1)
    per_b = lambda w: pl.BlockSpec((1, s, w), lambda b: (b, 0, 0))
    return pl.pallas_call(
        _mlstm_kernel,
        grid=(bsz,),
        in_specs=[per_b(3 * W_MIX), per_b(W_MIX), per_b(GATE_PAD),
                  pl.BlockSpec((1, N_GATES, s), lambda b: (b, 0, 0)),
                  pl.BlockSpec((1, GATE_PAD), lambda b: (0, 0)),
                  pl.BlockSpec((N_GATES, 1), lambda b: (0, 0)),
                  pl.BlockSpec((1, W_MIX), lambda b: (0, 0))],
        out_specs=per_b(W_MIX),
        out_shape=jax.ShapeDtypeStruct((bsz, s, W_MIX), BF16),
        scratch_shapes=[pltpu.VMEM((s, W_MIX), F32),
                        pltpu.VMEM((s, GATE_PAD), F32),
                        pltpu.VMEM((s, GATE_PAD), F32),
                        pltpu.VMEM((N_GATES, s), F32),
                        pltpu.VMEM((N_GATES, s), F32),
                        pltpu.VMEM((W_MIX, W_MIX), F32),
                        pltpu.VMEM((W_MIX, W_MIX), F32),
                        pltpu.VMEM((1, GATE_PAD), F32)],
        compiler_params=_params("arbitrary"),
        name="mlstm",
    )(qkv, o, g, gt, gb, gbt, norm_w.reshape(1, W_MIX))


def _sgu_conv_kernel(zb_ref, zc_ref, sgw_ref, sgb_ref, slnw_ref, slnb_ref, cw_ref, cb_ref,
                     clnw_ref, clnb_ref, yb_ref, yc_ref, pad_ref):
    j = pl.program_id(1)
    zb = zb_ref[0]
    u = _gelu(zb[:, 0:W_MIX])
    vn = _layer_norm(_gelu(zb[:, W_MIX:2 * W_MIX]), slnw_ref[...], slnb_ref[...]).astype(BF16)
    lane_grp = lax.broadcasted_iota(jnp.int32, (1, W_MIX), 1) // SGU_GD
    for cl in range(TILE // SGU_CHUNK):
        rows = slice(cl * SGU_CHUNK, (cl + 1) * SGU_CHUNK)
        vc = vn[rows, :]
        acc = sgb_ref[...]
        for gi in range(SGU_GROUPS):
            acc = acc + _dot(sgw_ref[gi], jnp.where(lane_grp == gi, vc, jnp.zeros_like(vc)))
        yb_ref[0, rows, :] = (u[rows, :] * acc).astype(yb_ref.dtype)

    zc = zc_ref[0]
    a = zc[:, 0:W_MIX] * _sigmoid(zc[:, W_MIX:2 * W_MIX])
    pad_ref[...] = jnp.zeros_like(pad_ref)

    def finish(acc, rows):
        yc = _silu(_layer_norm(acc + cb_ref[...], clnw_ref[...], clnb_ref[...]))
        yc_ref[0, rows, :] = yc.astype(yc_ref.dtype)

    @pl.when(j == 0)
    def _():
        pad_ref[pl.ds(16, TILE), :] = a
        acc = jnp.zeros((TILE, W_MIX), F32)
        for t in range(CONV_WIDTH):
            acc = acc + cw_ref[t:t + 1, :] * pad_ref[pl.ds(16 - CONV_PAD + t, TILE), :]
        finish(acc, slice(0, TILE))

    @pl.when(j > 0)
    def _():
        seg = GRID_W + 32
        for r in range(TILE // GRID_W):
            pad_ref[pl.ds(r * seg + 16, GRID_W), :] = a[r * GRID_W:(r + 1) * GRID_W, :]
        for r in range(TILE // GRID_W):
            acc = jnp.zeros((GRID_W, W_MIX), F32)
            for t in range(CONV_WIDTH):
                acc = acc + cw_ref[t:t + 1, :] * pad_ref[pl.ds(r * seg + 16 - CONV_PAD + t, GRID_W), :]
            finish(acc, slice(r * GRID_W, (r + 1) * GRID_W))


def _sgu_conv(zb, zc, sgu_w, sgu_b, sgu_ln_w, sgu_ln_b, conv_w, conv_b, conv_ln_w, conv_ln_b):
    bsz, s, _ = zb.shape
    nt = s // TILE
    bias_full = jnp.repeat(sgu_b.T, SGU_GD, axis=1)
    row = lambda a: a.reshape(1, W_MIX)
    tok = lambda w: pl.BlockSpec((1, TILE, w), lambda b, j: (b, j, 0))
    pad_rows = (TILE // GRID_W) * (GRID_W + 32)
    return pl.pallas_call(
        _sgu_conv_kernel,
        grid=(bsz, nt),
        in_specs=[tok(2 * W_MIX), tok(2 * W_MIX),
                  _const_spec((SGU_GROUPS, SGU_CHUNK, SGU_CHUNK)),
                  _const_spec((SGU_CHUNK, W_MIX)),
                  _const_spec((1, W_MIX)), _const_spec((1, W_MIX)),
                  _const_spec((CONV_WIDTH, W_MIX)), _const_spec((1, W_MIX)),
                  _const_spec((1, W_MIX)), _const_spec((1, W_MIX))],
        out_specs=(tok(W_MIX), tok(W_MIX)),
        out_shape=(jax.ShapeDtypeStruct((bsz, s, W_MIX), BF16),
                   jax.ShapeDtypeStruct((bsz, s, W_MIX), BF16)),
        scratch_shapes=[pltpu.VMEM((pad_rows, W_MIX), F32)],
        compiler_params=_params("arbitrary", "arbitrary"),
        name="sgu_conv",
    )(zb, zc, sgu_w.astype(BF16), bias_full, row(sgu_ln_w), row(sgu_ln_b), conv_w,
      row(conv_b), row(conv_ln_w), row(conv_ln_b))


def _s5_operators(a_re, a_im, log_dt, b_re, b_im, c_re, c_im):
    L = S5_CHUNK
    lam = lax.complex(a_re, a_im)
    lam_dt = lam * jnp.exp(log_dt)[..., None]
    a_bar = jnp.exp(lam_dt)
    b_bar = ((a_bar - 1.0) / lam)[..., None] * lax.complex(b_re, b_im)[None]
    c_mat = lax.complex(c_re, c_im)
    pw = jnp.exp(jnp.arange(L + 1, dtype=F32)[:, None, None, None] * lam_dt[None])
    taps = jnp.real(jnp.einsum('gpn,tdgn,dgnq->dtgpq', c_mat, pw[:L], b_bar))
    k0 = taps[0, 0] + taps[1, 0]
    vec = jnp.concatenate([taps[1, 1:][::-1], k0[None], taps[0, 1:]], axis=0)
    vec = jnp.moveaxis(vec, 0, -1)
    vec = jnp.pad(vec, ((0, 0), (0, 0), (0, 0), (0, 1)))
    flat = jnp.tile(vec, (1, 1, 1, L))[..., :L * (2 * L - 1)]
    toe = flat.reshape(S5_GROUPS, S5_P, S5_P, L, 2 * L - 1)[..., L - 1:]
    kbig = toe.transpose(0, 2, 3, 1, 4).reshape(S5_GROUPS, S5_FLAT, S5_FLAT)
    sl_f = pw[:L][::-1, 0][:, :, :, None] * b_bar[0][None]
    sl_b = pw[:L, 1][:, :, :, None] * b_bar[1][None]
    def state_in(m):
        return m.transpose(1, 3, 0, 2).reshape(S5_GROUPS, S5_FLAT, S5_N)
    w_loc = jnp.concatenate([jnp.real(state_in(sl_f)), jnp.imag(state_in(sl_f)),
                             jnp.real(state_in(sl_b)), jnp.imag(state_in(sl_b))], axis=-1)
    ro_f = c_mat[None] * jnp.moveaxis(pw[1:, 0], 0, 0)[:, :, None, :]
    ro_b = c_mat[None] * pw[1:, 1][::-1][:, :, None, :]
    def state_out(m):
        return m.transpose(1, 3, 2, 0).reshape(S5_GROUPS, S5_N, S5_FLAT)
    w_out = jnp.concatenate([jnp.real(state_out(ro_f)), -jnp.imag(state_out(ro_f)),
                             jnp.real(state_out(ro_b)), -jnp.imag(state_out(ro_b))], axis=1)
    al = pw[L]
    re, im = jnp.real(al), jnp.imag(al)
    z = jnp.zeros_like(re[0])
    rows = [jnp.concatenate([re[0], re[0]], -1), jnp.concatenate([-im[0], im[0]], -1),
            jnp.concatenate([re[1], re[1]], -1), jnp.concatenate([-im[1], im[1]], -1)]
    rows += [jnp.concatenate([z, z], -1)] * 4
    a_rows = jnp.stack(rows, axis=1)
    return kbig.astype(BF16), w_loc.astype(BF16), w_out.astype(BF16), a_rows


def _s5_kernel(u_ref, kb_ref, wloc_ref, wout_ref, ar_ref, y_ref, sin_ref, *, bsz, n_ctx_chunks):
    n = pl.program_id(1)
    n_chunks = u_ref.shape[1] // bsz
    two_n = 2 * S5_N

    @pl.when(n == 0)
    def _():
        sloc = _dot(u_ref[0], wloc_ref[0])
        ar = ar_ref[0]
        fwd_order = list(range(n_chunks))
        bwd_order = (list(range(n_ctx_chunks - 1, -1, -1)) +
                     list(range(n_chunks - 1, n_ctx_chunks - 1, -1)))
        for d, order in ((0, fwd_order), (1, bwd_order)):
            a1 = ar[2 * d:2 * d + 1, :]
            a2 = ar[2 * d + 1:2 * d + 2, :]
            st = jnp.zeros((bsz, two_n), F32)
            for c in order:
                sin_ref[c * bsz:(c + 1) * bsz, d * two_n:(d + 1) * two_n] = st.astype(BF16)
                loc = sloc[c * bsz:(c + 1) * bsz, d * two_n:(d + 1) * two_n]
                st = a1 * st + a2 * pltpu.roll(st, S5_N, 1) + loc

    y_ref[0] = _dot(u_ref[0], kb_ref[0]) + _dot(sin_ref[...], wout_ref[0])


def _s5_mix(ut, kbig, w_loc, w_out, a_rows, bsz):
    g, rows, flat = ut.shape
    tn = 512
    kern = functools.partial(_s5_kernel, bsz=bsz, n_ctx_chunks=CTX_LEN // S5_CHUNK)
    return pl.pallas_call(
        kern,
        grid=(g, flat // tn),
        in_specs=[pl.BlockSpec((1, rows, flat), lambda gi, n: (gi, 0, 0)),
                  pl.BlockSpec((1, flat, tn), lambda gi, n: (gi, 0, n)),
                  pl.BlockSpec((1, flat, 4 * S5_N), lambda gi, n: (gi, 0, 0)),
                  pl.BlockSpec((1, 4 * S5_N, tn), lambda gi, n: (gi, 0, n)),
                  pl.BlockSpec((1, 8, 2 * S5_N), lambda gi, n: (gi, 0, 0))],
        out_specs=pl.BlockSpec((1, rows, tn), lambda gi, n: (gi, 0, n)),
        out_shape=jax.ShapeDtypeStruct((g, rows, flat), F32),
        scratch_shapes=[pltpu.VMEM((rows, 4 * S5_N), BF16)],
        compiler_params=_params("arbitrary", "arbitrary"),
        name="s5_mix",
    )(ut, kbig, w_loc, w_out, a_rows)


def _s5_readout_kernel(y_ref, zd_ref, d_ref, gw_ref, gb_ref, o_ref):
    for cl in range(TILE // S5_CHUNK):
        rows = slice(cl * S5_CHUNK, (cl + 1) * S5_CHUNK)
        yt = y_ref[:, cl, 0, :, :].reshape(W_MIX, S5_CHUNK)
        y = yt.T + d_ref[...] * zd_ref[0, rows, :]
        y = _gelu(y)
        gate = _sigmoid(_dot(y.astype(BF16), gw_ref[...]) + gb_ref[...])
        o_ref[0, rows, :] = (y * gate).astype(o_ref.dtype)


def _s5_readout(y5, zd, d_skip, glu_w, glu_b):
    bsz, s, _ = zd.shape
    nt = s // TILE
    cpt = TILE // S5_CHUNK
    return pl.pallas_call(
        _s5_readout_kernel,
        grid=(bsz, nt),
        in_specs=[pl.BlockSpec((S5_GROUPS, cpt, 1, S5_P, S5_CHUNK), lambda b, j: (0, j, b, 0, 0)),
                  pl.BlockSpec((1, TILE, W_MIX), lambda b, j: (b, j, 0)),
                  _const_spec((1, W_MIX)), _const_spec((W_MIX, W_MIX)), _const_spec((1, W_MIX))],
        out_specs=pl.BlockSpec((1, TILE, W_MIX), lambda b, j: (b, j, 0)),
        out_shape=jax.ShapeDtypeStruct((bsz, s, W_MIX), BF16),
        compiler_params=_params("arbitrary", "arbitrary"),
        name="s5_readout",
    )(y5, zd, d_skip.reshape(1, W_MIX), glu_w.astype(BF16), glu_b.reshape(1, W_MIX))


def _out_ffn_kernel(x_ref, ya_ref, yb_ref, yc_ref, yd_ref, mod_ref, wo_ref, l1w_ref, l1b_ref,
                    wg_ref, wu_ref, wd_ref, l2w_ref, l2b_ref, o_ref, acc_ref):
    x = x_ref[0]
    g1 = mod_ref[0, 2:3, :]
    sh2 = mod_ref[0, 3:4, :]
    sc2 = mod_ref[0, 4:5, :]
    g2 = mod_ref[0, 5:6, :]
    y = (_dot(ya_ref[0], wo_ref[0:W_MIX, :]) + _dot(yb_ref[0], wo_ref[W_MIX:2 * W_MIX, :]) +
         _dot(yc_ref[0], wo_ref[2 * W_MIX:3 * W_MIX, :]) + _dot(yd_ref[0], wo_ref[3 * W_MIX:4 * W_MIX, :]))
    x1 = _layer_norm(ALPHA * x + g1 * y, l1w_ref[...], l1b_ref[...])
    h2 = (x1 * (1.0 + sc2) + sh2).astype(BF16)
    acc_ref[...] = jnp.zeros_like(acc_ref)

    def body(kk, carry):
        gk = _dot(h2, wg_ref[kk])
        uk = _dot(h2, wu_ref[kk])
        acc_ref[...] += _dot((_silu(gk) * uk).astype(BF16), wd_ref[kk])
        return carry

    lax.fori_loop(0, D_FF // FF_CHUNK, body, 0)
    o_ref[0] = _layer_norm(ALPHA * x1 + g2 * acc_ref[...], l2w_ref[...], l2b_ref[...])


def _out_ffn(xcat, ya, yb, yc, yd, mod, w_out, ln1_w, ln1_b, w_g, w_u, w_d, ln2_w, ln2_b):
    bsz, s, d = xcat.shape
    nt = s // TILE
    mod_row = lambda b, j: (jnp.where(j == 0, bsz, b), 0, 0)
    tok = lambda w: pl.BlockSpec((1, TILE, w), lambda b, j: (b, j, 0))
    row = lambda a: a.reshape(1, d)
    return pl.pallas_call(
        _out_ffn_kernel,
        grid=(bsz, nt),
        in_specs=[tok(d), tok(W_MIX), tok(W_MIX), tok(W_MIX), tok(W_MIX),
                  pl.BlockSpec((1, 6, d), mod_row),
                  _const_spec(w_out.shape), _const_spec((1, d)), _const_spec((1, d)),
                  _const_spec(w_g.shape), _const_spec(w_u.shape), _const_spec(w_d.shape),
                  _const_spec((1, d)), _const_spec((1, d))],
        out_specs=tok(d),
        out_shape=jax.ShapeDtypeStruct((bsz, s, d), F32),
        scratch_shapes=[pltpu.VMEM((TILE, d), F32)],
        compiler_params=_params("arbitrary", "arbitrary"),
        name="out_ffn",
    )(xcat, ya, yb, yc, yd, mod, w_out, row(ln1_w), row(ln1_b), w_g, w_u, w_d, row(ln2_w), row(ln2_b))


def _reorder_w_in(w):
    off_b = 4 * W_MIX + N_GATES
    gates = w[:, 4 * W_MIX:off_b]
    w_r = jnp.concatenate([w[:, :4 * W_MIX], w[:, off_b:],
                           gates, jnp.zeros((w.shape[0], GATE_PAD - N_GATES), w.dtype)], axis=1)
    return w_r.astype(BF16), gates.T.astype(BF16)


def kernel(x, c, ctx, c_ctx, w_mod, b_mod, w_in, mlstm_gate_bias, mlstm_norm_w, sgu_ln_w, sgu_ln_b, sgu_w, sgu_b, conv_w, conv_b, conv_ln_w, conv_ln_b, s5_a_re, s5_a_im, s5_log_dt, s5_b_re, s5_b_im, s5_c_re, s5_c_im, s5_d, s5_glu_w, s5_glu_b, w_out, ln1_w, ln1_b, w_ffn_in, w_ffn_out, ln2_w, ln2_b):
    bsz, seq, d = x.shape
    s = CTX_LEN + seq
    xcat = jnp.concatenate([ctx, x], axis=1)
    mod_rows = -(-(bsz + 1) // 8) * 8
    cvec = jnp.zeros((mod_rows, d), F32).at[:bsz].set(c).at[bsz].set(c_ctx)
    mod_all = _modulation(cvec, w_mod, b_mod).reshape(DEPTH, mod_rows, 6, d)
    n_ff = D_FF // FF_CHUNK
    for l in range(DEPTH):
        mod = mod_all[l]
        w_r, wg_t = _reorder_w_in(w_in[l])
        qkv, o, zb, zc, zd, ut, g, gt = _in_projection(xcat, mod, w_r, wg_t)
        ya = _mlstm(qkv, o, g, gt, mlstm_gate_bias[l], mlstm_norm_w[l])
        yb, yc = _sgu_conv(zb, zc, sgu_w[l], sgu_b[l], sgu_ln_w[l], sgu_ln_b[l],
                           conv_w[l], conv_b[l], conv_ln_w[l], conv_ln_b[l])
        kbig, w_loc, w_so, a_rows = _s5_operators(s5_a_re[l], s5_a_im[l], s5_log_dt[l], s5_b_re[l],
                                                  s5_b_im[l], s5_c_re[l], s5_c_im[l])
        nc5 = s // S5_CHUNK
        y5 = _s5_mix(ut.reshape(S5_GROUPS, nc5 * bsz, S5_FLAT), kbig, w_loc, w_so, a_rows, bsz)
        yd = _s5_readout(y5.reshape(S5_GROUPS, nc5, bsz, S5_P, S5_CHUNK), zd, s5_d[l],
                         s5_glu_w[l], s5_glu_b[l])
        wf = w_ffn_in[l].astype(BF16)
        w_g = wf[:, :D_FF].reshape(d, n_ff, FF_CHUNK).transpose(1, 0, 2)
        w_u = wf[:, D_FF:].reshape(d, n_ff, FF_CHUNK).transpose(1, 0, 2)
        w_d = w_ffn_out[l].astype(BF16).reshape(n_ff, FF_CHUNK, d)
        xcat = _out_ffn(xcat, ya, yb, yc, yd, mod, w_out[l].astype(BF16), ln1_w[l], ln1_b[l],
                        w_g, w_u, w_d, ln2_w[l], ln2_b[l])
    return xcat[:, CTX_LEN:, :]
```

```python
import functools
import math

import jax
import jax.numpy as jnp
from jax import lax
from jax.experimental import pallas as pl
from jax.experimental.pallas import tpu as pltpu

F32 = jnp.float32
BF16 = jnp.bfloat16

D_MODEL = 1024
DEPTH = 2
CTX_LEN = 256
GRID_W = 64
W_MIX = 256
HEADS = 4
HEAD_DIM = W_MIX // HEADS
SGU_GROUPS = 4
SGU_GD = W_MIX // SGU_GROUPS
SGU_CHUNK = 128
CONV_WIDTH = 31
CONV_PAD = CONV_WIDTH // 2
S5_P = 16
S5_GROUPS = W_MIX // S5_P
S5_N = 64
S5_CHUNK = 128
S5_FLAT = S5_P * S5_CHUNK
D_FF = 2816
FF_CHUNK = 256
ALPHA = (2 * DEPTH) ** 0.25
LN_EPS = 1e-5
TILE = 256
N_GATES = 4 * HEADS
GATE_PAD = 128
C_QKV = 0
C_O = 3 * W_MIX
C_B = 4 * W_MIX
C_C = C_B + 2 * W_MIX
C_D = C_C + 2 * W_MIX
C_G = C_D + W_MIX
P_COLS = C_G + GATE_PAD
VMEM_LIMIT = 56 * 1024 * 1024


def _dot(a, b):
    return jnp.dot(a, b, preferred_element_type=F32)


def _dot_nt(a, b):
    return lax.dot_general(a, b, (((1,), (1,)), ((), ())), preferred_element_type=F32)


def _dot_tn(a, b):
    return lax.dot_general(a, b, (((0,), (0,)), ((), ())), preferred_element_type=F32)


def _split3(x):
    hi = x.astype(BF16)
    r = x - hi.astype(F32)
    mid = r.astype(BF16)
    lo = (r - mid.astype(F32)).astype(BF16)
    return hi, mid, lo


def _dot3_l(x, w_bf16):
    hi, mid, lo = _split3(x)
    return _dot(hi, w_bf16) + _dot(mid, w_bf16) + _dot(lo, w_bf16)


def _dot3_r(w_bf16, x):
    hi, mid, lo = _split3(x)
    return _dot(w_bf16, hi) + _dot(w_bf16, mid) + _dot(w_bf16, lo)


def _layer_norm(x, w, b):
    mu = jnp.mean(x, axis=-1, keepdims=True)
    d = x - mu
    var = jnp.mean(d * d, axis=-1, keepdims=True)
    return d * lax.rsqrt(var + LN_EPS) * w + b


def _sigmoid(x):
    return jax.nn.sigmoid(x)


def _silu(x):
    return x * jax.nn.sigmoid(x)


def _gelu(x):
    return jax.nn.gelu(x, approximate=True)


def _log_sigmoid(x):
    return jnp.minimum(x, 0.0) - jnp.log1p(jnp.exp(-jnp.abs(x)))


def _params(*sem):
    return pltpu.CompilerParams(dimension_semantics=sem, vmem_limit_bytes=VMEM_LIMIT)


def _const_spec(shape):
    nd = len(shape)
    return pl.BlockSpec(shape, lambda *_: (0,) * nd, pipeline_mode=pl.Buffered(1))


def _mod_kernel(c_ref, w_ref, b_ref, o_ref):
    s = _silu(c_ref[...])
    s_hi = s.astype(BF16)
    s_lo = (s - s_hi.astype(F32)).astype(BF16)
    w = w_ref[0]
    w_hi = w.astype(BF16)
    w_lo = (w - w_hi.astype(F32)).astype(BF16)
    o_ref[0] = _dot(s_hi, w_hi) + _dot(s_hi, w_lo) + _dot(s_lo, w_hi) + b_ref[0]


def _modulation(cvec, w_mod, b_mod):
    depth, d, n = w_mod.shape
    rows = cvec.shape[0]
    tn = 1024
    return pl.pallas_call(
        _mod_kernel,
        grid=(depth, n // tn),
        in_specs=[pl.BlockSpec((rows, d), lambda l, j: (0, 0)),
                  pl.BlockSpec((1, d, tn), lambda l, j: (l, 0, j)),
                  pl.BlockSpec((1, 1, tn), lambda l, j: (l, 0, j))],
        out_specs=pl.BlockSpec((1, rows, tn), lambda l, j: (l, 0, j)),
        out_shape=jax.ShapeDtypeStruct((depth, rows, n), F32),
        compiler_params=_params("arbitrary", "arbitrary"),
        name="modulation",
    )(cvec, w_mod, b_mod.reshape(depth, 1, n))


def _inproj_kernel(x_ref, mod_ref, w_ref, wgt_ref, qkv_ref, o_ref, zb_ref, zc_ref, zd_ref,
                   ut_ref, g_ref, gt_ref):
    x = x_ref[0]
    sh1 = mod_ref[0, 0:1, :]
    sc1 = mod_ref[0, 1:2, :]
    h = (x * (1.0 + sc1) + sh1).astype(BF16)
    zq = _dot(h, w_ref[:, 0:W_MIX])
    zk = _dot(h, w_ref[:, W_MIX:2 * W_MIX]) * (HEAD_DIM ** -0.5)
    zv = _dot(h, w_ref[:, 2 * W_MIX:3 * W_MIX])
    qkv_ref[0, :, 0:W_MIX] = zq.astype(BF16)
    qkv_ref[0, :, W_MIX:2 * W_MIX] = zk.astype(BF16)
    qkv_ref[0, :, 2 * W_MIX:3 * W_MIX] = zv.astype(BF16)
    o_ref[0] = _dot(h, w_ref[:, C_O:C_B])
    zb_ref[0] = _dot(h, w_ref[:, C_B:C_C])
    zc_ref[0] = _dot(h, w_ref[:, C_C:C_D])
    zd = _dot(h, w_ref[:, C_D:C_G])
    zd_ref[0] = zd
    zdt = zd.T
    for cl in range(TILE // S5_CHUNK):
        blk = zdt[:, cl * S5_CHUNK:(cl + 1) * S5_CHUNK]
        ut_ref[:, cl, 0, :, :] = blk.reshape(S5_GROUPS, S5_P, S5_CHUNK).astype(BF16)
    g_ref[0] = _dot(h, w_ref[:, C_G:P_COLS])
    gt_ref[0] = _dot_nt(wgt_ref[...], h)


def _in_projection(xcat, mod, w_r, wg_t):
    bsz, s, d = xcat.shape
    nt = s // TILE
    nc5 = s // S5_CHUNK
    cpt = TILE // S5_CHUNK
    mod_row = lambda b, j: (jnp.where(j == 0, bsz, b), 0, 0)
    tok = lambda w: pl.BlockSpec((1, TILE, w), lambda b, j: (b, j, 0))
    out_shape = (
        jax.ShapeDtypeStruct((bsz, s, 3 * W_MIX), BF16),
        jax.ShapeDtypeStruct((bsz, s, W_MIX), F32),
        jax.ShapeDtypeStruct((bsz, s, 2 * W_MIX), F32),
        jax.ShapeDtypeStruct((bsz, s, 2 * W_MIX), F32),
        jax.ShapeDtypeStruct((bsz, s, W_MIX), F32),
        jax.ShapeDtypeStruct((S5_GROUPS, nc5, bsz, S5_P, S5_CHUNK), BF16),
        jax.ShapeDtypeStruct((bsz, s, GATE_PAD), F32),
        jax.ShapeDtypeStruct((bsz, N_GATES, s), F32),
    )
    out_specs = (
        tok(3 * W_MIX), tok(W_MIX), tok(2 * W_MIX), tok(2 * W_MIX), tok(W_MIX),
        pl.BlockSpec((S5_GROUPS, cpt, 1, S5_P, S5_CHUNK), lambda b, j: (0, j, b, 0, 0)),
        tok(GATE_PAD),
        pl.BlockSpec((1, N_GATES, TILE), lambda b, j: (b, 0, j)),
    )
    return pl.pallas_call(
        _inproj_kernel,
        grid=(bsz, nt),
        in_specs=[pl.BlockSpec((1, TILE, d), lambda b, j: (b, j, 0)),
                  pl.BlockSpec((1, 6, d), mod_row),
                  _const_spec(w_r.shape),
                  _const_spec(wg_t.shape)],
        out_specs=out_specs,
        out_shape=out_shape,
        compiler_params=_params("arbitrary", "arbitrary"),
        name="in_projection",
    )(xcat, mod, w_r, wg_t)


def _mlstm_kernel(qkv_ref, o_ref, g_ref, gt_ref, gb_ref, gbt_ref, nw_ref, y_ref,
                  hsum_ref, lg_ref, cum_ref, lgt_ref, cumt_ref, ct_ref, nbd_ref, m_ref):
    L = TILE
    n_chunks = qkv_ref.shape[1] // L
    row = lax.broadcasted_iota(jnp.int32, (L, L), 0)
    col = lax.broadcasted_iota(jnp.int32, (L, L), 1)
    tri = (col <= row).astype(BF16)
    lane_g = lax.broadcasted_iota(jnp.int32, (1, GATE_PAD), 1)
    is_f_lane = (lane_g % 8) >= HEADS
    sub_g = lax.broadcasted_iota(jnp.int32, (N_GATES, 1), 0)
    is_f_sub = (sub_g % 8) >= HEADS
    lane_w = lax.broadcasted_iota(jnp.int32, (1, W_MIX), 1) // HEAD_DIM
    sub_w = lax.broadcasted_iota(jnp.int32, (W_MIX, 1), 0) // HEAD_DIM
    bd_mask = sub_w == lane_w
    avg = jnp.where(bd_mask, 1.0 / HEAD_DIM, 0.0).astype(BF16)
    ones_lw = jnp.ones((L, W_MIX), BF16)

    def gate_prologue(c, carry):
        r0 = pl.multiple_of(c * L, L)
        graw = g_ref[0, pl.ds(r0, L), :] + gb_ref[...]
        lg = jnp.where(is_f_lane, _log_sigmoid(graw), graw)
        lg_ref[pl.ds(r0, L), :] = lg
        cum_ref[pl.ds(r0, L), :] = _dot3_r(tri, lg)
        grawt = gt_ref[0, :, pl.ds(r0, L)] + gbt_ref[...]
        lgt = jnp.where(is_f_sub, _log_sigmoid(grawt), grawt)
        lgt_ref[:, pl.ds(r0, L)] = lgt
        hi, mid, lo = _split3(lgt)
        cumt_ref[:, pl.ds(r0, L)] = _dot_nt(hi, tri) + _dot_nt(mid, tri) + _dot_nt(lo, tri)
        return carry

    lax.fori_loop(0, n_chunks, gate_prologue, 0)

    def chunk(c, d):
        r0 = pl.multiple_of(c * L, L)
        q = qkv_ref[0, pl.ds(r0, L), 0:W_MIX]
        k = qkv_ref[0, pl.ds(r0, L), W_MIX:2 * W_MIX]
        v = qkv_ref[0, pl.ds(r0, L), 2 * W_MIX:3 * W_MIX]
        lg = lg_ref[pl.ds(r0, L), :]
        cum = cum_ref[pl.ds(r0, L), :]
        lgt = lgt_ref[:, pl.ds(r0, L)]
        cumt = cumt_ref[:, pl.ds(r0, L)]
        tot = cum[L - 1:L, :]
        if d == 1:
            cum = tot - cum + lg
            cumt = cumt[:, L - 1:L] - cumt + lgt
        mask = (col <= row) if d == 0 else (col >= row)
        m_vec = m_ref[...]
        num = jnp.zeros((L, W_MIX), F32)
        w_inter_full = jnp.zeros((L, W_MIX), F32)
        den_full = jnp.zeros((L, W_MIX), F32)
        floor_full = jnp.zeros((L, W_MIX), F32)
        e_full = jnp.zeros((L, W_MIX), F32)
        decay_col = jnp.zeros((W_MIX, 1), F32)
        m_new_vec = m_vec
        for h in range(HEADS):
            ic = d * 8 + h
            fc = d * 8 + HEADS + h
            head_lanes = lane_w == h
            cc = cum[:, fc:fc + 1]
            m_h = m_vec[:, fc:fc + 1]
            rowv = lgt[ic:ic + 1, :] - cumt[fc:fc + 1, :]
            logw = jnp.where(mask, cc + rowv, -jnp.inf)
            rmax = jnp.max(logw, axis=-1, keepdims=True)
            inter = cc + m_h
            m_t = jnp.maximum(inter, rmax)
            dm = jnp.exp(logw - m_t)
            w_inter = jnp.exp(inter - m_t)
            kh = jnp.where(head_lanes, k, jnp.zeros_like(k))
            vh = jnp.where(head_lanes, v, jnp.zeros_like(v))
            sc = _dot_nt(q, kh) * dm
            den_h = jnp.sum(sc, axis=-1, keepdims=True)
            num = num + _dot(sc.astype(BF16), vh)
            w_inter_full = jnp.where(head_lanes, w_inter, w_inter_full)
            den_full = jnp.where(head_lanes, den_h, den_full)
            floor_full = jnp.where(head_lanes, jnp.exp(-m_t), floor_full)
            tot_h = tot[:, fc:fc + 1]
            logu = tot_h - cc + lg[:, ic:ic + 1]
            m_new = jnp.maximum(tot_h + m_h, jnp.max(logu, axis=0, keepdims=True))
            e_full = jnp.where(head_lanes, jnp.exp(logu - m_new), e_full)
            decay_col = jnp.where(sub_w == h, jnp.exp(tot_h + m_h - m_new), decay_col)
            m_new_vec = jnp.where(lane_g == fc, m_new, m_new_vec)
        ct = ct_ref[...]
        nbd = nbd_ref[...]
        num = num + w_inter_full * _dot(q, ct.astype(BF16))
        den = den_full + w_inter_full * _dot(q, nbd.astype(BF16))
        hval = num / jnp.maximum(jnp.abs(den), floor_full)
        ke = (k.astype(F32) * e_full).astype(BF16)
        ct_ref[...] = decay_col * ct + jnp.where(bd_mask, _dot_tn(ke, v), 0.0)
        nbd_ref[...] = decay_col * nbd + jnp.where(bd_mask, _dot_tn(ke, ones_lw), 0.0)
        m_ref[...] = m_new_vec
        if d == 0:
            hsum_ref[pl.ds(r0, L), :] = hval
        else:
            hs = hsum_ref[pl.ds(r0, L), :] + hval
            mu = _dot3_l(hs, avg)
            dl = hs - mu
            var = _dot3_l(dl * dl, avg)
            hn = dl * lax.rsqrt(var + LN_EPS) * nw_ref[...]
            y = hn * _sigmoid(o_ref[0, pl.ds(r0, L), :])
            y_ref[0, pl.ds(r0, L), :] = y.astype(y_ref.dtype)

    def reset_state():
        ct_ref[...] = jnp.zeros_like(ct_ref)
        nbd_ref[...] = jnp.zeros_like(nbd_ref)
        m_ref[...] = jnp.zeros_like(m_ref)

    reset_state()

    def fwd_body(c, carry):
        chunk(c, 0)
        return carry

    lax.fori_loop(0, n_chunks, fwd_body, 0)
    reset_state()
    chunk(0, 1)

    def bwd_body(i, carry):
        chunk(n_chunks - 1 - i, 1)
        return carry

    lax.fori_loop(0, n_chunks - 1, bwd_body, 0)


def _mlstm(qkv, o, g, gt, gate_bias, norm_w):
    bsz, s, _ = qkv.shape
    gb = jnp.zeros((1, GATE_PAD), F32).at[0, :N_GATES].set(gate_bias)
    gbt = gate_bias.reshape(N_GATES, 1)
    per_b = lambda w: pl.BlockSpec((1, s, w), lambda b: (b, 0, 0))
    return pl.pallas_call(
        _mlstm_kernel,
        grid=(bsz,),
        in_specs=[per_b(3 * W_MIX), per_b(W_MIX), per_b(GATE_PAD),
                  pl.BlockSpec((1, N_GATES, s), lambda b: (b, 0, 0)),
                  pl.BlockSpec((1, GATE_PAD), lambda b: (0, 0)),
                  pl.BlockSpec((N_GATES, 1), lambda b: (0, 0)),
                  pl.BlockSpec((1, W_MIX), lambda b: (0, 0))],
        out_specs=per_b(W_MIX),
        out_shape=jax.ShapeDtypeStruct((bsz, s, W_MIX), BF16),
        scratch_shapes=[pltpu.VMEM((s, W_MIX), F32),
                        pltpu.VMEM((s, GATE_PAD), F32),
                        pltpu.VMEM((s, GATE_PAD), F32),
                        pltpu.VMEM((N_GATES, s), F32),
                        pltpu.VMEM((N_GATES, s), F32),
                        pltpu.VMEM((W_MIX, W_MIX), F32),
                        pltpu.VMEM((W_MIX, W_MIX), F32),
                        pltpu.VMEM((1, GATE_PAD), F32)],
        compiler_params=_params("arbitrary"),
        name="mlstm",
    )(qkv, o, g, gt, gb, gbt, norm_w.reshape(1, W_MIX))


def _sgu_conv_kernel(zb_ref, zc_ref, sgw_ref, sgb_ref, slnw_ref, slnb_ref, cw_ref, cb_ref,
                     clnw_ref, clnb_ref, yb_ref, yc_ref, pad_ref):
    j = pl.program_id(1)
    zb = zb_ref[0]
    u = _gelu(zb[:, 0:W_MIX])
    vn = _layer_norm(_gelu(zb[:, W_MIX:2 * W_MIX]), slnw_ref[...], slnb_ref[...]).astype(BF16)
    lane_grp = lax.broadcasted_iota(jnp.int32, (1, W_MIX), 1) // SGU_GD
    for cl in range(TILE // SGU_CHUNK):
        rows = slice(cl * SGU_CHUNK, (cl + 1) * SGU_CHUNK)
        vc = vn[rows, :]
        acc = sgb_ref[...]
        for gi in range(SGU_GROUPS):
            acc = acc + _dot(sgw_ref[gi], jnp.where(lane_grp == gi, vc, jnp.zeros_like(vc)))
        yb_ref[0, rows, :] = (u[rows, :] * acc).astype(yb_ref.dtype)

    zc = zc_ref[0]
    a = zc[:, 0:W_MIX] * _sigmoid(zc[:, W_MIX:2 * W_MIX])
    pad_ref[...] = jnp.zeros_like(pad_ref)

    def finish(acc, rows):
        yc = _silu(_layer_norm(acc + cb_ref[...], clnw_ref[...], clnb_ref[...]))
        yc_ref[0, rows, :] = yc.astype(yc_ref.dtype)

    @pl.when(j == 0)
    def _():
        pad_ref[pl.ds(16, TILE), :] = a
        acc = jnp.zeros((TILE, W_MIX), F32)
        for t in range(CONV_WIDTH):
            acc = acc + cw_ref[t:t + 1, :] * pad_ref[pl.ds(16 - CONV_PAD + t, TILE), :]
        finish(acc, slice(0, TILE))

    @pl.when(j > 0)
    def _():
        seg = GRID_W + 32
        for r in range(TILE // GRID_W):
            pad_ref[pl.ds(r * seg + 16, GRID_W), :] = a[r * GRID_W:(r + 1) * GRID_W, :]
        for r in range(TILE // GRID_W):
            acc = jnp.zeros((GRID_W, W_MIX), F32)
            for t in range(CONV_WIDTH):
                acc = acc + cw_ref[t:t + 1, :] * pad_ref[pl.ds(r * seg + 16 - CONV_PAD + t, GRID_W), :]
            finish(acc, slice(r * GRID_W, (r + 1) * GRID_W))


def _sgu_conv(zb, zc, sgu_w, sgu_b, sgu_ln_w, sgu_ln_b, conv_w, conv_b, conv_ln_w, conv_ln_b):
    bsz, s, _ = zb.shape
    nt = s // TILE
    bias_full = jnp.repeat(sgu_b.T, SGU_GD, axis=1)
    row = lambda a: a.reshape(1, W_MIX)
    tok = lambda w: pl.BlockSpec((1, TILE, w), lambda b, j: (b, j, 0))
    pad_rows = (TILE // GRID_W) * (GRID_W + 32)
    return pl.pallas_call(
        _sgu_conv_kernel,
        grid=(bsz, nt),
        in_specs=[tok(2 * W_MIX), tok(2 * W_MIX),
                  _const_spec((SGU_GROUPS, SGU_CHUNK, SGU_CHUNK)),
                  _const_spec((SGU_CHUNK, W_MIX)),
                  _const_spec((1, W_MIX)), _const_spec((1, W_MIX)),
                  _const_spec((CONV_WIDTH, W_MIX)), _const_spec((1, W_MIX)),
                  _const_spec((1, W_MIX)), _const_spec((1, W_MIX))],
        out_specs=(tok(W_MIX), tok(W_MIX)),
        out_shape=(jax.ShapeDtypeStruct((bsz, s, W_MIX), BF16),
                   jax.ShapeDtypeStruct((bsz, s, W_MIX), BF16)),
        scratch_shapes=[pltpu.VMEM((pad_rows, W_MIX), F32)],
        compiler_params=_params("arbitrary", "arbitrary"),
        name="sgu_conv",
    )(zb, zc, sgu_w.astype(BF16), bias_full, row(sgu_ln_w), row(sgu_ln_b), conv_w,
      row(conv_b), row(conv_ln_w), row(conv_ln_b))


def _cpow(ar, ai, ld, k):
    dt = jnp.exp(ld)
    mag = jnp.exp(k * (ar * dt))
    ang = k * (ai * dt)
    return mag * jnp.cos(ang), mag * jnp.sin(ang)


def _zoh_coef(ar, ai, ld):
    er, ei = _cpow(ar, ai, ld, 1.0)
    den = ar * ar + ai * ai
    return ((er - 1.0) * ar + ei * ai) / den, (ei * ar - (er - 1.0) * ai) / den


def _dot_x3(a, b):
    a_hi, a_mid, _ = _split3(a)
    b_hi, b_mid, _ = _split3(b)
    return _dot(a_hi, b_hi) + _dot(a_hi, b_mid) + _dot(a_mid, b_hi)


def _s5_ops_kernel(cols_ref, rows_ref, btre_ref, btim_ref, bx_ref, bsw_ref, cre_ref, cim_ref,
                   ctre_ref, ctim_ref, kb_ref, wloc_ref, wout_ref, ar_ref,
                   mt_ref, pt_ref, pw_ref, v_ref):
    L = S5_CHUNK
    i = pl.program_id(1)
    lane2 = lax.broadcasted_iota(jnp.int32, (1, 2 * S5_N), 1)

    @pl.when(i == 0)
    def _():
        col = lambda k: cols_ref[0, :, k:k + 1]
        row = lambda k: rows_ref[0, k:k + 1, :]
        cf = [_zoh_coef(col(3 * d), col(3 * d + 1), col(3 * d + 2)) for d in range(2)]
        jj = lax.broadcasted_iota(jnp.int32, (1, 2 * L), 1)
        fwd = jj >= L
        k = jnp.abs(jj - L).astype(F32)
        sel = lambda a, b: jnp.where(fwd, a, b)
        pre, pim = _cpow(sel(col(0), col(3)), sel(col(1), col(4)), sel(col(2), col(5)), k)
        cre_, cim_ = sel(cf[0][0], cf[1][0]), sel(cf[0][1], cf[1][1])
        m_re = pre * cre_ - pim * cim_
        m_im = pre * cim_ + pim * cre_
        mt_ref[0] = m_re + jnp.where(jj == L, cf[1][0], 0.0)
        mt_ref[1] = m_im + jnp.where(jj == L, cf[1][1], 0.0)
        tt = lax.broadcasted_iota(jnp.int32, (1, L), 1)
        ss = lax.broadcasted_iota(jnp.int32, (L, 1), 0)
        for d in range(2):
            kk = (tt + 1 if d == 0 else L - tt).astype(F32)
            pt_ref[d, 0], pt_ref[d, 1] = _cpow(col(3 * d), col(3 * d + 1), col(3 * d + 2), kk)
            ks = (L - 1 - ss if d == 0 else ss).astype(F32)
            qre, qim = _cpow(row(3 * d), row(3 * d + 1), row(3 * d + 2), ks)
            fre, fim = _zoh_coef(row(3 * d), row(3 * d + 1), row(3 * d + 2))
            pw_ref[d, 0] = qre * fre - qim * fim
            pw_ref[d, 1] = qre * fim + qim * fre
            lre, lim = _cpow(row(3 * d), row(3 * d + 1), row(3 * d + 2), float(L))
            ar_ref[0, 2 * d:2 * d + 1, :] = lre
            ar_ref[0, 2 * d + 1:2 * d + 2, :] = jnp.where(lane2 < S5_N, -lim, lim)
        ar_ref[0, 4:8, :] = jnp.zeros((4, 2 * S5_N), F32)

    cre = cre_ref[0, pl.ds(i, 1), :]
    cim = cim_ref[0, pl.ds(i, 1), :]
    g_re = cre * btre_ref[0] - cim * btim_ref[0]
    g_im = cre * btim_ref[0] + cim * btre_ref[0]
    v_ref[...] = _dot_x3(g_re, mt_ref[0]) - _dot_x3(g_im, mt_ref[1])
    for q in range(S5_P):
        vq = jnp.broadcast_to(v_ref[q:q + 1, :], (L, 2 * L))
        toe = pltpu.roll(vq, 0, 1, stride=1, stride_axis=0)
        kb_ref[0, q * L:(q + 1) * L, :] = toe[:, L:2 * L].astype(BF16)

    lane_p = lax.broadcasted_iota(jnp.int32, (1, S5_P), 1)
    pick = lambda ref: jnp.sum(jnp.where(lane_p == i, ref[0], 0.0), axis=1, keepdims=True)
    ccol_re, ccol_im = pick(ctre_ref), pick(ctim_ref)
    bx = bx_ref[0, pl.ds(i, 1), :]
    bsw = bsw_ref[0, pl.ds(i, 1), :]
    by = jnp.where(lane2 < S5_N, -bsw, bsw)
    for d in range(2):
        pre, pim = pt_ref[d, 0], pt_ref[d, 1]
        base = d * 2 * S5_N
        wout_ref[0, base:base + S5_N, :] = (ccol_re * pre - ccol_im * pim).astype(BF16)
        wout_ref[0, base + S5_N:base + 2 * S5_N, :] = (-(ccol_re * pim + ccol_im * pre)).astype(BF16)
        wloc_ref[0, :, base:base + 2 * S5_N] = (pw_ref[d, 0] * bx + pw_ref[d, 1] * by).astype(BF16)


def _s5_operators(a_re, a_im, log_dt, b_re, b_im, c_re, c_im):
    L = S5_CHUNK
    g, n2 = S5_GROUPS, 2 * S5_N
    ld = jnp.broadcast_to(log_dt[:, :, None], a_re.shape)
    per_dir = [a_re[0], a_im[0], ld[0], a_re[1], a_im[1], ld[1]]
    zero = jnp.zeros_like(a_re[0])
    cols = jnp.stack(per_dir + [zero, zero], axis=-1)
    dup = lambda v: jnp.concatenate([v, v], axis=-1)
    rows = jnp.stack([dup(v) for v in per_dir + [zero, zero]], axis=1)
    bt_re, bt_im = b_re.transpose(0, 2, 1), b_im.transpose(0, 2, 1)
    bx = jnp.concatenate([bt_re, bt_im], axis=-1)
    bsw = jnp.concatenate([bt_im, bt_re], axis=-1)
    ct_re, ct_im = c_re.transpose(0, 2, 1), c_im.transpose(0, 2, 1)
    per_g = lambda a: pl.BlockSpec((1,) + a.shape[1:], lambda gi, i: (gi, 0, 0))
    ins = (cols, rows, bt_re, bt_im, bx, bsw, c_re, c_im, ct_re, ct_im)
    return pl.pallas_call(
        _s5_ops_kernel,
        grid=(g, S5_P),
        in_specs=[per_g(a) for a in ins],
        out_specs=(pl.BlockSpec((1, S5_FLAT, L), lambda gi, i: (gi, 0, i)),
                   pl.BlockSpec((1, L, 2 * n2), lambda gi, i: (gi, i, 0)),
                   pl.BlockSpec((1, 2 * n2, L), lambda gi, i: (gi, 0, i)),
                   pl.BlockSpec((1, 8, n2), lambda gi, i: (gi, 0, 0))),
        out_shape=(jax.ShapeDtypeStruct((g, S5_FLAT, S5_FLAT), BF16),
                   jax.ShapeDtypeStruct((g, S5_FLAT, 2 * n2), BF16),
                   jax.ShapeDtypeStruct((g, 2 * n2, S5_FLAT), BF16),
                   jax.ShapeDtypeStruct((g, 8, n2), F32)),
        scratch_shapes=[pltpu.VMEM((2, S5_N, 2 * L), F32),
                        pltpu.VMEM((2, 2, S5_N, L), F32),
                        pltpu.VMEM((2, 2, L, n2), F32),
                        pltpu.VMEM((S5_P, 2 * L), F32)],
        compiler_params=_params("arbitrary", "arbitrary"),
        name="s5_operators",
    )(*ins)


def _s5_kernel(u_ref, kb_ref, wloc_ref, wout_ref, ar_ref, y_ref, sin_ref, *, bsz, n_ctx_chunks):
    n = pl.program_id(1)
    n_chunks = u_ref.shape[1] // bsz
    two_n = 2 * S5_N

    @pl.when(n == 0)
    def _():
        sloc = _dot(u_ref[0], wloc_ref[0])
        ar = ar_ref[0]
        fwd_order = list(range(n_chunks))
        bwd_order = (list(range(n_ctx_chunks - 1, -1, -1)) +
                     list(range(n_chunks - 1, n_ctx_chunks - 1, -1)))
        for d, order in ((0, fwd_order), (1, bwd_order)):
            a1 = ar[2 * d:2 * d + 1, :]
            a2 = ar[2 * d + 1:2 * d + 2, :]
            st = jnp.zeros((bsz, two_n), F32)
            for c in order:
                sin_ref[c * bsz:(c + 1) * bsz, d * two_n:(d + 1) * two_n] = st.astype(BF16)
                loc = sloc[c * bsz:(c + 1) * bsz, d * two_n:(d + 1) * two_n]
                st = a1 * st + a2 * pltpu.roll(st, S5_N, 1) + loc

    y_ref[0] = _dot(u_ref[0], kb_ref[0]) + _dot(sin_ref[...], wout_ref[0])


def _s5_mix(ut, kbig, w_loc, w_out, a_rows, bsz):
    g, rows, flat = ut.shape
    tn = 512
    kern = functools.partial(_s5_kernel, bsz=bsz, n_ctx_chunks=CTX_LEN // S5_CHUNK)
    return pl.pallas_call(
        kern,
        grid=(g, flat // tn),
        in_specs=[pl.BlockSpec((1, rows, flat), lambda gi, n: (gi, 0, 0)),
                  pl.BlockSpec((1, flat, tn), lambda gi, n: (gi, 0, n)),
                  pl.BlockSpec((1, flat, 4 * S5_N), lambda gi, n: (gi, 0, 0)),
                  pl.BlockSpec((1, 4 * S5_N, tn), lambda gi, n: (gi, 0, n)),
                  pl.BlockSpec((1, 8, 2 * S5_N), lambda gi, n: (gi, 0, 0))],
        out_specs=pl.BlockSpec((1, rows, tn), lambda gi, n: (gi, 0, n)),
        out_shape=jax.ShapeDtypeStruct((g, rows, flat), F32),
        scratch_shapes=[pltpu.VMEM((rows, 4 * S5_N), BF16)],
        compiler_params=_params("arbitrary", "arbitrary"),
        name="s5_mix",
    )(ut, kbig, w_loc, w_out, a_rows)


def _s5_readout_kernel(y_ref, zd_ref, d_ref, gw_ref, gb_ref, o_ref):
    for cl in range(TILE // S5_CHUNK):
        rows = slice(cl * S5_CHUNK, (cl + 1) * S5_CHUNK)
        yt = y_ref[:, cl, 0, :, :].reshape(W_MIX, S5_CHUNK)
        y = yt.T + d_ref[...] * zd_ref[0, rows, :]
        y = _gelu(y)
        gate = _sigmoid(_dot(y.astype(BF16), gw_ref[...]) + gb_ref[...])
        o_ref[0, rows, :] = (y * gate).astype(o_ref.dtype)


def _s5_readout(y5, zd, d_skip, glu_w, glu_b):
    bsz, s, _ = zd.shape
    nt = s // TILE
    cpt = TILE // S5_CHUNK
    return pl.pallas_call(
        _s5_readout_kernel,
        grid=(bsz, nt),
        in_specs=[pl.BlockSpec((S5_GROUPS, cpt, 1, S5_P, S5_CHUNK), lambda b, j: (0, j, b, 0, 0)),
                  pl.BlockSpec((1, TILE, W_MIX), lambda b, j: (b, j, 0)),
                  _const_spec((1, W_MIX)), _const_spec((W_MIX, W_MIX)), _const_spec((1, W_MIX))],
        out_specs=pl.BlockSpec((1, TILE, W_MIX), lambda b, j: (b, j, 0)),
        out_shape=jax.ShapeDtypeStruct((bsz, s, W_MIX), BF16),
        compiler_params=_params("arbitrary", "arbitrary"),
        name="s5_readout",
    )(y5, zd, d_skip.reshape(1, W_MIX), glu_w.astype(BF16), glu_b.reshape(1, W_MIX))


def _out_ffn_kernel(x_ref, ya_ref, yb_ref, yc_ref, yd_ref, mod_ref, wo_ref, l1w_ref, l1b_ref,
                    wg_ref, wu_ref, wd_ref, l2w_ref, l2b_ref, o_ref, acc_ref):
    x = x_ref[0]
    g1 = mod_ref[0, 2:3, :]
    sh2 = mod_ref[0, 3:4, :]
    sc2 = mod_ref[0, 4:5, :]
    g2 = mod_ref[0, 5:6, :]
    y = (_dot(ya_ref[0], wo_ref[0:W_MIX, :]) + _dot(yb_ref[0], wo_ref[W_MIX:2 * W_MIX, :]) +
         _dot(yc_ref[0], wo_ref[2 * W_MIX:3 * W_MIX, :]) + _dot(yd_ref[0], wo_ref[3 * W_MIX:4 * W_MIX, :]))
    x1 = _layer_norm(ALPHA * x + g1 * y, l1w_ref[...], l1b_ref[...])
    h2 = (x1 * (1.0 + sc2) + sh2).astype(BF16)
    acc_ref[...] = jnp.zeros_like(acc_ref)

    def body(kk, carry):
        gk = _dot(h2, wg_ref[kk])
        uk = _dot(h2, wu_ref[kk])
        acc_ref[...] += _dot((_silu(gk) * uk).astype(BF16), wd_ref[kk])
        return carry

    lax.fori_loop(0, D_FF // FF_CHUNK, body, 0)
    o_ref[0] = _layer_norm(ALPHA * x1 + g2 * acc_ref[...], l2w_ref[...], l2b_ref[...])


def _out_ffn(xcat, ya, yb, yc, yd, mod, w_out, ln1_w, ln1_b, w_g, w_u, w_d, ln2_w, ln2_b):
    bsz, s, d = xcat.shape
    nt = s // TILE
    mod_row = lambda b, j: (jnp.where(j == 0, bsz, b), 0, 0)
    tok = lambda w: pl.BlockSpec((1, TILE, w), lambda b, j: (b, j, 0))
    row = lambda a: a.reshape(1, d)
    return pl.pallas_call(
        _out_ffn_kernel,
        grid=(bsz, nt),
        in_specs=[tok(d), tok(W_MIX), tok(W_MIX), tok(W_MIX), tok(W_MIX),
                  pl.BlockSpec((1, 6, d), mod_row),
                  _const_spec(w_out.shape), _const_spec((1, d)), _const_spec((1, d)),
                  _const_spec(w_g.shape), _const_spec(w_u.shape), _const_spec(w_d.shape),
                  _const_spec((1, d)), _const_spec((1, d))],
        out_specs=tok(d),
        out_shape=jax.ShapeDtypeStruct((bsz, s, d), F32),
        scratch_shapes=[pltpu.VMEM((TILE, d), F32)],
        compiler_params=_params("arbitrary", "arbitrary"),
        name="out_ffn",
    )(xcat, ya, yb, yc, yd, mod, w_out, row(ln1_w), row(ln1_b), w_g, w_u, w_d, row(ln2_w), row(ln2_b))


def _reorder_w_in(w):
    off_b = 4 * W_MIX + N_GATES
    gates = w[:, 4 * W_MIX:off_b]
    w_r = jnp.concatenate([w[:, :4 * W_MIX], w[:, off_b:],
                           gates, jnp.zeros((w.shape[0], GATE_PAD - N_GATES), w.dtype)], axis=1)
    return w_r.astype(BF16), gates.T.astype(BF16)


def kernel(x, c, ctx, c_ctx, w_mod, b_mod, w_in, mlstm_gate_bias, mlstm_norm_w, sgu_ln_w, sgu_ln_b, sgu_w, sgu_b, conv_w, conv_b, conv_ln_w, conv_ln_b, s5_a_re, s5_a_im, s5_log_dt, s5_b_re, s5_b_im, s5_c_re, s5_c_im, s5_d, s5_glu_w, s5_glu_b, w_out, ln1_w, ln1_b, w_ffn_in, w_ffn_out, ln2_w, ln2_b):
    bsz, seq, d = x.shape
    s = CTX_LEN + seq
    xcat = jnp.concatenate([ctx, x], axis=1)
    mod_rows = -(-(bsz + 1) // 8) * 8
    cvec = jnp.zeros((mod_rows, d), F32).at[:bsz].set(c).at[bsz].set(c_ctx)
    mod_all = _modulation(cvec, w_mod, b_mod).reshape(DEPTH, mod_rows, 6, d)
    n_ff = D_FF // FF_CHUNK
    for l in range(DEPTH):
        mod = mod_all[l]
        w_r, wg_t = _reorder_w_in(w_in[l])
        qkv, o, zb, zc, zd, ut, g, gt = _in_projection(xcat, mod, w_r, wg_t)
        ya = _mlstm(qkv, o, g, gt, mlstm_gate_bias[l], mlstm_norm_w[l])
        yb, yc = _sgu_conv(zb, zc, sgu_w[l], sgu_b[l], sgu_ln_w[l], sgu_ln_b[l],
                           conv_w[l], conv_b[l], conv_ln_w[l], conv_ln_b[l])
        kbig, w_loc, w_so, a_rows = _s5_operators(s5_a_re[l], s5_a_im[l], s5_log_dt[l], s5_b_re[l],
                                                  s5_b_im[l], s5_c_re[l], s5_c_im[l])
        nc5 = s // S5_CHUNK
        y5 = _s5_mix(ut.reshape(S5_GROUPS, nc5 * bsz, S5_FLAT), kbig, w_loc, w_so, a_rows, bsz)
        yd = _s5_readout(y5.reshape(S5_GROUPS, nc5, bsz, S5_P, S5_CHUNK), zd, s5_d[l],
                         s5_glu_w[l], s5_glu_b[l])
        wf = w_ffn_in[l].astype(BF16)
        w_g = wf[:, :D_FF].reshape(d, n_ff, FF_CHUNK).transpose(1, 0, 2)
        w_u = wf[:, D_FF:].reshape(d, n_ff, FF_CHUNK).transpose(1, 0, 2)
        w_d = w_ffn_out[l].astype(BF16).reshape(n_ff, FF_CHUNK, d)
        xcat = _out_ffn(xcat, ya, yb, yc, yd, mod, w_out[l].astype(BF16), ln1_w[l], ln1_b[l],
                        w_g, w_u, w_d, ln2_w[l], ln2_b[l])
    return xcat[:, CTX_LEN:, :]
```

```python
import functools

import jax
import jax.numpy as jnp
from jax import lax
from jax.experimental import pallas as pl
from jax.experimental.pallas import tpu as pltpu

F32 = jnp.float32
BF16 = jnp.bfloat16

D_MODEL = 1024
DEPTH = 2
CTX_LEN = 256
GRID_W = 64
W_MIX = 256
HEADS = 4
HEAD_DIM = W_MIX // HEADS
SGU_GROUPS = 4
SGU_GD = W_MIX // SGU_GROUPS
SGU_CHUNK = 128
CONV_WIDTH = 31
CONV_PAD = CONV_WIDTH // 2
S5_P = 16
S5_GROUPS = W_MIX // S5_P
S5_N = 64
S5_CHUNK = 128
S5_FLAT = S5_P * S5_CHUNK
D_FF = 2816
FF_CHUNK = 256
ALPHA = (2 * DEPTH) ** 0.25
LN_EPS = 1e-5
TILE = 256
TM_MAIN = 3 * TILE
TM_LATENT = 2 * TILE
N_GATES = 4 * HEADS
GATE_PAD = 128
C_QKV = 0
C_O = 3 * W_MIX
C_B = 4 * W_MIX
C_C = C_B + 2 * W_MIX
C_D = C_C + 2 * W_MIX
C_G = C_D + W_MIX
P_COLS = C_G + GATE_PAD
VMEM_LIMIT = 56 * 1024 * 1024


def _dot(a, b):
    return jnp.dot(a, b, preferred_element_type=F32)


def _dot_nt(a, b):
    return lax.dot_general(a, b, (((1,), (1,)), ((), ())), preferred_element_type=F32)


def _dot_tn(a, b):
    return lax.dot_general(a, b, (((0,), (0,)), ((), ())), preferred_element_type=F32)


def _split3(x):
    hi = x.astype(BF16)
    r = x - hi.astype(F32)
    mid = r.astype(BF16)
    lo = (r - mid.astype(F32)).astype(BF16)
    return hi, mid, lo


def _dot3_l(x, w_bf16):
    hi, mid, lo = _split3(x)
    return _dot(hi, w_bf16) + _dot(mid, w_bf16) + _dot(lo, w_bf16)


def _dot3_r(w_bf16, x):
    hi, mid, lo = _split3(x)
    return _dot(w_bf16, hi) + _dot(w_bf16, mid) + _dot(w_bf16, lo)


def _layer_norm(x, w, b):
    mu = jnp.mean(x, axis=-1, keepdims=True)
    d = x - mu
    var = jnp.mean(d * d, axis=-1, keepdims=True)
    return d * lax.rsqrt(var + LN_EPS) * w + b


def _sigmoid(x):
    return jax.nn.sigmoid(x)


def _silu(x):
    return x * jax.nn.sigmoid(x)


def _gelu(x):
    return jax.nn.gelu(x, approximate=True)


def _log_sigmoid(x):
    return jnp.minimum(x, 0.0) - jnp.log1p(jnp.exp(-jnp.abs(x)))


def _params(*sem):
    return pltpu.CompilerParams(dimension_semantics=sem, vmem_limit_bytes=VMEM_LIMIT)


def _const_spec(shape):
    nd = len(shape)
    return pl.BlockSpec(shape, lambda *_: (0,) * nd, pipeline_mode=pl.Buffered(1))


def _mod_kernel(c_ref, w_ref, b_ref, o_ref):
    s = _silu(c_ref[...])
    s_hi = s.astype(BF16)
    s_lo = (s - s_hi.astype(F32)).astype(BF16)
    w = w_ref[0]
    w_hi = w.astype(BF16)
    w_lo = (w - w_hi.astype(F32)).astype(BF16)
    o_ref[0] = _dot(s_hi, w_hi) + _dot(s_hi, w_lo) + _dot(s_lo, w_hi) + b_ref[0]


def _modulation(cvec, w_mod, b_mod):
    depth, d, n = w_mod.shape
    rows = cvec.shape[0]
    tn = 1024
    return pl.pallas_call(
        _mod_kernel,
        grid=(depth, n // tn),
        in_specs=[pl.BlockSpec((rows, d), lambda l, j: (0, 0)),
                  pl.BlockSpec((1, d, tn), lambda l, j: (l, 0, j)),
                  pl.BlockSpec((1, 1, tn), lambda l, j: (l, 0, j))],
        out_specs=pl.BlockSpec((1, rows, tn), lambda l, j: (l, 0, j)),
        out_shape=jax.ShapeDtypeStruct((depth, rows, n), F32),
        compiler_params=_params("arbitrary", "arbitrary"),
        name="modulation",
    )(cvec, w_mod, b_mod.reshape(depth, 1, n))


def _tile_mod(modb_ref, modc_ref, i):
    last = pl.program_id(1) == pl.num_programs(1) - 1
    row_b = modb_ref[0, i:i + 1, :]
    return row_b, jnp.where(last, modc_ref[0, i:i + 1, :], row_b)


def _inproj_kernel(x_ref, modb_ref, modc_ref, w_ref, wgt_ref, k_ref, qt_ref, vt_ref, ot_ref,
                   zb_ref, zc_ref, zd_ref, ut_ref, g_ref, gt_ref, h_ref):
    tm = x_ref.shape[1]
    sh_b, sh_c = _tile_mod(modb_ref, modc_ref, 0)
    sc_b, sc_c = _tile_mod(modb_ref, modc_ref, 1)
    n_lat = tm - CTX_LEN
    h_ref[0:n_lat, :] = (x_ref[0, 0:n_lat, :] * (1.0 + sc_b) + sh_b).astype(BF16)
    h_ref[n_lat:tm, :] = (x_ref[0, n_lat:tm, :] * (1.0 + sc_c) + sh_c).astype(BF16)
    for r in range(tm // TILE):
        rows = slice(r * TILE, (r + 1) * TILE)
        h = h_ref[rows, :]
        qt_ref[0, :, rows] = _dot(h, w_ref[:, 0:W_MIX]).T.astype(BF16)
        k_ref[0, rows, :] = (_dot(h, w_ref[:, W_MIX:2 * W_MIX]) * (HEAD_DIM ** -0.5)).astype(BF16)
        vt_ref[0, :, rows] = _dot(h, w_ref[:, 2 * W_MIX:3 * W_MIX]).T.astype(BF16)
        ot_ref[0, :, rows] = _dot(h, w_ref[:, C_O:C_B]).T
        zb_ref[0, rows, :] = _dot(h, w_ref[:, C_B:C_C])
        zc_ref[0, rows, :] = _dot(h, w_ref[:, C_C:C_D])
        zd = _dot(h, w_ref[:, C_D:C_G])
        zd_ref[0, rows, :] = zd
        zdt = zd.T
        for cl in range(TILE // S5_CHUNK):
            blk = zdt[:, cl * S5_CHUNK:(cl + 1) * S5_CHUNK]
            ut_ref[:, r * (TILE // S5_CHUNK) + cl, 0, :, :] = (
                blk.reshape(S5_GROUPS, S5_P, S5_CHUNK).astype(BF16))
        g_ref[0, rows, :] = _dot(h, w_ref[:, C_G:P_COLS])
        gt_ref[0, :, rows] = _dot_nt(wgt_ref[...], h)


def _in_projection(xcat, mod, w_r, wg_t):
    bsz, s, d = xcat.shape
    tm = TM_MAIN
    nc5 = s // S5_CHUNK
    cpt = tm // S5_CHUNK
    tok = lambda w: pl.BlockSpec((1, tm, w), lambda b, j: (b, j, 0))
    chan = lambda c: pl.BlockSpec((1, c, tm), lambda b, j: (b, 0, j))
    out_shape = (
        jax.ShapeDtypeStruct((bsz, s, W_MIX), BF16),
        jax.ShapeDtypeStruct((bsz, W_MIX, s), BF16),
        jax.ShapeDtypeStruct((bsz, W_MIX, s), BF16),
        jax.ShapeDtypeStruct((bsz, W_MIX, s), F32),
        jax.ShapeDtypeStruct((bsz, s, 2 * W_MIX), F32),
        jax.ShapeDtypeStruct((bsz, s, 2 * W_MIX), F32),
        jax.ShapeDtypeStruct((bsz, s, W_MIX), F32),
        jax.ShapeDtypeStruct((S5_GROUPS, nc5, bsz, S5_P, S5_CHUNK), BF16),
        jax.ShapeDtypeStruct((bsz, s, GATE_PAD), F32),
        jax.ShapeDtypeStruct((bsz, N_GATES, s), F32),
    )
    out_specs = (
        tok(W_MIX), chan(W_MIX), chan(W_MIX), chan(W_MIX), tok(2 * W_MIX), tok(2 * W_MIX), tok(W_MIX),
        pl.BlockSpec((S5_GROUPS, cpt, 1, S5_P, S5_CHUNK), lambda b, j: (0, j, b, 0, 0)),
        tok(GATE_PAD), chan(N_GATES),
    )
    return pl.pallas_call(
        _inproj_kernel,
        grid=(bsz, s // tm),
        in_specs=[tok(d),
                  pl.BlockSpec((1, 6, d), lambda b, j: (b, 0, 0)),
                  pl.BlockSpec((1, 6, d), lambda b, j: (bsz, 0, 0)),
                  _const_spec(w_r.shape),
                  _const_spec(wg_t.shape)],
        out_specs=out_specs,
        out_shape=out_shape,
        scratch_shapes=[pltpu.VMEM((tm, d), BF16)],
        compiler_params=_params("arbitrary", "arbitrary"),
        name="in_projection",
    )(xcat, mod, mod, w_r, wg_t)


def _mlstm_kernel(k_ref, qt_ref, vt_ref, ot_ref, g_ref, gt_ref, gb_ref, gbt_ref, nw_ref, y_ref,
                  ht_ref, lg_ref, cum_ref, lgt_ref, cumt_ref, c_ref, n_ref, m_ref):
    L = TILE
    n_chunks = k_ref.shape[1] // L
    row = lax.broadcasted_iota(jnp.int32, (L, L), 0)
    col = lax.broadcasted_iota(jnp.int32, (L, L), 1)
    tri = (col <= row).astype(BF16)
    lane_g = lax.broadcasted_iota(jnp.int32, (1, GATE_PAD), 1)
    is_f_lane = (lane_g % 8) >= HEADS
    sub_g = lax.broadcasted_iota(jnp.int32, (N_GATES, 1), 0)
    is_f_sub = (sub_g % 8) >= HEADS
    lane_w = lax.broadcasted_iota(jnp.int32, (1, W_MIX), 1) // HEAD_DIM
    sub_w = lax.broadcasted_iota(jnp.int32, (W_MIX, 1), 0) // HEAD_DIM
    bd_mask = sub_w == lane_w
    n_mask = sub_g == lane_w

    def gate_prologue(c, carry):
        r0 = pl.multiple_of(c * L, L)
        graw = g_ref[0, pl.ds(r0, L), :] + gb_ref[...]
        lg = jnp.where(is_f_lane, _log_sigmoid(graw), graw)
        lg_ref[pl.ds(r0, L), :] = lg
        cum_ref[pl.ds(r0, L), :] = _dot3_r(tri, lg)
        grawt = gt_ref[0, :, pl.ds(r0, L)] + gbt_ref[...]
        lgt = jnp.where(is_f_sub, _log_sigmoid(grawt), grawt)
        lgt_ref[:, pl.ds(r0, L)] = lgt
        hi, mid, lo = _split3(lgt)
        cumt_ref[:, pl.ds(r0, L)] = _dot_nt(hi, tri) + _dot_nt(mid, tri) + _dot_nt(lo, tri)
        return carry

    lax.fori_loop(0, n_chunks, gate_prologue, 0)

    def chunk(c, d):
        r0 = pl.multiple_of(c * L, L)
        k = k_ref[0, pl.ds(r0, L), :]
        qt = qt_ref[0, :, pl.ds(r0, L)]
        vt = vt_ref[0, :, pl.ds(r0, L)]
        lg = lg_ref[pl.ds(r0, L), :]
        cum = cum_ref[pl.ds(r0, L), :]
        lgt = lgt_ref[:, pl.ds(r0, L)]
        cumt = cumt_ref[:, pl.ds(r0, L)]
        tot = cumt[:, L - 1:L]
        if d == 1:
            cum = cum[L - 1:L, :] - cum + lg
            cumt = tot - cumt + lgt
        mask = (row <= col) if d == 0 else (row >= col)
        m_vec = m_ref[...]
        c_old = c_ref[...]
        n_old = n_ref[...]
        qc = _dot(c_old.astype(BF16), qt)
        qn = _dot(n_old.astype(BF16), qt)
        e_slabs = []
        e_small = jnp.zeros((N_GATES, L), F32)
        decay_row = jnp.zeros((1, W_MIX), F32)
        m_new_vec = m_vec
        for h in range(HEADS):
            ic = d * 8 + h
            fc = d * 8 + HEADS + h
            head_lanes = lane_w == h
            sl = slice(h * HEAD_DIM, (h + 1) * HEAD_DIM)
            m_h = m_vec[:, fc:fc + 1]
            src = lg[:, ic:ic + 1] - cum[:, fc:fc + 1]
            cct = cumt[fc:fc + 1, :]
            logw = jnp.where(mask, src + cct, -jnp.inf)
            inter = cct + m_h
            m_t = jnp.maximum(inter, jnp.max(logw, axis=0, keepdims=True))
            w_inter = jnp.exp(inter - m_t)
            kh = jnp.where(head_lanes, k, jnp.zeros_like(k))
            sc = _dot(kh, qt) * jnp.exp(logw - m_t)
            den = jnp.sum(sc, axis=0, keepdims=True) + w_inter * qn[h:h + 1, :]
            num = _dot(vt[sl, :], sc.astype(BF16)) + w_inter * qc[sl, :]
            hv = num / jnp.maximum(jnp.abs(den), jnp.exp(-m_t))
            if d == 0:
                ht_ref[sl, pl.ds(r0, L)] = hv
            else:
                hs = ht_ref[sl, pl.ds(r0, L)] + hv
                mu = jnp.mean(hs, axis=0, keepdims=True)
                dl = hs - mu
                var = jnp.mean(dl * dl, axis=0, keepdims=True)
                hn = dl * lax.rsqrt(var + LN_EPS) * nw_ref[sl, :]
                ht_ref[sl, pl.ds(r0, L)] = hn * _sigmoid(ot_ref[0, sl, pl.ds(r0, L)])
            tot_h = tot[fc:fc + 1, :]
            logu = tot_h - cct + lgt[ic:ic + 1, :]
            m_new = jnp.maximum(tot_h + m_h, jnp.max(logu, axis=1, keepdims=True))
            e_h = jnp.exp(logu - m_new)
            e_slabs.append(jnp.broadcast_to(e_h, (HEAD_DIM, L)))
            e_small = jnp.where(sub_g == h, e_h, e_small)
            decay_row = jnp.where(head_lanes, jnp.exp(tot_h + m_h - m_new), decay_row)
            m_new_vec = jnp.where(lane_g == fc, m_new, m_new_vec)
        vet = (vt.astype(F32) * jnp.concatenate(e_slabs, axis=0)).astype(BF16)
        c_ref[...] = decay_row * c_old + jnp.where(bd_mask, _dot(vet, k), 0.0)
        n_ref[...] = decay_row * n_old + jnp.where(n_mask, _dot(e_small.astype(BF16), k), 0.0)
        m_ref[...] = m_new_vec
        if d == 1:
            y_ref[0, pl.ds(r0, L), :] = ht_ref[:, pl.ds(r0, L)].T.astype(y_ref.dtype)

    def reset_state():
        c_ref[...] = jnp.zeros_like(c_ref)
        n_ref[...] = jnp.zeros_like(n_ref)
        m_ref[...] = jnp.zeros_like(m_ref)

    ctx_chunk = n_chunks - 1
    reset_state()
    chunk(ctx_chunk, 0)

    def fwd_body(c, carry):
        chunk(c, 0)
        return carry

    lax.fori_loop(0, ctx_chunk, fwd_body, 0)
    reset_state()
    chunk(ctx_chunk, 1)

    def bwd_body(i, carry):
        chunk(ctx_chunk - 1 - i, 1)
        return carry

    lax.fori_loop(0, ctx_chunk, bwd_body, 0)


def _mlstm(k, qt, vt, ot, g, gt, gate_bias, norm_w):
    bsz, s, _ = k.shape
    gb = jnp.zeros((1, GATE_PAD), F32).at[0, :N_GATES].set(gate_bias)
    gbt = gate_bias.reshape(N_GATES, 1)
    nw_full = jnp.broadcast_to(norm_w.reshape(W_MIX, 1), (W_MIX, TILE))
    tok = lambda w: pl.BlockSpec((1, s, w), lambda b: (b, 0, 0))
    chan = lambda c: pl.BlockSpec((1, c, s), lambda b: (b, 0, 0))
    return pl.pallas_call(
        _mlstm_kernel,
        grid=(bsz,),
        in_specs=[tok(W_MIX), chan(W_MIX), chan(W_MIX), chan(W_MIX), tok(GATE_PAD), chan(N_GATES),
                  pl.BlockSpec((1, GATE_PAD), lambda b: (0, 0)),
                  pl.BlockSpec((N_GATES, 1), lambda b: (0, 0)),
                  pl.BlockSpec((W_MIX, TILE), lambda b: (0, 0))],
        out_specs=tok(W_MIX),
        out_shape=jax.ShapeDtypeStruct((bsz, s, W_MIX), BF16),
        scratch_shapes=[pltpu.VMEM((W_MIX, s), F32),
                        pltpu.VMEM((s, GATE_PAD), F32),
                        pltpu.VMEM((s, GATE_PAD), F32),
                        pltpu.VMEM((N_GATES, s), F32),
                        pltpu.VMEM((N_GATES, s), F32),
                        pltpu.VMEM((W_MIX, W_MIX), F32),
                        pltpu.VMEM((N_GATES, W_MIX), F32),
                        pltpu.VMEM((1, GATE_PAD), F32)],
        compiler_params=_params("arbitrary"),
        name="mlstm",
    )(k, qt, vt, ot, g, gt, gb, gbt, nw_full)


def _sgu_conv_kernel(zb_ref, zc_ref, sgw_ref, sgb_ref, slnw_ref, slnb_ref, cw_ref, cb_ref,
                     clnw_ref, clnb_ref, yb_ref, yc_ref, pad_ref):
    is_ctx = pl.program_id(1) == pl.num_programs(1) - 1
    zb = zb_ref[0]
    u = _gelu(zb[:, 0:W_MIX])
    vn = _layer_norm(_gelu(zb[:, W_MIX:2 * W_MIX]), slnw_ref[...], slnb_ref[...]).astype(BF16)
    lane_grp = lax.broadcasted_iota(jnp.int32, (1, W_MIX), 1) // SGU_GD
    for cl in range(TILE // SGU_CHUNK):
        rows = slice(cl * SGU_CHUNK, (cl + 1) * SGU_CHUNK)
        vc = vn[rows, :]
        acc = sgb_ref[...]
        for gi in range(SGU_GROUPS):
            acc = acc + _dot(sgw_ref[gi], jnp.where(lane_grp == gi, vc, jnp.zeros_like(vc)))
        yb_ref[0, rows, :] = (u[rows, :] * acc).astype(yb_ref.dtype)

    zc = zc_ref[0]
    a = zc[:, 0:W_MIX] * _sigmoid(zc[:, W_MIX:2 * W_MIX])
    pad_ref[...] = jnp.zeros_like(pad_ref)

    def finish(acc, rows):
        yc = _silu(_layer_norm(acc + cb_ref[...], clnw_ref[...], clnb_ref[...]))
        yc_ref[0, rows, :] = yc.astype(yc_ref.dtype)

    @pl.when(is_ctx)
    def _():
        pad_ref[pl.ds(16, TILE), :] = a
        acc = jnp.zeros((TILE, W_MIX), F32)
        for t in range(CONV_WIDTH):
            acc = acc + cw_ref[t:t + 1, :] * pad_ref[pl.ds(16 - CONV_PAD + t, TILE), :]
        finish(acc, slice(0, TILE))

    @pl.when(jnp.logical_not(is_ctx))
    def _():
        seg = GRID_W + 32
        for r in range(TILE // GRID_W):
            pad_ref[pl.ds(r * seg + 16, GRID_W), :] = a[r * GRID_W:(r + 1) * GRID_W, :]
        for r in range(TILE // GRID_W):
            acc = jnp.zeros((GRID_W, W_MIX), F32)
            for t in range(CONV_WIDTH):
                acc = acc + cw_ref[t:t + 1, :] * pad_ref[pl.ds(r * seg + 16 - CONV_PAD + t, GRID_W), :]
            finish(acc, slice(r * GRID_W, (r + 1) * GRID_W))


def _sgu_conv(zb, zc, sgu_w, sgu_b, sgu_ln_w, sgu_ln_b, conv_w, conv_b, conv_ln_w, conv_ln_b):
    bsz, s, _ = zb.shape
    nt = s // TILE
    bias_full = jnp.repeat(sgu_b.T, SGU_GD, axis=1)
    row = lambda a: a.reshape(1, W_MIX)
    tok = lambda w: pl.BlockSpec((1, TILE, w), lambda b, j: (b, j, 0))
    pad_rows = (TILE // GRID_W) * (GRID_W + 32)
    return pl.pallas_call(
        _sgu_conv_kernel,
        grid=(bsz, nt),
        in_specs=[tok(2 * W_MIX), tok(2 * W_MIX),
                  _const_spec((SGU_GROUPS, SGU_CHUNK, SGU_CHUNK)),
                  _const_spec((SGU_CHUNK, W_MIX)),
                  _const_spec((1, W_MIX)), _const_spec((1, W_MIX)),
                  _const_spec((CONV_WIDTH, W_MIX)), _const_spec((1, W_MIX)),
                  _const_spec((1, W_MIX)), _const_spec((1, W_MIX))],
        out_specs=(tok(W_MIX), tok(W_MIX)),
        out_shape=(jax.ShapeDtypeStruct((bsz, s, W_MIX), BF16),
                   jax.ShapeDtypeStruct((bsz, s, W_MIX), BF16)),
        scratch_shapes=[pltpu.VMEM((pad_rows, W_MIX), F32)],
        compiler_params=_params("arbitrary", "arbitrary"),
        name="sgu_conv",
    )(zb, zc, sgu_w.astype(BF16), bias_full, row(sgu_ln_w), row(sgu_ln_b), conv_w,
      row(conv_b), row(conv_ln_w), row(conv_ln_b))


def _cpow(ar, ai, ld, k):
    dt = jnp.exp(ld)
    mag = jnp.exp(k * (ar * dt))
    ang = k * (ai * dt)
    return mag * jnp.cos(ang), mag * jnp.sin(ang)


def _zoh_coef(ar, ai, ld):
    er, ei = _cpow(ar, ai, ld, 1.0)
    den = ar * ar + ai * ai
    return ((er - 1.0) * ar + ei * ai) / den, (ei * ar - (er - 1.0) * ai) / den


def _dot_x3(a, b):
    a_hi, a_mid, _ = _split3(a)
    b_hi, b_mid, _ = _split3(b)
    return _dot(a_hi, b_hi) + _dot(a_hi, b_mid) + _dot(a_mid, b_hi)


def _s5_ops_kernel(cols_ref, rows_ref, btre_ref, btim_ref, bx_ref, bsw_ref, cre_ref, cim_ref,
                   ctre_ref, ctim_ref, kb_ref, wloc_ref, wout_ref, ar_ref,
                   mt_ref, pt_ref, pw_ref, v_ref):
    L = S5_CHUNK
    i = pl.program_id(1)
    lane2 = lax.broadcasted_iota(jnp.int32, (1, 2 * S5_N), 1)

    @pl.when(i == 0)
    def _():
        col = lambda k: cols_ref[0, :, k:k + 1]
        row = lambda k: rows_ref[0, k:k + 1, :]
        cf = [_zoh_coef(col(3 * d), col(3 * d + 1), col(3 * d + 2)) for d in range(2)]
        jj = lax.broadcasted_iota(jnp.int32, (1, 2 * L), 1)
        fwd = jj >= L
        k = jnp.abs(jj - L).astype(F32)
        sel = lambda a, b: jnp.where(fwd, a, b)
        pre, pim = _cpow(sel(col(0), col(3)), sel(col(1), col(4)), sel(col(2), col(5)), k)
        cre_, cim_ = sel(cf[0][0], cf[1][0]), sel(cf[0][1], cf[1][1])
        m_re = pre * cre_ - pim * cim_
        m_im = pre * cim_ + pim * cre_
        mt_ref[0] = m_re + jnp.where(jj == L, cf[1][0], 0.0)
        mt_ref[1] = m_im + jnp.where(jj == L, cf[1][1], 0.0)
        tt = lax.broadcasted_iota(jnp.int32, (1, L), 1)
        ss = lax.broadcasted_iota(jnp.int32, (L, 1), 0)
        for d in range(2):
            kk = (tt + 1 if d == 0 else L - tt).astype(F32)
            pt_ref[d, 0], pt_ref[d, 1] = _cpow(col(3 * d), col(3 * d + 1), col(3 * d + 2), kk)
            ks = (L - 1 - ss if d == 0 else ss).astype(F32)
            qre, qim = _cpow(row(3 * d), row(3 * d + 1), row(3 * d + 2), ks)
            fre, fim = _zoh_coef(row(3 * d), row(3 * d + 1), row(3 * d + 2))
            pw_ref[d, 0] = qre * fre - qim * fim
            pw_ref[d, 1] = qre * fim + qim * fre
            lre, lim = _cpow(row(3 * d), row(3 * d + 1), row(3 * d + 2), float(L))
            ar_ref[0, 2 * d:2 * d + 1, :] = lre
            ar_ref[0, 2 * d + 1:2 * d + 2, :] = jnp.where(lane2 < S5_N, -lim, lim)
        ar_ref[0, 4:8, :] = jnp.zeros((4, 2 * S5_N), F32)

    cre = cre_ref[0, pl.ds(i, 1), :]
    cim = cim_ref[0, pl.ds(i, 1), :]
    g_re = cre * btre_ref[0] - cim * btim_ref[0]
    g_im = cre * btim_ref[0] + cim * btre_ref[0]
    v_ref[...] = _dot_x3(g_re, mt_ref[0]) - _dot_x3(g_im, mt_ref[1])
    for q in range(S5_P):
        vq = jnp.broadcast_to(v_ref[q:q + 1, :], (L, 2 * L))
        toe = pltpu.roll(vq, 0, 1, stride=1, stride_axis=0)
        kb_ref[0, q * L:(q + 1) * L, :] = toe[:, L:2 * L].astype(BF16)

    lane_p = lax.broadcasted_iota(jnp.int32, (1, S5_P), 1)
    pick = lambda ref: jnp.sum(jnp.where(lane_p == i, ref[0], 0.0), axis=1, keepdims=True)
    ccol_re, ccol_im = pick(ctre_ref), pick(ctim_ref)
    bx = bx_ref[0, pl.ds(i, 1), :]
    bsw = bsw_ref[0, pl.ds(i, 1), :]
    by = jnp.where(lane2 < S5_N, -bsw, bsw)
    for d in range(2):
        pre, pim = pt_ref[d, 0], pt_ref[d, 1]
        base = d * 2 * S5_N
        wout_ref[0, base:base + S5_N, :] = (ccol_re * pre - ccol_im * pim).astype(BF16)
        wout_ref[0, base + S5_N:base + 2 * S5_N, :] = (-(ccol_re * pim + ccol_im * pre)).astype(BF16)
        wloc_ref[0, :, base:base + 2 * S5_N] = (pw_ref[d, 0] * bx + pw_ref[d, 1] * by).astype(BF16)


def _s5_operators(a_re, a_im, log_dt, b_re, b_im, c_re, c_im):
    L = S5_CHUNK
    g, n2 = S5_GROUPS, 2 * S5_N
    ld = jnp.broadcast_to(log_dt[:, :, None], a_re.shape)
    per_dir = [a_re[0], a_im[0], ld[0], a_re[1], a_im[1], ld[1]]
    zero = jnp.zeros_like(a_re[0])
    cols = jnp.stack(per_dir + [zero, zero], axis=-1)
    dup = lambda v: jnp.concatenate([v, v], axis=-1)
    rows = jnp.stack([dup(v) for v in per_dir + [zero, zero]], axis=1)
    bt_re, bt_im = b_re.transpose(0, 2, 1), b_im.transpose(0, 2, 1)
    bx = jnp.concatenate([bt_re, bt_im], axis=-1)
    bsw = jnp.concatenate([bt_im, bt_re], axis=-1)
    ct_re, ct_im = c_re.transpose(0, 2, 1), c_im.transpose(0, 2, 1)
    per_g = lambda a: pl.BlockSpec((1,) + a.shape[1:], lambda gi, i: (gi, 0, 0))
    ins = (cols, rows, bt_re, bt_im, bx, bsw, c_re, c_im, ct_re, ct_im)
    return pl.pallas_call(
        _s5_ops_kernel,
        grid=(g, S5_P),
        in_specs=[per_g(a) for a in ins],
        out_specs=(pl.BlockSpec((1, S5_FLAT, L), lambda gi, i: (gi, 0, i)),
                   pl.BlockSpec((1, L, 2 * n2), lambda gi, i: (gi, i, 0)),
                   pl.BlockSpec((1, 2 * n2, L), lambda gi, i: (gi, 0, i)),
                   pl.BlockSpec((1, 8, n2), lambda gi, i: (gi, 0, 0))),
        out_shape=(jax.ShapeDtypeStruct((g, S5_FLAT, S5_FLAT), BF16),
                   jax.ShapeDtypeStruct((g, S5_FLAT, 2 * n2), BF16),
                   jax.ShapeDtypeStruct((g, 2 * n2, S5_FLAT), BF16),
                   jax.ShapeDtypeStruct((g, 8, n2), F32)),
        scratch_shapes=[pltpu.VMEM((2, S5_N, 2 * L), F32),
                        pltpu.VMEM((2, 2, S5_N, L), F32),
                        pltpu.VMEM((2, 2, L, n2), F32),
                        pltpu.VMEM((S5_P, 2 * L), F32)],
        compiler_params=_params("arbitrary", "arbitrary"),
        name="s5_operators",
    )(*ins)


def _s5_kernel(u_ref, kb_ref, wloc_ref, wout_ref, ar_ref, y_ref, sin_ref, *, bsz, n_ctx_chunks):
    n = pl.program_id(1)
    n_chunks = u_ref.shape[1] // bsz
    two_n = 2 * S5_N

    @pl.when(n == 0)
    def _():
        sloc = _dot(u_ref[0], wloc_ref[0])
        ar = ar_ref[0]
        n_lat = n_chunks - n_ctx_chunks
        fwd_order = list(range(n_lat, n_chunks)) + list(range(n_lat))
        bwd_order = list(range(n_chunks - 1, n_lat - 1, -1)) + list(range(n_lat - 1, -1, -1))
        for d, order in ((0, fwd_order), (1, bwd_order)):
            a1 = ar[2 * d:2 * d + 1, :]
            a2 = ar[2 * d + 1:2 * d + 2, :]
            st = jnp.zeros((bsz, two_n), F32)
            for c in order:
                sin_ref[c * bsz:(c + 1) * bsz, d * two_n:(d + 1) * two_n] = st.astype(BF16)
                loc = sloc[c * bsz:(c + 1) * bsz, d * two_n:(d + 1) * two_n]
                st = a1 * st + a2 * pltpu.roll(st, S5_N, 1) + loc

    y_ref[0] = _dot(u_ref[0], kb_ref[0]) + _dot(sin_ref[...], wout_ref[0])


def _s5_mix(ut, kbig, w_loc, w_out, a_rows, bsz):
    g, rows, flat = ut.shape
    tn = 512
    kern = functools.partial(_s5_kernel, bsz=bsz, n_ctx_chunks=CTX_LEN // S5_CHUNK)
    return pl.pallas_call(
        kern,
        grid=(g, flat // tn),
        in_specs=[pl.BlockSpec((1, rows, flat), lambda gi, n: (gi, 0, 0)),
                  pl.BlockSpec((1, flat, tn), lambda gi, n: (gi, 0, n)),
                  pl.BlockSpec((1, flat, 4 * S5_N), lambda gi, n: (gi, 0, 0)),
                  pl.BlockSpec((1, 4 * S5_N, tn), lambda gi, n: (gi, 0, n)),
                  pl.BlockSpec((1, 8, 2 * S5_N), lambda gi, n: (gi, 0, 0))],
        out_specs=pl.BlockSpec((1, rows, tn), lambda gi, n: (gi, 0, n)),
        out_shape=jax.ShapeDtypeStruct((g, rows, flat), F32),
        scratch_shapes=[pltpu.VMEM((rows, 4 * S5_N), BF16)],
        compiler_params=_params("arbitrary", "arbitrary"),
        name="s5_mix",
    )(ut, kbig, w_loc, w_out, a_rows)


def _s5_readout_kernel(y_ref, zd_ref, d_ref, gw_ref, gb_ref, o_ref):
    for cl in range(TILE // S5_CHUNK):
        rows = slice(cl * S5_CHUNK, (cl + 1) * S5_CHUNK)
        yt = y_ref[:, cl, 0, :, :].reshape(W_MIX, S5_CHUNK)
        y = yt.T + d_ref[...] * zd_ref[0, rows, :]
        y = _gelu(y)
        gate = _sigmoid(_dot(y.astype(BF16), gw_ref[...]) + gb_ref[...])
        o_ref[0, rows, :] = (y * gate).astype(o_ref.dtype)


def _s5_readout(y5, zd, d_skip, glu_w, glu_b):
    bsz, s, _ = zd.shape
    nt = s // TILE
    cpt = TILE // S5_CHUNK
    return pl.pallas_call(
        _s5_readout_kernel,
        grid=(bsz, nt),
        in_specs=[pl.BlockSpec((S5_GROUPS, cpt, 1, S5_P, S5_CHUNK), lambda b, j: (0, j, b, 0, 0)),
                  pl.BlockSpec((1, TILE, W_MIX), lambda b, j: (b, j, 0)),
                  _const_spec((1, W_MIX)), _const_spec((W_MIX, W_MIX)), _const_spec((1, W_MIX))],
        out_specs=pl.BlockSpec((1, TILE, W_MIX), lambda b, j: (b, j, 0)),
        out_shape=jax.ShapeDtypeStruct((bsz, s, W_MIX), BF16),
        compiler_params=_params("arbitrary", "arbitrary"),
        name="s5_readout",
    )(y5, zd, d_skip.reshape(1, W_MIX), glu_w.astype(BF16), glu_b.reshape(1, W_MIX))


def _out_ffn_kernel(x_ref, ya_ref, yb_ref, yc_ref, yd_ref, modb_ref, modc_ref, wo_ref, l1w_ref,
                    l1b_ref, wi_ref, wd_ref, l2w_ref, l2b_ref, o_ref, x1_ref, h2_ref, a_ref, *, ctx_rows):
    tm = x_ref.shape[1]
    n_lat = tm - ctx_rows
    segs = [(0, n_lat, 0)] + ([(n_lat, tm, 1)] if ctx_rows else [])
    y = (_dot(ya_ref[0], wo_ref[0:W_MIX, :]) + _dot(yb_ref[0], wo_ref[W_MIX:2 * W_MIX, :]) +
         _dot(yc_ref[0], wo_ref[2 * W_MIX:3 * W_MIX, :]) + _dot(yd_ref[0], wo_ref[3 * W_MIX:4 * W_MIX, :]))
    for r0, r1, which in segs:
        g1 = _tile_mod(modb_ref, modc_ref, 2)[which]
        sh2 = _tile_mod(modb_ref, modc_ref, 3)[which]
        sc2 = _tile_mod(modb_ref, modc_ref, 4)[which]
        x1 = _layer_norm(ALPHA * x_ref[0, r0:r1, :] + g1 * y[r0:r1, :], l1w_ref[...], l1b_ref[...])
        x1_ref[r0:r1, :] = x1
        h2_ref[r0:r1, :] = (x1 * (1.0 + sc2) + sh2).astype(BF16)
    h2 = h2_ref[...]
    for kk in range(D_FF // FF_CHUNK):
        cols = slice(kk * FF_CHUNK, (kk + 1) * FF_CHUNK)
        gk = _dot(h2, wi_ref[:, cols])
        uk = _dot(h2, wi_ref[:, D_FF + kk * FF_CHUNK:D_FF + (kk + 1) * FF_CHUNK])
        a_ref[:, cols] = (_silu(gk) * uk).astype(BF16)
    f = _dot(a_ref[...], wd_ref[...])
    for r0, r1, which in segs:
        g2 = _tile_mod(modb_ref, modc_ref, 5)[which]
        o_ref[0, r0:r1, :] = _layer_norm(ALPHA * x1_ref[r0:r1, :] + g2 * f[r0:r1, :],
                                         l2w_ref[...], l2b_ref[...])


def _out_ffn(xcat, ya, yb, yc, yd, mod, w_out, ln1_w, ln1_b, w_ffn_in, w_ffn_out, ln2_w, ln2_b,
             tm, n_out):
    bsz, s, d = xcat.shape
    ctx_rows = CTX_LEN if n_out == s else 0
    tok = lambda w: pl.BlockSpec((1, tm, w), lambda b, j: (b, j, 0))
    row = lambda a: a.reshape(1, d)
    return pl.pallas_call(
        functools.partial(_out_ffn_kernel, ctx_rows=ctx_rows),
        grid=(bsz, n_out // tm),
        in_specs=[tok(d), tok(W_MIX), tok(W_MIX), tok(W_MIX), tok(W_MIX),
                  pl.BlockSpec((1, 6, d), lambda b, j: (b, 0, 0)),
                  pl.BlockSpec((1, 6, d), lambda b, j: (bsz, 0, 0)),
                  _const_spec(w_out.shape), _const_spec((1, d)), _const_spec((1, d)),
                  _const_spec(w_ffn_in.shape), _const_spec(w_ffn_out.shape),
                  _const_spec((1, d)), _const_spec((1, d))],
        out_specs=tok(d),
        out_shape=jax.ShapeDtypeStruct((bsz, n_out, d), F32),
        scratch_shapes=[pltpu.VMEM((tm, d), F32), pltpu.VMEM((tm, d), BF16), pltpu.VMEM((tm, D_FF), BF16)],
        compiler_params=_params("arbitrary", "arbitrary"),
        name="out_ffn",
    )(xcat, ya, yb, yc, yd, mod, mod, w_out, row(ln1_w), row(ln1_b), w_ffn_in, w_ffn_out,
      row(ln2_w), row(ln2_b))


def _reorder_w_in(w):
    off_b = 4 * W_MIX + N_GATES
    gates = w[:, 4 * W_MIX:off_b]
    w_r = jnp.concatenate([w[:, :4 * W_MIX], w[:, off_b:],
                           gates, jnp.zeros((w.shape[0], GATE_PAD - N_GATES), w.dtype)], axis=1)
    return w_r.astype(BF16), gates.T.astype(BF16)


def kernel(x, c, ctx, c_ctx, w_mod, b_mod, w_in, mlstm_gate_bias, mlstm_norm_w, sgu_ln_w, sgu_ln_b, sgu_w, sgu_b, conv_w, conv_b, conv_ln_w, conv_ln_b, s5_a_re, s5_a_im, s5_log_dt, s5_b_re, s5_b_im, s5_c_re, s5_c_im, s5_d, s5_glu_w, s5_glu_b, w_out, ln1_w, ln1_b, w_ffn_in, w_ffn_out, ln2_w, ln2_b):
    bsz, seq, d = x.shape
    s = seq + CTX_LEN
    xcat = jnp.concatenate([x, ctx], axis=1)
    mod_rows = -(-(bsz + 1) // 8) * 8
    cvec = jnp.zeros((mod_rows, d), F32).at[:bsz].set(c).at[bsz].set(c_ctx)
    mod_all = _modulation(cvec, w_mod, b_mod).reshape(DEPTH, mod_rows, 6, d)
    nc5 = s // S5_CHUNK
    for l in range(DEPTH):
        mod = mod_all[l]
        w_r, wg_t = _reorder_w_in(w_in[l])
        k, qt, vt, ot, zb, zc, zd, ut, g, gt = _in_projection(xcat, mod, w_r, wg_t)
        ya = _mlstm(k, qt, vt, ot, g, gt, mlstm_gate_bias[l], mlstm_norm_w[l])
        yb, yc = _sgu_conv(zb, zc, sgu_w[l], sgu_b[l], sgu_ln_w[l], sgu_ln_b[l],
                           conv_w[l], conv_b[l], conv_ln_w[l], conv_ln_b[l])
        kbig, w_loc, w_so, a_rows = _s5_operators(s5_a_re[l], s5_a_im[l], s5_log_dt[l], s5_b_re[l],
                                                  s5_b_im[l], s5_c_re[l], s5_c_im[l])
        y5 = _s5_mix(ut.reshape(S5_GROUPS, nc5 * bsz, S5_FLAT), kbig, w_loc, w_so, a_rows, bsz)
        yd = _s5_readout(y5.reshape(S5_GROUPS, nc5, bsz, S5_P, S5_CHUNK), zd, s5_d[l],
                         s5_glu_w[l], s5_glu_b[l])
        last = l == DEPTH - 1
        xcat = _out_ffn(xcat, ya, yb, yc, yd, mod, w_out[l].astype(BF16), ln1_w[l], ln1_b[l],
                        w_ffn_in[l].astype(BF16), w_ffn_out[l].astype(BF16), ln2_w[l], ln2_b[l],
                        tm=TM_LATENT if last else TM_MAIN, n_out=seq if last else s)
    return xcat
```

```python
import functools

import jax
import jax.numpy as jnp
from jax import lax
from jax.experimental import pallas as pl
from jax.experimental.pallas import tpu as pltpu

F32 = jnp.float32
BF16 = jnp.bfloat16

D_MODEL = 1024
DEPTH = 2
CTX_LEN = 256
GRID_W = 64
W_MIX = 256
HEADS = 4
HEAD_DIM = W_MIX // HEADS
SGU_GROUPS = 4
SGU_GD = W_MIX // SGU_GROUPS
SGU_CHUNK = 128
CONV_WIDTH = 31
CONV_PAD = CONV_WIDTH // 2
CONV_HALO = 32
S5_P = 16
S5_GROUPS = W_MIX // S5_P
S5_N = 64
S5_CHUNK = 128
S5_FLAT = S5_P * S5_CHUNK
D_FF = 2816
FF_CHUNK = 256
ALPHA = (2 * DEPTH) ** 0.25
LN_EPS = 1e-5
TILE = 256
MLSTM_CHUNK = 256
TM_MAIN = 3 * TILE
TM_LATENT = 2 * TILE
CONV_ROWS = max((TILE // GRID_W) * (GRID_W + CONV_HALO), CTX_LEN + CONV_HALO)
N_GATES = 4 * HEADS
GATE_PAD = 128
C_O = 3 * W_MIX
C_B = 4 * W_MIX
C_C = C_B + 2 * W_MIX
C_D = C_C + 2 * W_MIX
C_G = C_D + W_MIX
P_COLS = C_G + GATE_PAD
VMEM_LIMIT = 56 * 1024 * 1024


def _dot(a, b):
    return jnp.dot(a, b, preferred_element_type=F32)


def _dot_nt(a, b):
    return lax.dot_general(a, b, (((1,), (1,)), ((), ())), preferred_element_type=F32)


def _split3(x):
    hi = x.astype(BF16)
    r = x - hi.astype(F32)
    mid = r.astype(BF16)
    lo = (r - mid.astype(F32)).astype(BF16)
    return hi, mid, lo


def _dot3_r(w_bf16, x):
    hi, mid, lo = _split3(x)
    return _dot(w_bf16, hi) + _dot(w_bf16, mid) + _dot(w_bf16, lo)


def _layer_norm(x, w, b):
    mu = jnp.mean(x, axis=-1, keepdims=True)
    d = x - mu
    var = jnp.mean(d * d, axis=-1, keepdims=True)
    return d * lax.rsqrt(var + LN_EPS) * w + b


def _sigmoid(x):
    return jax.nn.sigmoid(x)


def _silu(x):
    return x * jax.nn.sigmoid(x)


def _gelu(x):
    return jax.nn.gelu(x, approximate=True)


def _log_sigmoid(x):
    return jnp.minimum(x, 0.0) - jnp.log1p(jnp.exp(-jnp.abs(x)))


def _params(*sem):
    return pltpu.CompilerParams(dimension_semantics=sem, vmem_limit_bytes=VMEM_LIMIT)


def _const_spec(shape):
    nd = len(shape)
    return pl.BlockSpec(shape, lambda *_: (0,) * nd, pipeline_mode=pl.Buffered(1))


def _mod_kernel(c_ref, w_ref, b_ref, o_ref):
    s = _silu(c_ref[...])
    s_hi = s.astype(BF16)
    s_lo = (s - s_hi.astype(F32)).astype(BF16)
    w = w_ref[0]
    w_hi = w.astype(BF16)
    w_lo = (w - w_hi.astype(F32)).astype(BF16)
    o_ref[0] = _dot(s_hi, w_hi) + _dot(s_hi, w_lo) + _dot(s_lo, w_hi) + b_ref[0]


def _modulation(cvec, w_mod, b_mod):
    depth, d, n = w_mod.shape
    rows = cvec.shape[0]
    tn = 1024
    return pl.pallas_call(
        _mod_kernel,
        grid=(depth, n // tn),
        in_specs=[pl.BlockSpec((rows, d), lambda l, j: (0, 0)),
                  pl.BlockSpec((1, d, tn), lambda l, j: (l, 0, j)),
                  pl.BlockSpec((1, 1, tn), lambda l, j: (l, 0, j))],
        out_specs=pl.BlockSpec((1, rows, tn), lambda l, j: (l, 0, j)),
        out_shape=jax.ShapeDtypeStruct((depth, rows, n), F32),
        compiler_params=_params("arbitrary", "arbitrary"),
        name="modulation",
    )(cvec, w_mod, b_mod.reshape(depth, 1, n))


def _tile_mod(modb_ref, modc_ref, i):
    last = pl.program_id(1) == pl.num_programs(1) - 1
    row_b = modb_ref[0, i:i + 1, :]
    return row_b, jnp.where(last, modc_ref[0, i:i + 1, :], row_b)


def _sgu_slab(zb, sgw_ref, sgb_ref, slnw_ref, slnb_ref):
    u = _gelu(zb[:, 0:W_MIX])
    vn = _layer_norm(_gelu(zb[:, W_MIX:2 * W_MIX]), slnw_ref[...], slnb_ref[...]).astype(BF16)
    lane_grp = lax.broadcasted_iota(jnp.int32, (1, W_MIX), 1) // SGU_GD
    out = []
    for cl in range(TILE // SGU_CHUNK):
        vc = vn[cl * SGU_CHUNK:(cl + 1) * SGU_CHUNK, :]
        acc = sgb_ref[...]
        for gi in range(SGU_GROUPS):
            acc = acc + _dot(sgw_ref[gi], jnp.where(lane_grp == gi, vc, jnp.zeros_like(vc)))
        out.append(acc)
    return u * jnp.concatenate(out, axis=0)


def _conv_slab(a, seg_len, cw_ref, sh_ref):
    n_seg = TILE // seg_len
    stride = seg_len + CONV_HALO
    zeros16 = jnp.zeros((16, W_MIX), F32)
    for sg in range(n_seg):
        base = sg * stride
        seg = a[sg * seg_len:(sg + 1) * seg_len, :]
        for r in range(8):
            sh_ref[r, pl.ds(base, 16), :] = zeros16
            sh_ref[r, pl.ds(base + seg_len + 8, 16), :] = zeros16
            sh_ref[r, pl.ds(base + 16 - r, seg_len), :] = seg
    out = []
    for sg in range(n_seg):
        base = sg * stride
        acc = jnp.zeros((seg_len, W_MIX), F32)
        for t in range(CONV_WIDTH):
            off = t - CONV_PAD
            r = off % 8
            acc = acc + cw_ref[t:t + 1, :] * sh_ref[r, pl.ds(base + (off - r) + 16, seg_len), :]
        out.append(acc)
    return out[0] if n_seg == 1 else jnp.concatenate(out, axis=0)


def _inproj_kernel(x_ref, modb_ref, modc_ref, w_ref, wgt_ref, sgw_ref, sgb_ref, slnw_ref, slnb_ref,
                   cw_ref, cb_ref, clnw_ref, clnb_ref,
                   k_ref, qt_ref, vt_ref, ot_ref, yb_ref, yc_ref, zd_ref, ut_ref, g_ref, gt_ref,
                   h_ref, sh_ref):
    tm = x_ref.shape[1]
    n_slabs = tm // TILE
    last = pl.program_id(1) == pl.num_programs(1) - 1
    sh_b, sh_c = _tile_mod(modb_ref, modc_ref, 0)
    sc_b, sc_c = _tile_mod(modb_ref, modc_ref, 1)
    n_lat = tm - CTX_LEN
    h_ref[0:n_lat, :] = (x_ref[0, 0:n_lat, :] * (1.0 + sc_b) + sh_b).astype(BF16)
    h_ref[n_lat:tm, :] = (x_ref[0, n_lat:tm, :] * (1.0 + sc_c) + sh_c).astype(BF16)
    for r in range(n_slabs):
        rows = slice(r * TILE, (r + 1) * TILE)
        h = h_ref[rows, :]
        qt_ref[0, :, rows] = _dot(h, w_ref[:, 0:W_MIX]).T.astype(BF16)
        k_ref[0, rows, :] = (_dot(h, w_ref[:, W_MIX:2 * W_MIX]) * (HEAD_DIM ** -0.5)).astype(BF16)
        vt_ref[0, :, rows] = _dot(h, w_ref[:, 2 * W_MIX:3 * W_MIX]).T.astype(BF16)
        ot_ref[0, :, rows] = _dot(h, w_ref[:, C_O:C_B]).T
        zd = _dot(h, w_ref[:, C_D:C_G])
        zd_ref[0, rows, :] = zd
        zdt = zd.T
        for cl in range(TILE // S5_CHUNK):
            blk = zdt[:, cl * S5_CHUNK:(cl + 1) * S5_CHUNK]
            ut_ref[:, r * (TILE // S5_CHUNK) + cl, 0, :, :] = (
                blk.reshape(S5_GROUPS, S5_P, S5_CHUNK).astype(BF16))
        g_ref[0, rows, :] = _dot(h, w_ref[:, C_G:P_COLS])
        gt_ref[0, :, rows] = _dot_nt(wgt_ref[...], h)
        yb = _sgu_slab(_dot(h, w_ref[:, C_B:C_C]), sgw_ref, sgb_ref, slnw_ref, slnb_ref)
        yb_ref[0, rows, :] = yb.astype(yb_ref.dtype)
        zc = _dot(h, w_ref[:, C_C:C_D])
        a = zc[:, 0:W_MIX] * _sigmoid(zc[:, W_MIX:2 * W_MIX])

        def conv_out(seg_len, a=a, rows=rows, r=r):
            acc = _conv_slab(a, seg_len, cw_ref, sh_ref.at[r])
            yc = _silu(_layer_norm(acc + cb_ref[...], clnw_ref[...], clnb_ref[...]))
            yc_ref[0, rows, :] = yc.astype(yc_ref.dtype)

        if r < n_slabs - 1:
            conv_out(GRID_W)
        else:
            pl.when(jnp.logical_not(last))(functools.partial(conv_out, GRID_W))
            pl.when(last)(functools.partial(conv_out, CTX_LEN))


def _in_projection(xcat, mod, w_r, wg_t, sgu_w, sgu_b, sgu_ln_w, sgu_ln_b, conv_w, conv_b,
                   conv_ln_w, conv_ln_b):
    bsz, s, d = xcat.shape
    tm = TM_MAIN
    nc5 = s // S5_CHUNK
    cpt = tm // S5_CHUNK
    bias_full = jnp.repeat(sgu_b.T, SGU_GD, axis=1)
    row = lambda a: a.reshape(1, W_MIX)
    tok = lambda w: pl.BlockSpec((1, tm, w), lambda b, j: (b, j, 0))
    chan = lambda c: pl.BlockSpec((1, c, tm), lambda b, j: (b, 0, j))
    out_shape = (
        jax.ShapeDtypeStruct((bsz, s, W_MIX), BF16),
        jax.ShapeDtypeStruct((bsz, W_MIX, s), BF16),
        jax.ShapeDtypeStruct((bsz, W_MIX, s), BF16),
        jax.ShapeDtypeStruct((bsz, W_MIX, s), F32),
        jax.ShapeDtypeStruct((bsz, s, W_MIX), BF16),
        jax.ShapeDtypeStruct((bsz, s, W_MIX), BF16),
        jax.ShapeDtypeStruct((bsz, s, W_MIX), F32),
        jax.ShapeDtypeStruct((S5_GROUPS, nc5, bsz, S5_P, S5_CHUNK), BF16),
        jax.ShapeDtypeStruct((bsz, s, GATE_PAD), F32),
        jax.ShapeDtypeStruct((bsz, N_GATES, s), F32),
    )
    out_specs = (
        tok(W_MIX), chan(W_MIX), chan(W_MIX), chan(W_MIX), tok(W_MIX), tok(W_MIX), tok(W_MIX),
        pl.BlockSpec((S5_GROUPS, cpt, 1, S5_P, S5_CHUNK), lambda b, j: (0, j, b, 0, 0)),
        tok(GATE_PAD), chan(N_GATES),
    )
    return pl.pallas_call(
        _inproj_kernel,
        grid=(bsz, s // tm),
        in_specs=[tok(d),
                  pl.BlockSpec((1, 6, d), lambda b, j: (b, 0, 0)),
                  pl.BlockSpec((1, 6, d), lambda b, j: (bsz, 0, 0)),
                  _const_spec(w_r.shape),
                  _const_spec(wg_t.shape),
                  _const_spec((SGU_GROUPS, SGU_CHUNK, SGU_CHUNK)),
                  _const_spec((SGU_CHUNK, W_MIX)),
                  _const_spec((1, W_MIX)), _const_spec((1, W_MIX)),
                  _const_spec((CONV_WIDTH, W_MIX)), _const_spec((1, W_MIX)),
                  _const_spec((1, W_MIX)), _const_spec((1, W_MIX))],
        out_specs=out_specs,
        out_shape=out_shape,
        scratch_shapes=[pltpu.VMEM((tm, d), BF16),
                        pltpu.VMEM((tm // TILE, 8, CONV_ROWS, W_MIX), F32)],
        compiler_params=_params("arbitrary", "arbitrary"),
        name="in_projection",
    )(xcat, mod, mod, w_r, wg_t, sgu_w.astype(BF16), bias_full, row(sgu_ln_w), row(sgu_ln_b),
      conv_w, row(conv_b), row(conv_ln_w), row(conv_ln_b))


def _mlstm_kernel(k_ref, qt_ref, vt_ref, ot_ref, g_ref, gt_ref, gb_ref, gbt_ref, nw_ref, y_ref,
                  ht_ref, lg_ref, cum_ref, lgt_ref, cumt_ref, c_ref, n_ref, m_ref):
    L = MLSTM_CHUNK
    n_chunks = k_ref.shape[1] // L
    row = lax.broadcasted_iota(jnp.int32, (L, L), 0)
    col = lax.broadcasted_iota(jnp.int32, (L, L), 1)
    tri = (col <= row).astype(BF16)
    lane_g = lax.broadcasted_iota(jnp.int32, (1, GATE_PAD), 1)
    is_f_lane = (lane_g % 8) >= HEADS
    sub_g = lax.broadcasted_iota(jnp.int32, (N_GATES, 1), 0)
    is_f_sub = (sub_g % 8) >= HEADS
    lane_w = lax.broadcasted_iota(jnp.int32, (1, W_MIX), 1) // HEAD_DIM
    sub_w = lax.broadcasted_iota(jnp.int32, (W_MIX, 1), 0) // HEAD_DIM
    bd_mask = sub_w == lane_w
    n_mask = sub_g == lane_w

    def gate_prologue(c, carry):
        r0 = pl.multiple_of(c * L, L)
        graw = g_ref[0, pl.ds(r0, L), :] + gb_ref[...]
        lg = jnp.where(is_f_lane, _log_sigmoid(graw), graw)
        lg_ref[pl.ds(r0, L), :] = lg
        cum_ref[pl.ds(r0, L), :] = _dot3_r(tri, lg)
        grawt = gt_ref[0, :, pl.ds(r0, L)] + gbt_ref[...]
        lgt = jnp.where(is_f_sub, _log_sigmoid(grawt), grawt)
        lgt_ref[:, pl.ds(r0, L)] = lgt
        hi, mid, lo = _split3(lgt)
        cumt_ref[:, pl.ds(r0, L)] = _dot_nt(hi, tri) + _dot_nt(mid, tri) + _dot_nt(lo, tri)
        return carry

    lax.fori_loop(0, n_chunks, gate_prologue, 0)

    def chunk(c, d):
        r0 = pl.multiple_of(c * L, L)
        k = k_ref[0, pl.ds(r0, L), :]
        qt = qt_ref[0, :, pl.ds(r0, L)]
        vt = vt_ref[0, :, pl.ds(r0, L)]
        lg = lg_ref[pl.ds(r0, L), :]
        cum = cum_ref[pl.ds(r0, L), :]
        lgt = lgt_ref[:, pl.ds(r0, L)]
        cumt = cumt_ref[:, pl.ds(r0, L)]
        tot = cumt[:, L - 1:L]
        if d == 1:
            cum = cum[L - 1:L, :] - cum + lg
            cumt = tot - cumt + lgt
        mask = (row <= col) if d == 0 else (row >= col)
        m_vec = m_ref[d]
        c_old = c_ref[d]
        n_old = n_ref[d]
        qc = _dot(c_old.astype(BF16), qt)
        qn = _dot(n_old.astype(BF16), qt)
        e_slabs = []
        e_small = jnp.zeros((N_GATES, L), F32)
        decay_row = jnp.zeros((1, W_MIX), F32)
        m_new_vec = m_vec
        for h in range(HEADS):
            ic = d * 8 + h
            fc = d * 8 + HEADS + h
            head_lanes = lane_w == h
            sl = slice(h * HEAD_DIM, (h + 1) * HEAD_DIM)
            m_h = m_vec[:, fc:fc + 1]
            src = lg[:, ic:ic + 1] - cum[:, fc:fc + 1]
            cct = cumt[fc:fc + 1, :]
            logw = jnp.where(mask, src + cct, -jnp.inf)
            inter = cct + m_h
            m_t = jnp.maximum(inter, jnp.max(logw, axis=0, keepdims=True))
            w_inter = jnp.exp(inter - m_t)
            kh = jnp.where(head_lanes, k, jnp.zeros_like(k))
            sc = _dot(kh, qt) * jnp.exp(logw - m_t)
            den = jnp.sum(sc, axis=0, keepdims=True) + w_inter * qn[h:h + 1, :]
            num = _dot(vt[sl, :], sc.astype(BF16)) + w_inter * qc[sl, :]
            ht_ref[d, sl, pl.ds(r0, L)] = num / jnp.maximum(jnp.abs(den), jnp.exp(-m_t))
            tot_h = tot[fc:fc + 1, :]
            logu = tot_h - cct + lgt[ic:ic + 1, :]
            m_new = jnp.maximum(tot_h + m_h, jnp.max(logu, axis=1, keepdims=True))
            e_h = jnp.exp(logu - m_new)
            e_slabs.append(jnp.broadcast_to(e_h, (HEAD_DIM, L)))
            e_small = jnp.where(sub_g == h, e_h, e_small)
            decay_row = jnp.where(head_lanes, jnp.exp(tot_h + m_h - m_new), decay_row)
            m_new_vec = jnp.where(lane_g == fc, m_new, m_new_vec)
        vet = (vt.astype(F32) * jnp.concatenate(e_slabs, axis=0)).astype(BF16)
        c_ref[d] = decay_row * c_old + jnp.where(bd_mask, _dot(vet, k), 0.0)
        n_ref[d] = decay_row * n_old + jnp.where(n_mask, _dot(e_small.astype(BF16), k), 0.0)
        m_ref[d] = m_new_vec

    c_ref[...] = jnp.zeros_like(c_ref)
    n_ref[...] = jnp.zeros_like(n_ref)
    m_ref[...] = jnp.zeros_like(m_ref)
    n_lat = n_chunks - CTX_LEN // L
    for j in range(n_chunks - n_lat):
        chunk(n_lat + j, 0)
        chunk(n_chunks - 1 - j, 1)

    def scan_body(i, carry):
        chunk(i, 0)
        chunk(n_lat - 1 - i, 1)
        return carry

    lax.fori_loop(0, n_lat, scan_body, 0)

    def readout(c, carry):
        r0 = pl.multiple_of(c * L, L)
        slabs = []
        for h in range(HEADS):
            sl = slice(h * HEAD_DIM, (h + 1) * HEAD_DIM)
            hs = ht_ref[0, sl, pl.ds(r0, L)] + ht_ref[1, sl, pl.ds(r0, L)]
            mu = jnp.mean(hs, axis=0, keepdims=True)
            dl = hs - mu
            var = jnp.mean(dl * dl, axis=0, keepdims=True)
            hn = dl * lax.rsqrt(var + LN_EPS) * nw_ref[sl, :]
            slabs.append(hn * _sigmoid(ot_ref[0, sl, pl.ds(r0, L)]))
        y_ref[0, pl.ds(r0, L), :] = jnp.concatenate(slabs, axis=0).T.astype(y_ref.dtype)
        return carry

    lax.fori_loop(0, n_chunks, readout, 0)


def _mlstm(k, qt, vt, ot, g, gt, gate_bias, norm_w):
    bsz, s, _ = k.shape
    gb = jnp.zeros((1, GATE_PAD), F32).at[0, :N_GATES].set(gate_bias)
    gbt = gate_bias.reshape(N_GATES, 1)
    nw_full = jnp.broadcast_to(norm_w.reshape(W_MIX, 1), (W_MIX, MLSTM_CHUNK))
    tok = lambda w: pl.BlockSpec((1, s, w), lambda b: (b, 0, 0))
    chan = lambda c: pl.BlockSpec((1, c, s), lambda b: (b, 0, 0))
    return pl.pallas_call(
        _mlstm_kernel,
        grid=(bsz,),
        in_specs=[tok(W_MIX), chan(W_MIX), chan(W_MIX), chan(W_MIX), tok(GATE_PAD), chan(N_GATES),
                  pl.BlockSpec((1, GATE_PAD), lambda b: (0, 0)),
                  pl.BlockSpec((N_GATES, 1), lambda b: (0, 0)),
                  pl.BlockSpec((W_MIX, MLSTM_CHUNK), lambda b: (0, 0))],
        out_specs=tok(W_MIX),
        out_shape=jax.ShapeDtypeStruct((bsz, s, W_MIX), BF16),
        scratch_shapes=[pltpu.VMEM((2, W_MIX, s), F32),
                        pltpu.VMEM((s, GATE_PAD), F32),
                        pltpu.VMEM((s, GATE_PAD), F32),
                        pltpu.VMEM((N_GATES, s), F32),
                        pltpu.VMEM((N_GATES, s), F32),
                        pltpu.VMEM((2, W_MIX, W_MIX), F32),
                        pltpu.VMEM((2, N_GATES, W_MIX), F32),
                        pltpu.VMEM((2, 1, GATE_PAD), F32)],
        compiler_params=_params("arbitrary"),
        name="mlstm",
    )(k, qt, vt, ot, g, gt, gb, gbt, nw_full)


def _cpow(ar, ai, ld, k):
    dt = jnp.exp(ld)
    mag = jnp.exp(k * (ar * dt))
    ang = k * (ai * dt)
    return mag * jnp.cos(ang), mag * jnp.sin(ang)


def _zoh_coef(ar, ai, ld):
    er, ei = _cpow(ar, ai, ld, 1.0)
    den = ar * ar + ai * ai
    return ((er - 1.0) * ar + ei * ai) / den, (ei * ar - (er - 1.0) * ai) / den


def _dot_x3(a, b):
    a_hi, a_mid, _ = _split3(a)
    b_hi, b_mid, _ = _split3(b)
    return _dot(a_hi, b_hi) + _dot(a_hi, b_mid) + _dot(a_mid, b_hi)


def _s5_ops_kernel(cols_ref, rows_ref, btre_ref, btim_ref, bx_ref, bsw_ref, cre_ref, cim_ref,
                   ctre_ref, ctim_ref, kb_ref, wloc_ref, wout_ref, ar_ref,
                   mt_ref, pt_ref, pw_ref, v_ref):
    L = S5_CHUNK
    i = pl.program_id(1)
    lane2 = lax.broadcasted_iota(jnp.int32, (1, 2 * S5_N), 1)

    @pl.when(i == 0)
    def _():
        col = lambda k: cols_ref[0, :, k:k + 1]
        row = lambda k: rows_ref[0, k:k + 1, :]
        cf = [_zoh_coef(col(3 * d), col(3 * d + 1), col(3 * d + 2)) for d in range(2)]
        jj = lax.broadcasted_iota(jnp.int32, (1, 2 * L), 1)
        fwd = jj >= L
        k = jnp.abs(jj - L).astype(F32)
        sel = lambda a, b: jnp.where(fwd, a, b)
        pre, pim = _cpow(sel(col(0), col(3)), sel(col(1), col(4)), sel(col(2), col(5)), k)
        cre_, cim_ = sel(cf[0][0], cf[1][0]), sel(cf[0][1], cf[1][1])
        m_re = pre * cre_ - pim * cim_
        m_im = pre * cim_ + pim * cre_
        mt_ref[0] = m_re + jnp.where(jj == L, cf[1][0], 0.0)
        mt_ref[1] = m_im + jnp.where(jj == L, cf[1][1], 0.0)
        tt = lax.broadcasted_iota(jnp.int32, (1, L), 1)
        ss = lax.broadcasted_iota(jnp.int32, (L, 1), 0)
        for d in range(2):
            kk = (tt + 1 if d == 0 else L - tt).astype(F32)
            pt_ref[d, 0], pt_ref[d, 1] = _cpow(col(3 * d), col(3 * d + 1), col(3 * d + 2), kk)
            ks = (L - 1 - ss if d == 0 else ss).astype(F32)
            qre, qim = _cpow(row(3 * d), row(3 * d + 1), row(3 * d + 2), ks)
            fre, fim = _zoh_coef(row(3 * d), row(3 * d + 1), row(3 * d + 2))
            pw_ref[d, 0] = qre * fre - qim * fim
            pw_ref[d, 1] = qre * fim + qim * fre
            lre, lim = _cpow(row(3 * d), row(3 * d + 1), row(3 * d + 2), float(L))
            ar_ref[0, 2 * d:2 * d + 1, :] = lre
            ar_ref[0, 2 * d + 1:2 * d + 2, :] = jnp.where(lane2 < S5_N, -lim, lim)
        ar_ref[0, 4:8, :] = jnp.zeros((4, 2 * S5_N), F32)

    cre = cre_ref[0, pl.ds(i, 1), :]
    cim = cim_ref[0, pl.ds(i, 1), :]
    g_re = cre * btre_ref[0] - cim * btim_ref[0]
    g_im = cre * btim_ref[0] + cim * btre_ref[0]
    v_ref[...] = _dot_x3(g_re, mt_ref[0]) - _dot_x3(g_im, mt_ref[1])
    for q in range(S5_P):
        vq = jnp.broadcast_to(v_ref[q:q + 1, :], (L, 2 * L))
        toe = pltpu.roll(vq, 0, 1, stride=1, stride_axis=0)
        kb_ref[0, q * L:(q + 1) * L, :] = toe[:, L:2 * L].astype(BF16)

    lane_p = lax.broadcasted_iota(jnp.int32, (1, S5_P), 1)
    pick = lambda ref: jnp.sum(jnp.where(lane_p == i, ref[0], 0.0), axis=1, keepdims=True)
    ccol_re, ccol_im = pick(ctre_ref), pick(ctim_ref)
    bx = bx_ref[0, pl.ds(i, 1), :]
    bsw = bsw_ref[0, pl.ds(i, 1), :]
    by = jnp.where(lane2 < S5_N, -bsw, bsw)
    for d in range(2):
        pre, pim = pt_ref[d, 0], pt_ref[d, 1]
        base = d * 2 * S5_N
        wout_ref[0, base:base + S5_N, :] = (ccol_re * pre - ccol_im * pim).astype(BF16)
        wout_ref[0, base + S5_N:base + 2 * S5_N, :] = (-(ccol_re * pim + ccol_im * pre)).astype(BF16)
        wloc_ref[0, :, base:base + 2 * S5_N] = (pw_ref[d, 0] * bx + pw_ref[d, 1] * by).astype(BF16)


def _s5_operators(a_re, a_im, log_dt, b_re, b_im, c_re, c_im):
    L = S5_CHUNK
    g, n2 = S5_GROUPS, 2 * S5_N
    ld = jnp.broadcast_to(log_dt[:, :, None], a_re.shape)
    per_dir = [a_re[0], a_im[0], ld[0], a_re[1], a_im[1], ld[1]]
    zero = jnp.zeros_like(a_re[0])
    cols = jnp.stack(per_dir + [zero, zero], axis=-1)
    dup = lambda v: jnp.concatenate([v, v], axis=-1)
    rows = jnp.stack([dup(v) for v in per_dir + [zero, zero]], axis=1)
    bt_re, bt_im = b_re.transpose(0, 2, 1), b_im.transpose(0, 2, 1)
    bx = jnp.concatenate([bt_re, bt_im], axis=-1)
    bsw = jnp.concatenate([bt_im, bt_re], axis=-1)
    ct_re, ct_im = c_re.transpose(0, 2, 1), c_im.transpose(0, 2, 1)
    per_g = lambda a: pl.BlockSpec((1,) + a.shape[1:], lambda gi, i: (gi, 0, 0))
    ins = (cols, rows, bt_re, bt_im, bx, bsw, c_re, c_im, ct_re, ct_im)
    return pl.pallas_call(
        _s5_ops_kernel,
        grid=(g, S5_P),
        in_specs=[per_g(a) for a in ins],
        out_specs=(pl.BlockSpec((1, S5_FLAT, L), lambda gi, i: (gi, 0, i)),
                   pl.BlockSpec((1, L, 2 * n2), lambda gi, i: (gi, i, 0)),
                   pl.BlockSpec((1, 2 * n2, L), lambda gi, i: (gi, 0, i)),
                   pl.BlockSpec((1, 8, n2), lambda gi, i: (gi, 0, 0))),
        out_shape=(jax.ShapeDtypeStruct((g, S5_FLAT, S5_FLAT), BF16),
                   jax.ShapeDtypeStruct((g, S5_FLAT, 2 * n2), BF16),
                   jax.ShapeDtypeStruct((g, 2 * n2, S5_FLAT), BF16),
                   jax.ShapeDtypeStruct((g, 8, n2), F32)),
        scratch_shapes=[pltpu.VMEM((2, S5_N, 2 * L), F32),
                        pltpu.VMEM((2, 2, S5_N, L), F32),
                        pltpu.VMEM((2, 2, L, n2), F32),
                        pltpu.VMEM((S5_P, 2 * L), F32)],
        compiler_params=_params("arbitrary", "arbitrary"),
        name="s5_operators",
    )(*ins)


def _s5_kernel(u_ref, kb_ref, wloc_ref, wout_ref, ar_ref, y_ref, sin_ref, *, bsz, n_ctx_chunks):
    n = pl.program_id(1)
    n_chunks = u_ref.shape[1] // bsz
    two_n = 2 * S5_N

    @pl.when(n == 0)
    def _():
        sloc = _dot(u_ref[0], wloc_ref[0])
        ar = ar_ref[0]
        n_lat = n_chunks - n_ctx_chunks
        fwd_order = list(range(n_lat, n_chunks)) + list(range(n_lat))
        bwd_order = list(range(n_chunks - 1, n_lat - 1, -1)) + list(range(n_lat - 1, -1, -1))
        for d, order in ((0, fwd_order), (1, bwd_order)):
            a1 = ar[2 * d:2 * d + 1, :]
            a2 = ar[2 * d + 1:2 * d + 2, :]
            st = jnp.zeros((bsz, two_n), F32)
            for c in order:
                sin_ref[c * bsz:(c + 1) * bsz, d * two_n:(d + 1) * two_n] = st.astype(BF16)
                loc = sloc[c * bsz:(c + 1) * bsz, d * two_n:(d + 1) * two_n]
                st = a1 * st + a2 * pltpu.roll(st, S5_N, 1) + loc

    y_ref[0] = _dot(u_ref[0], kb_ref[0]) + _dot(sin_ref[...], wout_ref[0])


def _s5_mix(ut, kbig, w_loc, w_out, a_rows, bsz):
    g, rows, flat = ut.shape
    tn = 512
    kern = functools.partial(_s5_kernel, bsz=bsz, n_ctx_chunks=CTX_LEN // S5_CHUNK)
    return pl.pallas_call(
        kern,
        grid=(g, flat // tn),
        in_specs=[pl.BlockSpec((1, rows, flat), lambda gi, n: (gi, 0, 0)),
                  pl.BlockSpec((1, flat, tn), lambda gi, n: (gi, 0, n)),
                  pl.BlockSpec((1, flat, 4 * S5_N), lambda gi, n: (gi, 0, 0)),
                  pl.BlockSpec((1, 4 * S5_N, tn), lambda gi, n: (gi, 0, n)),
                  pl.BlockSpec((1, 8, 2 * S5_N), lambda gi, n: (gi, 0, 0))],
        out_specs=pl.BlockSpec((1, rows, tn), lambda gi, n: (gi, 0, n)),
        out_shape=jax.ShapeDtypeStruct((g, rows, flat), F32),
        scratch_shapes=[pltpu.VMEM((rows, 4 * S5_N), BF16)],
        compiler_params=_params("arbitrary", "arbitrary"),
        name="s5_mix",
    )(ut, kbig, w_loc, w_out, a_rows)


def _s5_readout(y5_ref, zd_ref, d_ref, gw_ref, gb_ref, yd_ref):
    for cl in range(zd_ref.shape[1] // S5_CHUNK):
        rows = slice(cl * S5_CHUNK, (cl + 1) * S5_CHUNK)
        yt = y5_ref[:, cl, 0, :, :].reshape(W_MIX, S5_CHUNK)
        y = _gelu(yt.T + d_ref[...] * zd_ref[0, rows, :])
        gate = _sigmoid(_dot(y.astype(BF16), gw_ref[...]) + gb_ref[...])
        yd_ref[rows, :] = (y * gate).astype(yd_ref.dtype)


def _out_ffn_kernel(x_ref, ya_ref, yb_ref, yc_ref, y5_ref, zd_ref, modb_ref, modc_ref, d5_ref,
                    gw_ref, gb_ref, wo_ref, l1w_ref, l1b_ref, wi_ref, wd_ref, l2w_ref, l2b_ref,
                    o_ref, yd_ref, x1_ref, h2_ref, a_ref, *, ctx_rows):
    tm = x_ref.shape[1]
    n_lat = tm - ctx_rows
    segs = [(0, n_lat, 0)] + ([(n_lat, tm, 1)] if ctx_rows else [])
    _s5_readout(y5_ref, zd_ref, d5_ref, gw_ref, gb_ref, yd_ref)
    y = (_dot(ya_ref[0], wo_ref[0:W_MIX, :]) + _dot(yb_ref[0], wo_ref[W_MIX:2 * W_MIX, :]) +
         _dot(yc_ref[0], wo_ref[2 * W_MIX:3 * W_MIX, :]) + _dot(yd_ref[...], wo_ref[3 * W_MIX:4 * W_MIX, :]))
    for r0, r1, which in segs:
        g1 = _tile_mod(modb_ref, modc_ref, 2)[which]
        sh2 = _tile_mod(modb_ref, modc_ref, 3)[which]
        sc2 = _tile_mod(modb_ref, modc_ref, 4)[which]
        x1 = _layer_norm(ALPHA * x_ref[0, r0:r1, :] + g1 * y[r0:r1, :], l1w_ref[...], l1b_ref[...])
        x1_ref[r0:r1, :] = x1
        h2_ref[r0:r1, :] = (x1 * (1.0 + sc2) + sh2).astype(BF16)
    h2 = h2_ref[...]
    for kk in range(D_FF // FF_CHUNK):
        cols = slice(kk * FF_CHUNK, (kk + 1) * FF_CHUNK)
        gk = _dot(h2, wi_ref[:, cols])
        uk = _dot(h2, wi_ref[:, D_FF + kk * FF_CHUNK:D_FF + (kk + 1) * FF_CHUNK])
        a_ref[:, cols] = (_silu(gk) * uk).astype(BF16)
    f = _dot(a_ref[...], wd_ref[...])
    for r0, r1, which in segs:
        g2 = _tile_mod(modb_ref, modc_ref, 5)[which]
        o_ref[0, r0:r1, :] = _layer_norm(ALPHA * x1_ref[r0:r1, :] + g2 * f[r0:r1, :],
                                         l2w_ref[...], l2b_ref[...])


def _out_ffn(xcat, ya, yb, yc, y5, zd, mod, s5_d, glu_w, glu_b, w_out, ln1_w, ln1_b, w_ffn_in,
             w_ffn_out, ln2_w, ln2_b, tm, n_out):
    bsz, s, d = xcat.shape
    ctx_rows = CTX_LEN if n_out == s else 0
    tok = lambda w: pl.BlockSpec((1, tm, w), lambda b, j: (b, j, 0))
    row = lambda a: a.reshape(1, -1)
    return pl.pallas_call(
        functools.partial(_out_ffn_kernel, ctx_rows=ctx_rows),
        grid=(bsz, n_out // tm),
        in_specs=[tok(d), tok(W_MIX), tok(W_MIX), tok(W_MIX),
                  pl.BlockSpec((S5_GROUPS, tm // S5_CHUNK, 1, S5_P, S5_CHUNK), lambda b, j: (0, j, b, 0, 0)),
                  tok(W_MIX),
                  pl.BlockSpec((1, 6, d), lambda b, j: (b, 0, 0)),
                  pl.BlockSpec((1, 6, d), lambda b, j: (bsz, 0, 0)),
                  _const_spec((1, W_MIX)), _const_spec((W_MIX, W_MIX)), _const_spec((1, W_MIX)),
                  _const_spec(w_out.shape), _const_spec((1, d)), _const_spec((1, d)),
                  _const_spec(w_ffn_in.shape), _const_spec(w_ffn_out.shape),
                  _const_spec((1, d)), _const_spec((1, d))],
        out_specs=tok(d),
        out_shape=jax.ShapeDtypeStruct((bsz, n_out, d), F32),
        scratch_shapes=[pltpu.VMEM((tm, W_MIX), BF16), pltpu.VMEM((tm, d), F32),
                        pltpu.VMEM((tm, d), BF16), pltpu.VMEM((tm, D_FF), BF16)],
        compiler_params=_params("arbitrary", "arbitrary"),
        name="out_ffn",
    )(xcat, ya, yb, yc, y5, zd, mod, mod, row(s5_d), glu_w.astype(BF16), row(glu_b), w_out,
      row(ln1_w), row(ln1_b), w_ffn_in, w_ffn_out, row(ln2_w), row(ln2_b))


def _reorder_w_in(w):
    off_b = 4 * W_MIX + N_GATES
    gates = w[:, 4 * W_MIX:off_b]
    w_r = jnp.concatenate([w[:, :4 * W_MIX], w[:, off_b:],
                           gates, jnp.zeros((w.shape[0], GATE_PAD - N_GATES), w.dtype)], axis=1)
    return w_r.astype(BF16), gates.T.astype(BF16)


def kernel(x, c, ctx, c_ctx, w_mod, b_mod, w_in, mlstm_gate_bias, mlstm_norm_w, sgu_ln_w, sgu_ln_b, sgu_w, sgu_b, conv_w, conv_b, conv_ln_w, conv_ln_b, s5_a_re, s5_a_im, s5_log_dt, s5_b_re, s5_b_im, s5_c_re, s5_c_im, s5_d, s5_glu_w, s5_glu_b, w_out, ln1_w, ln1_b, w_ffn_in, w_ffn_out, ln2_w, ln2_b):
    bsz, seq, d = x.shape
    s = seq + CTX_LEN
    xcat = jnp.concatenate([x, ctx], axis=1)
    mod_rows = -(-(bsz + 1) // 8) * 8
    cvec = jnp.zeros((mod_rows, d), F32).at[:bsz].set(c).at[bsz].set(c_ctx)
    mod_all = _modulation(cvec, w_mod, b_mod).reshape(DEPTH, mod_rows, 6, d)
    nc5 = s // S5_CHUNK
    for l in range(DEPTH):
        mod = mod_all[l]
        w_r, wg_t = _reorder_w_in(w_in[l])
        k, qt, vt, ot, yb, yc, zd, ut, g, gt = _in_projection(
            xcat, mod, w_r, wg_t, sgu_w[l], sgu_b[l], sgu_ln_w[l], sgu_ln_b[l],
            conv_w[l], conv_b[l], conv_ln_w[l], conv_ln_b[l])
        ya = _mlstm(k, qt, vt, ot, g, gt, mlstm_gate_bias[l], mlstm_norm_w[l])
        kbig, w_loc, w_so, a_rows = _s5_operators(s5_a_re[l], s5_a_im[l], s5_log_dt[l], s5_b_re[l],
                                                  s5_b_im[l], s5_c_re[l], s5_c_im[l])
        y5 = _s5_mix(ut.reshape(S5_GROUPS, nc5 * bsz, S5_FLAT), kbig, w_loc, w_so, a_rows, bsz)
        y5 = y5.reshape(S5_GROUPS, nc5, bsz, S5_P, S5_CHUNK)
        last = l == DEPTH - 1
        xcat = _out_ffn(xcat, ya, yb, yc, y5, zd, mod, s5_d[l], s5_glu_w[l], s5_glu_b[l],
                        w_out[l].astype(BF16), ln1_w[l], ln1_b[l],
                        w_ffn_in[l].astype(BF16), w_ffn_out[l].astype(BF16), ln2_w[l], ln2_b[l],
                        tm=TM_LATENT if last else TM_MAIN, n_out=seq if last else s)
    return xcat
```

```python
import functools

import jax
import jax.numpy as jnp
from jax import lax
from jax.experimental import pallas as pl
from jax.experimental.pallas import tpu as pltpu

F32 = jnp.float32
BF16 = jnp.bfloat16

D_MODEL = 1024
DEPTH = 2
CTX_LEN = 256
GRID_W = 64
W_MIX = 256
HEADS = 4
HEAD_DIM = W_MIX // HEADS
SGU_GROUPS = 4
SGU_GD = W_MIX // SGU_GROUPS
SGU_CHUNK = 128
CONV_WIDTH = 31
CONV_PAD = CONV_WIDTH // 2
CONV_HALO = 32
S5_P = 16
S5_GROUPS = W_MIX // S5_P
S5_N = 64
S5_CHUNK = 128
S5_FLAT = S5_P * S5_CHUNK
D_FF = 2816
FF_CHUNK = 256
ALPHA = (2 * DEPTH) ** 0.25
LN_EPS = 1e-5
TILE = 256
MLSTM_CHUNK = 256
TM_MAIN = 3 * TILE
TM_LATENT = 2 * TILE
CONV_ROWS = max((TILE // GRID_W) * (GRID_W + CONV_HALO), CTX_LEN + CONV_HALO)
N_GATES = 4 * HEADS
GATE_PAD = 128
C_O = 3 * W_MIX
C_B = 4 * W_MIX
C_C = C_B + 2 * W_MIX
C_D = C_C + 2 * W_MIX
C_G = C_D + W_MIX
VMEM_LIMIT = 56 * 1024 * 1024


def _dot(a, b):
    return jnp.dot(a, b, preferred_element_type=F32)


def _dot_nt(a, b):
    return lax.dot_general(a, b, (((1,), (1,)), ((), ())), preferred_element_type=F32)


def _split3(x):
    hi = x.astype(BF16)
    r = x - hi.astype(F32)
    mid = r.astype(BF16)
    lo = (r - mid.astype(F32)).astype(BF16)
    return hi, mid, lo


def _dot3_r(w_bf16, x):
    hi, mid, lo = _split3(x)
    return _dot(w_bf16, hi) + _dot(w_bf16, mid) + _dot(w_bf16, lo)


def _layer_norm(x, w, b):
    mu = jnp.mean(x, axis=-1, keepdims=True)
    d = x - mu
    var = jnp.mean(d * d, axis=-1, keepdims=True)
    return d * lax.rsqrt(var + LN_EPS) * w + b


def _sigmoid(x):
    return jax.nn.sigmoid(x)


def _silu(x):
    return x * jax.nn.sigmoid(x)


def _gelu(x):
    return jax.nn.gelu(x, approximate=True)


def _log_sigmoid(x):
    return jnp.minimum(x, 0.0) - jnp.log1p(jnp.exp(-jnp.abs(x)))


def _params(*sem):
    return pltpu.CompilerParams(dimension_semantics=sem, vmem_limit_bytes=VMEM_LIMIT)


def _const_spec(shape):
    nd = len(shape)
    return pl.BlockSpec(shape, lambda *_: (0,) * nd, pipeline_mode=pl.Buffered(1))


def _mod_kernel(c_ref, w_ref, b_ref, o_ref):
    s = _silu(c_ref[...])
    s_hi = s.astype(BF16)
    s_lo = (s - s_hi.astype(F32)).astype(BF16)
    w = w_ref[0]
    w_hi = w.astype(BF16)
    w_lo = (w - w_hi.astype(F32)).astype(BF16)
    o_ref[0] = _dot(s_hi, w_hi) + _dot(s_hi, w_lo) + _dot(s_lo, w_hi) + b_ref[0]


def _modulation(cvec, w_mod, b_mod):
    depth, d, n = w_mod.shape
    rows = cvec.shape[0]
    tn = 1024
    return pl.pallas_call(
        _mod_kernel,
        grid=(depth, n // tn),
        in_specs=[pl.BlockSpec((rows, d), lambda l, j: (0, 0)),
                  pl.BlockSpec((1, d, tn), lambda l, j: (l, 0, j)),
                  pl.BlockSpec((1, 1, tn), lambda l, j: (l, 0, j))],
        out_specs=pl.BlockSpec((1, rows, tn), lambda l, j: (l, 0, j)),
        out_shape=jax.ShapeDtypeStruct((depth, rows, n), F32),
        compiler_params=_params("arbitrary", "arbitrary"),
        name="modulation",
    )(cvec, w_mod, b_mod.reshape(depth, 1, n))


def _tile_mod(modb_ref, modc_ref, i):
    last = pl.program_id(1) == pl.num_programs(1) - 1
    row_b = modb_ref[0, i:i + 1, :]
    return row_b, jnp.where(last, modc_ref[0, i:i + 1, :], row_b)


def _sgu_slab(zb, sgw_ref, sgb_ref, slnw_ref, slnb_ref):
    u = _gelu(zb[:, 0:W_MIX])
    vn = _layer_norm(_gelu(zb[:, W_MIX:2 * W_MIX]), slnw_ref[...], slnb_ref[...]).astype(BF16)
    lane_grp = lax.broadcasted_iota(jnp.int32, (1, W_MIX), 1) // SGU_GD
    out = []
    for cl in range(TILE // SGU_CHUNK):
        vc = vn[cl * SGU_CHUNK:(cl + 1) * SGU_CHUNK, :]
        acc = sgb_ref[...]
        for gi in range(SGU_GROUPS):
            acc = acc + _dot(sgw_ref[gi], jnp.where(lane_grp == gi, vc, jnp.zeros_like(vc)))
        out.append(acc)
    return u * jnp.concatenate(out, axis=0)


def _conv_slab(a, seg_len, cw_ref, sh_ref):
    n_seg = TILE // seg_len
    stride = seg_len + CONV_HALO
    zeros16 = jnp.zeros((16, W_MIX), F32)
    for sg in range(n_seg):
        base = sg * stride
        seg = a[sg * seg_len:(sg + 1) * seg_len, :]
        for r in range(8):
            sh_ref[r, pl.ds(base, 16), :] = zeros16
            sh_ref[r, pl.ds(base + seg_len + 8, 16), :] = zeros16
            sh_ref[r, pl.ds(base + 16 - r, seg_len), :] = seg
    out = []
    for sg in range(n_seg):
        base = sg * stride
        acc = jnp.zeros((seg_len, W_MIX), F32)
        for t in range(CONV_WIDTH):
            off = t - CONV_PAD
            r = off % 8
            acc = acc + cw_ref[t:t + 1, :] * sh_ref[r, pl.ds(base + (off - r) + 16, seg_len), :]
        out.append(acc)
    return out[0] if n_seg == 1 else jnp.concatenate(out, axis=0)


def _inproj_kernel(x_ref, modb_ref, modc_ref, w_ref, wgt_ref, sgw_ref, sgb_ref, slnw_ref, slnb_ref,
                   cw_ref, cb_ref, clnw_ref, clnb_ref,
                   k_ref, qt_ref, vt_ref, ot_ref, yb_ref, yc_ref, zd_ref, ut_ref, gt_ref,
                   h_ref, sh_ref):
    tm = x_ref.shape[1]
    n_slabs = tm // TILE
    last = pl.program_id(1) == pl.num_programs(1) - 1
    sh_b, sh_c = _tile_mod(modb_ref, modc_ref, 0)
    sc_b, sc_c = _tile_mod(modb_ref, modc_ref, 1)
    n_lat = tm - CTX_LEN
    h_ref[0:n_lat, :] = (x_ref[0, 0:n_lat, :] * (1.0 + sc_b) + sh_b).astype(BF16)
    h_ref[n_lat:tm, :] = (x_ref[0, n_lat:tm, :] * (1.0 + sc_c) + sh_c).astype(BF16)
    for r in range(n_slabs):
        rows = slice(r * TILE, (r + 1) * TILE)
        h = h_ref[rows, :]
        qt_ref[0, :, rows] = _dot(h, w_ref[:, 0:W_MIX]).T.astype(BF16)
        k_ref[0, rows, :] = (_dot(h, w_ref[:, W_MIX:2 * W_MIX]) * (HEAD_DIM ** -0.5)).astype(BF16)
        vt_ref[0, :, rows] = _dot(h, w_ref[:, 2 * W_MIX:3 * W_MIX]).T.astype(BF16)
        ot_ref[0, :, rows] = _dot(h, w_ref[:, C_O:C_B]).T
        zd = _dot(h, w_ref[:, C_D:C_G])
        zd_ref[0, rows, :] = zd
        zdt = zd.T
        for cl in range(TILE // S5_CHUNK):
            blk = zdt[:, cl * S5_CHUNK:(cl + 1) * S5_CHUNK]
            ut_ref[:, r * (TILE // S5_CHUNK) + cl, 0, :, :] = (
                blk.reshape(S5_GROUPS, S5_P, S5_CHUNK).astype(BF16))
        gt_ref[0, :, rows] = _dot_nt(wgt_ref[...], h)
        yb = _sgu_slab(_dot(h, w_ref[:, C_B:C_C]), sgw_ref, sgb_ref, slnw_ref, slnb_ref)
        yb_ref[0, rows, :] = yb.astype(yb_ref.dtype)
        zc = _dot(h, w_ref[:, C_C:C_D])
        a = zc[:, 0:W_MIX] * _sigmoid(zc[:, W_MIX:2 * W_MIX])

        def conv_out(seg_len, a=a, rows=rows, r=r):
            acc = _conv_slab(a, seg_len, cw_ref, sh_ref.at[r])
            yc = _silu(_layer_norm(acc + cb_ref[...], clnw_ref[...], clnb_ref[...]))
            yc_ref[0, rows, :] = yc.astype(yc_ref.dtype)

        if r < n_slabs - 1:
            conv_out(GRID_W)
        else:
            pl.when(jnp.logical_not(last))(functools.partial(conv_out, GRID_W))
            pl.when(last)(functools.partial(conv_out, CTX_LEN))


def _in_projection(xcat, mod, w_r, wg_t, sgu_w, sgu_b, sgu_ln_w, sgu_ln_b, conv_w, conv_b,
                   conv_ln_w, conv_ln_b):
    bsz, s, d = xcat.shape
    tm = TM_MAIN
    nc5 = s // S5_CHUNK
    cpt = tm // S5_CHUNK
    bias_full = jnp.repeat(sgu_b.T, SGU_GD, axis=1)
    row = lambda a: a.reshape(1, W_MIX)
    tok = lambda w: pl.BlockSpec((1, tm, w), lambda b, j: (b, j, 0))
    chan = lambda c: pl.BlockSpec((1, c, tm), lambda b, j: (b, 0, j))
    out_shape = (
        jax.ShapeDtypeStruct((bsz, s, W_MIX), BF16),
        jax.ShapeDtypeStruct((bsz, W_MIX, s), BF16),
        jax.ShapeDtypeStruct((bsz, W_MIX, s), BF16),
        jax.ShapeDtypeStruct((bsz, W_MIX, s), F32),
        jax.ShapeDtypeStruct((bsz, s, W_MIX), BF16),
        jax.ShapeDtypeStruct((bsz, s, W_MIX), BF16),
        jax.ShapeDtypeStruct((bsz, s, W_MIX), F32),
        jax.ShapeDtypeStruct((S5_GROUPS, nc5, bsz, S5_P, S5_CHUNK), BF16),
        jax.ShapeDtypeStruct((bsz, N_GATES, s), F32),
    )
    out_specs = (
        tok(W_MIX), chan(W_MIX), chan(W_MIX), chan(W_MIX), tok(W_MIX), tok(W_MIX), tok(W_MIX),
        pl.BlockSpec((S5_GROUPS, cpt, 1, S5_P, S5_CHUNK), lambda b, j: (0, j, b, 0, 0)),
        chan(N_GATES),
    )
    return pl.pallas_call(
        _inproj_kernel,
        grid=(bsz, s // tm),
        in_specs=[tok(d),
                  pl.BlockSpec((1, 6, d), lambda b, j: (b, 0, 0)),
                  pl.BlockSpec((1, 6, d), lambda b, j: (bsz, 0, 0)),
                  _const_spec(w_r.shape),
                  _const_spec(wg_t.shape),
                  _const_spec((SGU_GROUPS, SGU_CHUNK, SGU_CHUNK)),
                  _const_spec((SGU_CHUNK, W_MIX)),
                  _const_spec((1, W_MIX)), _const_spec((1, W_MIX)),
                  _const_spec((CONV_WIDTH, W_MIX)), _const_spec((1, W_MIX)),
                  _const_spec((1, W_MIX)), _const_spec((1, W_MIX))],
        out_specs=out_specs,
        out_shape=out_shape,
        scratch_shapes=[pltpu.VMEM((tm, d), BF16),
                        pltpu.VMEM((tm // TILE, 8, CONV_ROWS, W_MIX), F32)],
        compiler_params=_params("arbitrary", "arbitrary"),
        name="in_projection",
    )(xcat, mod, mod, w_r, wg_t, sgu_w.astype(BF16), bias_full, row(sgu_ln_w), row(sgu_ln_b),
      conv_w, row(conv_b), row(conv_ln_w), row(conv_ln_b))


def _mlstm_kernel(k_ref, qt_ref, vt_ref, ot_ref, gt_ref, gbt_ref, nw_ref, y_ref,
                  ht_ref, src_ref, lgt_ref, cumt_ref, pm_ref, c_ref, n_ref, m_ref):
    L = MLSTM_CHUNK
    n_chunks = k_ref.shape[1] // L
    row = lax.broadcasted_iota(jnp.int32, (L, L), 0)
    col = lax.broadcasted_iota(jnp.int32, (L, L), 1)
    tri = (col <= row).astype(BF16)
    lane_g = lax.broadcasted_iota(jnp.int32, (1, GATE_PAD), 1)
    sub_g = lax.broadcasted_iota(jnp.int32, (N_GATES, 1), 0)
    is_f_sub = (sub_g % 8) >= HEADS
    lane_w = lax.broadcasted_iota(jnp.int32, (1, W_MIX), 1) // HEAD_DIM
    sub_w = lax.broadcasted_iota(jnp.int32, (W_MIX, 1), 0) // HEAD_DIM
    bd_mask = sub_w == lane_w
    n_mask = sub_g == lane_w
    row4 = lax.broadcasted_iota(jnp.int32, (HEADS * L, L), 0) % L
    col4 = lax.broadcasted_iota(jnp.int32, (HEADS * L, L), 1)
    mask4 = (row4 <= col4, row4 >= col4)

    for c in range(n_chunks):
        cols = slice(c * L, (c + 1) * L)
        grawt = gt_ref[0, :, cols] + gbt_ref[...]
        lgt = jnp.where(is_f_sub, _log_sigmoid(grawt), grawt)
        lgt_ref[:, cols] = lgt
        hi, mid, lo = _split3(lgt)
        pret = _dot_nt(hi, tri) + _dot_nt(mid, tri) + _dot_nt(lo, tri)
        cumt = jnp.where(sub_g >= 8, pret[:, L - 1:L] - pret + lgt, pret)
        cumt_ref[:, cols] = cumt
        srct = jnp.where(is_f_sub, -jnp.inf, lgt - pltpu.roll(cumt, N_GATES - HEADS, 0))
        pm_ref[:, cols] = srct
        padded = jnp.concatenate([srct, jnp.zeros((GATE_PAD - N_GATES, L), F32)], axis=0)
        src_ref[cols, :] = padded.T
    pos = lax.broadcasted_iota(jnp.int32, (1, pm_ref.shape[1]), 1) % L
    pm = pm_ref[...]
    sh = 1
    while sh < L:
        fwd_v = jnp.where(pos >= sh, pltpu.roll(pm, sh, 1), -jnp.inf)
        bwd_v = jnp.where(pos < L - sh, pltpu.roll(pm, pm.shape[1] - sh, 1), -jnp.inf)
        pm = jnp.maximum(pm, jnp.where(sub_g >= 8, bwd_v, fwd_v))
        sh *= 2
    pm_ref[...] = pm

    def chunk(c, d):
        r0 = pl.multiple_of(c * L, L)
        k = k_ref[0, pl.ds(r0, L), :]
        qt = qt_ref[0, :, pl.ds(r0, L)]
        vt = vt_ref[0, :, pl.ds(r0, L)]
        src_cols = src_ref[pl.ds(r0, L), :]
        lgt = lgt_ref[:, pl.ds(r0, L)]
        cumt = cumt_ref[:, pl.ds(r0, L)]
        pm = pm_ref[:, pl.ds(r0, L)]
        tot = cumt[:, L - 1:L] if d == 0 else cumt[:, 0:1]
        m_vec = m_ref[d]
        c_old = c_ref[d]
        n_old = n_ref[d]
        qc = _dot(c_old.astype(BF16), qt)
        qn = _dot(n_old.astype(BF16), qt)
        e_slabs = []
        e_small = jnp.zeros((N_GATES, L), F32)
        decay_row = jnp.zeros((1, W_MIX), F32)
        m_new_vec = m_vec
        srcs, col_terms, khs, w_inters, floors = [], [], [], [], []
        for h in range(HEADS):
            ic = d * 8 + h
            fc = d * 8 + HEADS + h
            m_h = m_vec[:, fc:fc + 1]
            cct = cumt[fc:fc + 1, :]
            inter = cct + m_h
            m_t = jnp.maximum(inter, cct + pm[ic:ic + 1, :])
            w_inters.append(jnp.exp(inter - m_t))
            floors.append(jnp.exp(-m_t))
            col_terms.append(jnp.broadcast_to(cct - m_t, (L, L)))
            srcs.append(src_cols[:, ic:ic + 1])
            khs.append(jnp.where(lane_w == h, k, jnp.zeros_like(k)))
        arg = jnp.concatenate(srcs, axis=0) + jnp.concatenate(col_terms, axis=0)
        arg = jnp.where(mask4[d], arg, -jnp.inf)
        sc = _dot(jnp.concatenate(khs, axis=0), qt) * jnp.exp(arg)
        scb = sc.astype(BF16)
        for h in range(HEADS):
            ic = d * 8 + h
            fc = d * 8 + HEADS + h
            head_lanes = lane_w == h
            sl = slice(h * HEAD_DIM, (h + 1) * HEAD_DIM)
            hrows = slice(h * L, (h + 1) * L)
            m_h = m_vec[:, fc:fc + 1]
            cct = cumt[fc:fc + 1, :]
            den = jnp.sum(sc[hrows, :], axis=0, keepdims=True) + w_inters[h] * qn[h:h + 1, :]
            num = _dot(vt[sl, :], scb[hrows, :]) + w_inters[h] * qc[sl, :]
            ht_ref[d, sl, pl.ds(r0, L)] = num / jnp.maximum(jnp.abs(den), floors[h])
            tot_h = tot[fc:fc + 1, :]
            logu = tot_h - cct + lgt[ic:ic + 1, :]
            m_new = jnp.maximum(tot_h + m_h, jnp.max(logu, axis=1, keepdims=True))
            e_h = jnp.exp(logu - m_new)
            e_slabs.append(jnp.broadcast_to(e_h, (HEAD_DIM, L)))
            e_small = jnp.where(sub_g == h, e_h, e_small)
            decay_row = jnp.where(head_lanes, jnp.exp(tot_h + m_h - m_new), decay_row)
            m_new_vec = jnp.where(lane_g == fc, m_new, m_new_vec)
        vet = (vt.astype(F32) * jnp.concatenate(e_slabs, axis=0)).astype(BF16)
        c_ref[d] = decay_row * c_old + jnp.where(bd_mask, _dot(vet, k), 0.0)
        n_ref[d] = decay_row * n_old + jnp.where(n_mask, _dot(e_small.astype(BF16), k), 0.0)
        m_ref[d] = m_new_vec

    c_ref[...] = jnp.zeros_like(c_ref)
    n_ref[...] = jnp.zeros_like(n_ref)
    m_ref[...] = jnp.zeros_like(m_ref)
    n_lat = n_chunks - CTX_LEN // L
    for j in range(n_chunks - n_lat):
        chunk(n_lat + j, 0)
        chunk(n_chunks - 1 - j, 1)

    def scan_body(i, carry):
        chunk(i, 0)
        chunk(n_lat - 1 - i, 1)
        return carry

    lax.fori_loop(0, n_lat, scan_body, 0)

    def readout(c, carry):
        r0 = pl.multiple_of(c * L, L)
        slabs = []
        for h in range(HEADS):
            sl = slice(h * HEAD_DIM, (h + 1) * HEAD_DIM)
            hs = ht_ref[0, sl, pl.ds(r0, L)] + ht_ref[1, sl, pl.ds(r0, L)]
            mu = jnp.mean(hs, axis=0, keepdims=True)
            dl = hs - mu
            var = jnp.mean(dl * dl, axis=0, keepdims=True)
            hn = dl * lax.rsqrt(var + LN_EPS) * nw_ref[sl, :]
            slabs.append(hn * _sigmoid(ot_ref[0, sl, pl.ds(r0, L)]))
        y_ref[0, pl.ds(r0, L), :] = jnp.concatenate(slabs, axis=0).T.astype(y_ref.dtype)
        return carry

    lax.fori_loop(0, n_chunks, readout, 0)


def _mlstm(k, qt, vt, ot, gt, gate_bias, norm_w):
    bsz, s, _ = k.shape
    gbt = gate_bias.reshape(N_GATES, 1)
    nw_full = jnp.broadcast_to(norm_w.reshape(W_MIX, 1), (W_MIX, MLSTM_CHUNK))
    tok = lambda w: pl.BlockSpec((1, s, w), lambda b: (b, 0, 0))
    chan = lambda c: pl.BlockSpec((1, c, s), lambda b: (b, 0, 0))
    return pl.pallas_call(
        _mlstm_kernel,
        grid=(bsz,),
        in_specs=[tok(W_MIX), chan(W_MIX), chan(W_MIX), chan(W_MIX), chan(N_GATES),
                  pl.BlockSpec((N_GATES, 1), lambda b: (0, 0)),
                  pl.BlockSpec((W_MIX, MLSTM_CHUNK), lambda b: (0, 0))],
        out_specs=tok(W_MIX),
        out_shape=jax.ShapeDtypeStruct((bsz, s, W_MIX), BF16),
        scratch_shapes=[pltpu.VMEM((2, W_MIX, s), F32),
                        pltpu.VMEM((s, GATE_PAD), F32),
                        pltpu.VMEM((N_GATES, s), F32),
                        pltpu.VMEM((N_GATES, s), F32),
                        pltpu.VMEM((N_GATES, s), F32),
                        pltpu.VMEM((2, W_MIX, W_MIX), F32),
                        pltpu.VMEM((2, N_GATES, W_MIX), F32),
                        pltpu.VMEM((2, 1, GATE_PAD), F32)],
        compiler_params=_params("arbitrary"),
        name="mlstm",
    )(k, qt, vt, ot, gt, gbt, nw_full)


def _cpow(ar, ai, ld, k):
    dt = jnp.exp(ld)
    mag = jnp.exp(k * (ar * dt))
    ang = k * (ai * dt)
    return mag * jnp.cos(ang), mag * jnp.sin(ang)


def _zoh_coef(ar, ai, ld):
    er, ei = _cpow(ar, ai, ld, 1.0)
    den = ar * ar + ai * ai
    return ((er - 1.0) * ar + ei * ai) / den, (ei * ar - (er - 1.0) * ai) / den


def _dot_x3(a, b):
    a_hi, a_mid, _ = _split3(a)
    b_hi, b_mid, _ = _split3(b)
    return _dot(a_hi, b_hi) + _dot(a_hi, b_mid) + _dot(a_mid, b_hi)


def _s5_ops_kernel(cols_ref, rows_ref, btre_ref, btim_ref, bx_ref, bsw_ref, cre_ref, cim_ref,
                   ctre_ref, ctim_ref, kb_ref, wloc_ref, wout_ref, ar_ref,
                   mt_ref, pt_ref, pw_ref, v_ref):
    L = S5_CHUNK
    i = pl.program_id(1)
    lane2 = lax.broadcasted_iota(jnp.int32, (1, 2 * S5_N), 1)

    @pl.when(i == 0)
    def _():
        col = lambda k: cols_ref[0, :, k:k + 1]
        row = lambda k: rows_ref[0, k:k + 1, :]
        cf = [_zoh_coef(col(3 * d), col(3 * d + 1), col(3 * d + 2)) for d in range(2)]
        jj = lax.broadcasted_iota(jnp.int32, (1, 2 * L), 1)
        fwd = jj >= L
        k = jnp.abs(jj - L).astype(F32)
        sel = lambda a, b: jnp.where(fwd, a, b)
        pre, pim = _cpow(sel(col(0), col(3)), sel(col(1), col(4)), sel(col(2), col(5)), k)
        cre_, cim_ = sel(cf[0][0], cf[1][0]), sel(cf[0][1], cf[1][1])
        m_re = pre * cre_ - pim * cim_
        m_im = pre * cim_ + pim * cre_
        mt_ref[0] = m_re + jnp.where(jj == L, cf[1][0], 0.0)
        mt_ref[1] = m_im + jnp.where(jj == L, cf[1][1], 0.0)
        tt = lax.broadcasted_iota(jnp.int32, (1, L), 1)
        ss = lax.broadcasted_iota(jnp.int32, (L, 1), 0)
        for d in range(2):
            kk = (tt + 1 if d == 0 else L - tt).astype(F32)
            pt_ref[d, 0], pt_ref[d, 1] = _cpow(col(3 * d), col(3 * d + 1), col(3 * d + 2), kk)
            ks = (L - 1 - ss if d == 0 else ss).astype(F32)
            qre, qim = _cpow(row(3 * d), row(3 * d + 1), row(3 * d + 2), ks)
            fre, fim = _zoh_coef(row(3 * d), row(3 * d + 1), row(3 * d + 2))
            pw_ref[d, 0] = qre * fre - qim * fim
            pw_ref[d, 1] = qre * fim + qim * fre
            lre, lim = _cpow(row(3 * d), row(3 * d + 1), row(3 * d + 2), float(L))
            ar_ref[0, 2 * d:2 * d + 1, :] = lre
            ar_ref[0, 2 * d + 1:2 * d + 2, :] = jnp.where(lane2 < S5_N, -lim, lim)
        ar_ref[0, 4:8, :] = jnp.zeros((4, 2 * S5_N), F32)

    cre = cre_ref[0, pl.ds(i, 1), :]
    cim = cim_ref[0, pl.ds(i, 1), :]
    g_re = cre * btre_ref[0] - cim * btim_ref[0]
    g_im = cre * btim_ref[0] + cim * btre_ref[0]
    v_ref[...] = _dot_x3(g_re, mt_ref[0]) - _dot_x3(g_im, mt_ref[1])
    for q in range(S5_P):
        vq = jnp.broadcast_to(v_ref[q:q + 1, :], (L, 2 * L))
        toe = pltpu.roll(vq, 0, 1, stride=1, stride_axis=0)
        kb_ref[0, q * L:(q + 1) * L, :] = toe[:, L:2 * L].astype(BF16)

    lane_p = lax.broadcasted_iota(jnp.int32, (1, S5_P), 1)
    pick = lambda ref: jnp.sum(jnp.where(lane_p == i, ref[0], 0.0), axis=1, keepdims=True)
    ccol_re, ccol_im = pick(ctre_ref), pick(ctim_ref)
    bx = bx_ref[0, pl.ds(i, 1), :]
    bsw = bsw_ref[0, pl.ds(i, 1), :]
    by = jnp.where(lane2 < S5_N, -bsw, bsw)
    for d in range(2):
        pre, pim = pt_ref[d, 0], pt_ref[d, 1]
        base = d * 2 * S5_N
        wout_ref[0, base:base + S5_N, :] = (ccol_re * pre - ccol_im * pim).astype(BF16)
        wout_ref[0, base + S5_N:base + 2 * S5_N, :] = (-(ccol_re * pim + ccol_im * pre)).astype(BF16)
        wloc_ref[0, :, base:base + 2 * S5_N] = (pw_ref[d, 0] * bx + pw_ref[d, 1] * by).astype(BF16)


def _s5_operators(a_re, a_im, log_dt, b_re, b_im, c_re, c_im):
    L = S5_CHUNK
    g, n2 = S5_GROUPS, 2 * S5_N
    ld = jnp.broadcast_to(log_dt[:, :, None], a_re.shape)
    per_dir = [a_re[0], a_im[0], ld[0], a_re[1], a_im[1], ld[1]]
    zero = jnp.zeros_like(a_re[0])
    cols = jnp.stack(per_dir + [zero, zero], axis=-1)
    dup = lambda v: jnp.concatenate([v, v], axis=-1)
    rows = jnp.stack([dup(v) for v in per_dir + [zero, zero]], axis=1)
    bt_re, bt_im = b_re.transpose(0, 2, 1), b_im.transpose(0, 2, 1)
    bx = jnp.concatenate([bt_re, bt_im], axis=-1)
    bsw = jnp.concatenate([bt_im, bt_re], axis=-1)
    ct_re, ct_im = c_re.transpose(0, 2, 1), c_im.transpose(0, 2, 1)
    per_g = lambda a: pl.BlockSpec((1,) + a.shape[1:], lambda gi, i: (gi, 0, 0))
    ins = (cols, rows, bt_re, bt_im, bx, bsw, c_re, c_im, ct_re, ct_im)
    return pl.pallas_call(
        _s5_ops_kernel,
        grid=(g, S5_P),
        in_specs=[per_g(a) for a in ins],
        out_specs=(pl.BlockSpec((1, S5_FLAT, L), lambda gi, i: (gi, 0, i)),
                   pl.BlockSpec((1, L, 2 * n2), lambda gi, i: (gi, i, 0)),
                   pl.BlockSpec((1, 2 * n2, L), lambda gi, i: (gi, 0, i)),
                   pl.BlockSpec((1, 8, n2), lambda gi, i: (gi, 0, 0))),
        out_shape=(jax.ShapeDtypeStruct((g, S5_FLAT, S5_FLAT), BF16),
                   jax.ShapeDtypeStruct((g, S5_FLAT, 2 * n2), BF16),
                   jax.ShapeDtypeStruct((g, 2 * n2, S5_FLAT), BF16),
                   jax.ShapeDtypeStruct((g, 8, n2), F32)),
        scratch_shapes=[pltpu.VMEM((2, S5_N, 2 * L), F32),
                        pltpu.VMEM((2, 2, S5_N, L), F32),
                        pltpu.VMEM((2, 2, L, n2), F32),
                        pltpu.VMEM((S5_P, 2 * L), F32)],
        compiler_params=_params("arbitrary", "arbitrary"),
        name="s5_operators",
    )(*ins)


def _s5_kernel(u_ref, kb_ref, wloc_ref, wout_ref, ar_ref, y_ref, sin_ref, *, bsz, n_ctx_chunks):
    n = pl.program_id(1)
    n_chunks = u_ref.shape[1] // bsz
    two_n = 2 * S5_N

    @pl.when(n == 0)
    def _():
        sloc = _dot(u_ref[0], wloc_ref[0])
        ar = ar_ref[0]
        n_lat = n_chunks - n_ctx_chunks
        fwd_order = list(range(n_lat, n_chunks)) + list(range(n_lat))
        bwd_order = list(range(n_chunks - 1, n_lat - 1, -1)) + list(range(n_lat - 1, -1, -1))
        for d, order in ((0, fwd_order), (1, bwd_order)):
            a1 = ar[2 * d:2 * d + 1, :]
            a2 = ar[2 * d + 1:2 * d + 2, :]
            st = jnp.zeros((bsz, two_n), F32)
            for c in order:
                sin_ref[c * bsz:(c + 1) * bsz, d * two_n:(d + 1) * two_n] = st.astype(BF16)
                loc = sloc[c * bsz:(c + 1) * bsz, d * two_n:(d + 1) * two_n]
                st = a1 * st + a2 * pltpu.roll(st, S5_N, 1) + loc

    y_ref[0] = _dot(u_ref[0], kb_ref[0]) + _dot(sin_ref[...], wout_ref[0])


def _s5_mix(ut, kbig, w_loc, w_out, a_rows, bsz):
    g, rows, flat = ut.shape
    tn = 512
    kern = functools.partial(_s5_kernel, bsz=bsz, n_ctx_chunks=CTX_LEN // S5_CHUNK)
    return pl.pallas_call(
        kern,
        grid=(g, flat // tn),
        in_specs=[pl.BlockSpec((1, rows, flat), lambda gi, n: (gi, 0, 0)),
                  pl.BlockSpec((1, flat, tn), lambda gi, n: (gi, 0, n)),
                  pl.BlockSpec((1, flat, 4 * S5_N), lambda gi, n: (gi, 0, 0)),
                  pl.BlockSpec((1, 4 * S5_N, tn), lambda gi, n: (gi, 0, n)),
                  pl.BlockSpec((1, 8, 2 * S5_N), lambda gi, n: (gi, 0, 0))],
        out_specs=pl.BlockSpec((1, rows, tn), lambda gi, n: (gi, 0, n)),
        out_shape=jax.ShapeDtypeStruct((g, rows, flat), F32),
        scratch_shapes=[pltpu.VMEM((rows, 4 * S5_N), BF16)],
        compiler_params=_params("arbitrary", "arbitrary"),
        name="s5_mix",
    )(ut, kbig, w_loc, w_out, a_rows)


def _s5_readout(y5_ref, zd_ref, d_ref, gw_ref, gb_ref, yd_ref):
    for cl in range(zd_ref.shape[1] // S5_CHUNK):
        rows = slice(cl * S5_CHUNK, (cl + 1) * S5_CHUNK)
        yt = y5_ref[:, cl, 0, :, :].reshape(W_MIX, S5_CHUNK)
        y = _gelu(yt.T + d_ref[...] * zd_ref[0, rows, :])
        gate = _sigmoid(_dot(y.astype(BF16), gw_ref[...]) + gb_ref[...])
        yd_ref[rows, :] = (y * gate).astype(yd_ref.dtype)


def _out_ffn_kernel(x_ref, ya_ref, yb_ref, yc_ref, y5_ref, zd_ref, modb_ref, modc_ref, d5_ref,
                    gw_ref, gb_ref, wo_ref, l1w_ref, l1b_ref, wi_ref, wd_ref, l2w_ref, l2b_ref,
                    o_ref, yd_ref, x1_ref, h2_ref, a_ref, *, ctx_rows):
    tm = x_ref.shape[1]
    n_lat = tm - ctx_rows
    segs = [(0, n_lat, 0)] + ([(n_lat, tm, 1)] if ctx_rows else [])
    _s5_readout(y5_ref, zd_ref, d5_ref, gw_ref, gb_ref, yd_ref)
    y = (_dot(ya_ref[0], wo_ref[0:W_MIX, :]) + _dot(yb_ref[0], wo_ref[W_MIX:2 * W_MIX, :]) +
         _dot(yc_ref[0], wo_ref[2 * W_MIX:3 * W_MIX, :]) + _dot(yd_ref[...], wo_ref[3 * W_MIX:4 * W_MIX, :]))
    for r0, r1, which in segs:
        g1 = _tile_mod(modb_ref, modc_ref, 2)[which]
        sh2 = _tile_mod(modb_ref, modc_ref, 3)[which]
        sc2 = _tile_mod(modb_ref, modc_ref, 4)[which]
        x1 = _layer_norm(ALPHA * x_ref[0, r0:r1, :] + g1 * y[r0:r1, :], l1w_ref[...], l1b_ref[...])
        x1_ref[r0:r1, :] = x1
        h2_ref[r0:r1, :] = (x1 * (1.0 + sc2) + sh2).astype(BF16)
    h2 = h2_ref[...]
    for kk in range(D_FF // FF_CHUNK):
        cols = slice(kk * FF_CHUNK, (kk + 1) * FF_CHUNK)
        gk = _dot(h2, wi_ref[:, cols])
        uk = _dot(h2, wi_ref[:, D_FF + kk * FF_CHUNK:D_FF + (kk + 1) * FF_CHUNK])
        a_ref[:, cols] = (_silu(gk) * uk).astype(BF16)
    f = _dot(a_ref[...], wd_ref[...])
    for r0, r1, which in segs:
        g2 = _tile_mod(modb_ref, modc_ref, 5)[which]
        o_ref[0, r0:r1, :] = _layer_norm(ALPHA * x1_ref[r0:r1, :] + g2 * f[r0:r1, :],
                                         l2w_ref[...], l2b_ref[...])


def _out_ffn(xcat, ya, yb, yc, y5, zd, mod, s5_d, glu_w, glu_b, w_out, ln1_w, ln1_b, w_ffn_in,
             w_ffn_out, ln2_w, ln2_b, tm, n_out):
    bsz, s, d = xcat.shape
    ctx_rows = CTX_LEN if n_out == s else 0
    tok = lambda w: pl.BlockSpec((1, tm, w), lambda b, j: (b, j, 0))
    row = lambda a: a.reshape(1, -1)
    return pl.pallas_call(
        functools.partial(_out_ffn_kernel, ctx_rows=ctx_rows),
        grid=(bsz, n_out // tm),
        in_specs=[tok(d), tok(W_MIX), tok(W_MIX), tok(W_MIX),
                  pl.BlockSpec((S5_GROUPS, tm // S5_CHUNK, 1, S5_P, S5_CHUNK), lambda b, j: (0, j, b, 0, 0)),
                  tok(W_MIX),
                  pl.BlockSpec((1, 6, d), lambda b, j: (b, 0, 0)),
                  pl.BlockSpec((1, 6, d), lambda b, j: (bsz, 0, 0)),
                  _const_spec((1, W_MIX)), _const_spec((W_MIX, W_MIX)), _const_spec((1, W_MIX)),
                  _const_spec(w_out.shape), _const_spec((1, d)), _const_spec((1, d)),
                  _const_spec(w_ffn_in.shape), _const_spec(w_ffn_out.shape),
                  _const_spec((1, d)), _const_spec((1, d))],
        out_specs=tok(d),
        out_shape=jax.ShapeDtypeStruct((bsz, n_out, d), F32),
        scratch_shapes=[pltpu.VMEM((tm, W_MIX), BF16), pltpu.VMEM((tm, d), F32),
                        pltpu.VMEM((tm, d), BF16), pltpu.VMEM((tm, D_FF), BF16)],
        compiler_params=_params("arbitrary", "arbitrary"),
        name="out_ffn",
    )(xcat, ya, yb, yc, y5, zd, mod, mod, row(s5_d), glu_w.astype(BF16), row(glu_b), w_out,
      row(ln1_w), row(ln1_b), w_ffn_in, w_ffn_out, row(ln2_w), row(ln2_b))


def _reorder_w_in(w):
    off_b = 4 * W_MIX + N_GATES
    w_r = jnp.concatenate([w[:, :4 * W_MIX], w[:, off_b:]], axis=1)
    return w_r.astype(BF16), w[:, 4 * W_MIX:off_b].T.astype(BF16)


def kernel(x, c, ctx, c_ctx, w_mod, b_mod, w_in, mlstm_gate_bias, mlstm_norm_w, sgu_ln_w, sgu_ln_b, sgu_w, sgu_b, conv_w, conv_b, conv_ln_w, conv_ln_b, s5_a_re, s5_a_im, s5_log_dt, s5_b_re, s5_b_im, s5_c_re, s5_c_im, s5_d, s5_glu_w, s5_glu_b, w_out, ln1_w, ln1_b, w_ffn_in, w_ffn_out, ln2_w, ln2_b):
    bsz, seq, d = x.shape
    s = seq + CTX_LEN
    xcat = jnp.concatenate([x, ctx], axis=1)
    mod_rows = -(-(bsz + 1) // 8) * 8
    cvec = jnp.zeros((mod_rows, d), F32).at[:bsz].set(c).at[bsz].set(c_ctx)
    mod_all = _modulation(cvec, w_mod, b_mod).reshape(DEPTH, mod_rows, 6, d)
    nc5 = s // S5_CHUNK
    for l in range(DEPTH):
        mod = mod_all[l]
        w_r, wg_t = _reorder_w_in(w_in[l])
        k, qt, vt, ot, yb, yc, zd, ut, gt = _in_projection(
            xcat, mod, w_r, wg_t, sgu_w[l], sgu_b[l], sgu_ln_w[l], sgu_ln_b[l],
            conv_w[l], conv_b[l], conv_ln_w[l], conv_ln_b[l])
        ya = _mlstm(k, qt, vt, ot, gt, mlstm_gate_bias[l], mlstm_norm_w[l])
        kbig, w_loc, w_so, a_rows = _s5_operators(s5_a_re[l], s5_a_im[l], s5_log_dt[l], s5_b_re[l],
                                                  s5_b_im[l], s5_c_re[l], s5_c_im[l])
        y5 = _s5_mix(ut.reshape(S5_GROUPS, nc5 * bsz, S5_FLAT), kbig, w_loc, w_so, a_rows, bsz)
        y5 = y5.reshape(S5_GROUPS, nc5, bsz, S5_P, S5_CHUNK)
        last = l == DEPTH - 1
        xcat = _out_ffn(xcat, ya, yb, yc, y5, zd, mod, s5_d[l], s5_glu_w[l], s5_glu_b[l],
                        w_out[l].astype(BF16), ln1_w[l], ln1_b[l],
                        w_ffn_in[l].astype(BF16), w_ffn_out[l].astype(BF16), ln2_w[l], ln2_b[l],
                        tm=TM_LATENT if last else TM_MAIN, n_out=seq if last else s)
    return xcat
```

```python
import functools

import numpy as np
import jax
import jax.numpy as jnp
from jax import lax
from jax.experimental import pallas as pl
from jax.experimental.pallas import tpu as pltpu

F32 = jnp.float32
BF16 = jnp.bfloat16

D_MODEL = 1024
DEPTH = 2
CTX_LEN = 256
GRID_W = 64
W_MIX = 256
HEADS = 4
HEAD_DIM = W_MIX // HEADS
SGU_GROUPS = 4
SGU_GD = W_MIX // SGU_GROUPS
SGU_CHUNK = 128
CONV_WIDTH = 31
CONV_PAD = CONV_WIDTH // 2
CONV_HALO = 32
S5_P = 16
S5_GROUPS = W_MIX // S5_P
S5_N = 64
S5_CHUNK = 64
S5_HALF = S5_P // 2
S5_FLAT = S5_P * S5_CHUNK
S5_STATE = 8 * S5_N
D_FF = 2816
FF_CHUNK = 256
ALPHA = (2 * DEPTH) ** 0.25
LN_EPS = 1e-5
TILE = 256
MLSTM_CHUNK = 256
TM_MAIN = 3 * TILE
TM_LATENT = 2 * TILE
CONV_ROWS = max((TILE // GRID_W) * (GRID_W + CONV_HALO), CTX_LEN + CONV_HALO)
N_GATES = 4 * HEADS
GATE_PAD = 128
C_O = 3 * W_MIX
C_B = 4 * W_MIX
C_C = C_B + 2 * W_MIX
C_D = C_C + 2 * W_MIX
C_G = C_D + W_MIX
VMEM_LIMIT = 56 * 1024 * 1024


def _dot(a, b):
    return jnp.dot(a, b, preferred_element_type=F32)


def _dot_nt(a, b):
    return lax.dot_general(a, b, (((1,), (1,)), ((), ())), preferred_element_type=F32)


def _split3(x):
    hi = x.astype(BF16)
    r = x - hi.astype(F32)
    mid = r.astype(BF16)
    lo = (r - mid.astype(F32)).astype(BF16)
    return hi, mid, lo


def _dot3_r(w_bf16, x):
    hi, mid, lo = _split3(x)
    return _dot(w_bf16, hi) + _dot(w_bf16, mid) + _dot(w_bf16, lo)


def _layer_norm(x, w, b):
    mu = jnp.mean(x, axis=-1, keepdims=True)
    d = x - mu
    var = jnp.mean(d * d, axis=-1, keepdims=True)
    return d * lax.rsqrt(var + LN_EPS) * w + b


def _sigmoid(x):
    return jax.nn.sigmoid(x)


def _silu(x):
    return x * jax.nn.sigmoid(x)


def _gelu(x):
    return jax.nn.gelu(x, approximate=True)


def _log_sigmoid(x):
    return jnp.minimum(x, 0.0) - jnp.log1p(jnp.exp(-jnp.abs(x)))


def _params(*sem):
    return pltpu.CompilerParams(dimension_semantics=sem, vmem_limit_bytes=VMEM_LIMIT)


def _const_spec(shape):
    nd = len(shape)
    return pl.BlockSpec(shape, lambda *_: (0,) * nd, pipeline_mode=pl.Buffered(1))


def _mod_kernel(c_ref, w_ref, b_ref, o_ref):
    s = _silu(c_ref[...])
    s_hi = s.astype(BF16)
    s_lo = (s - s_hi.astype(F32)).astype(BF16)
    w = w_ref[0]
    w_hi = w.astype(BF16)
    w_lo = (w - w_hi.astype(F32)).astype(BF16)
    o_ref[0] = _dot(s_hi, w_hi) + _dot(s_hi, w_lo) + _dot(s_lo, w_hi) + b_ref[0]


def _modulation(cvec, w_mod, b_mod):
    depth, d, n = w_mod.shape
    rows = cvec.shape[0]
    tn = 1024
    return pl.pallas_call(
        _mod_kernel,
        grid=(depth, n // tn),
        in_specs=[pl.BlockSpec((rows, d), lambda l, j: (0, 0)),
                  pl.BlockSpec((1, d, tn), lambda l, j: (l, 0, j)),
                  pl.BlockSpec((1, 1, tn), lambda l, j: (l, 0, j))],
        out_specs=pl.BlockSpec((1, rows, tn), lambda l, j: (l, 0, j)),
        out_shape=jax.ShapeDtypeStruct((depth, rows, n), F32),
        compiler_params=_params("arbitrary", "arbitrary"),
        name="modulation",
    )(cvec, w_mod, b_mod.reshape(depth, 1, n))


def _tile_mod(modb_ref, modc_ref, i):
    last = pl.program_id(1) == pl.num_programs(1) - 1
    row_b = modb_ref[0, i:i + 1, :]
    return row_b, jnp.where(last, modc_ref[0, i:i + 1, :], row_b)


def _sgu_slab(zb, sgw_ref, sgb_ref, slnw_ref, slnb_ref):
    u = _gelu(zb[:, 0:W_MIX])
    vn = _layer_norm(_gelu(zb[:, W_MIX:2 * W_MIX]), slnw_ref[...], slnb_ref[...]).astype(BF16)
    lane_grp = lax.broadcasted_iota(jnp.int32, (1, W_MIX), 1) // SGU_GD
    out = []
    for cl in range(TILE // SGU_CHUNK):
        vc = vn[cl * SGU_CHUNK:(cl + 1) * SGU_CHUNK, :]
        acc = sgb_ref[...]
        for gi in range(SGU_GROUPS):
            acc = acc + _dot(sgw_ref[gi], jnp.where(lane_grp == gi, vc, jnp.zeros_like(vc)))
        out.append(acc)
    return u * jnp.concatenate(out, axis=0)


def _conv_slab(a, seg_len, cw_ref, sh_ref):
    n_seg = TILE // seg_len
    stride = seg_len + CONV_HALO
    zeros16 = jnp.zeros((16, W_MIX), F32)
    for sg in range(n_seg):
        base = sg * stride
        seg = a[sg * seg_len:(sg + 1) * seg_len, :]
        for r in range(8):
            sh_ref[r, pl.ds(base, 16), :] = zeros16
            sh_ref[r, pl.ds(base + seg_len + 8, 16), :] = zeros16
            sh_ref[r, pl.ds(base + 16 - r, seg_len), :] = seg
    out = []
    for sg in range(n_seg):
        base = sg * stride
        acc = jnp.zeros((seg_len, W_MIX), F32)
        for t in range(CONV_WIDTH):
            off = t - CONV_PAD
            r = off % 8
            acc = acc + cw_ref[t:t + 1, :] * sh_ref[r, pl.ds(base + (off - r) + 16, seg_len), :]
        out.append(acc)
    return out[0] if n_seg == 1 else jnp.concatenate(out, axis=0)


def _emit_s5_rows(zdt, ut_ref, chunk0):
    n_tok = zdt.shape[1]
    lo = lax.broadcasted_iota(jnp.int32, (1, 2 * S5_CHUNK), 1) < S5_CHUNK
    halves = zdt.reshape(S5_GROUPS, 2, S5_HALF, n_tok)
    first = halves[:, 0].reshape(S5_GROUPS * S5_HALF, n_tok)
    second = halves[:, 1].reshape(S5_GROUPS * S5_HALF, n_tok)
    for j in range(n_tok // (2 * S5_CHUNK)):
        a = first[:, j * 2 * S5_CHUNK:(j + 1) * 2 * S5_CHUNK]
        b = second[:, j * 2 * S5_CHUNK:(j + 1) * 2 * S5_CHUNK]
        pieces = (jnp.where(lo, a, pltpu.roll(b, S5_CHUNK, 1)),
                  jnp.where(lo, pltpu.roll(a, S5_CHUNK, 1), b))
        for cc, piece in enumerate(pieces):
            ut_ref[:, chunk0 + 2 * j + cc, 0, :, :] = (
                piece.reshape(S5_GROUPS, S5_HALF, 2 * S5_CHUNK).astype(BF16))


def _inproj_kernel(x_ref, modb_ref, modc_ref, w_ref, wgt_ref, sgw_ref, sgb_ref, slnw_ref, slnb_ref,
                   cw_ref, cb_ref, clnw_ref, clnb_ref,
                   k_ref, qt_ref, vt_ref, ot_ref, yb_ref, yc_ref, zd_ref, ut_ref, gt_ref,
                   h_ref, sh_ref):
    tm = x_ref.shape[1]
    n_slabs = tm // TILE
    last = pl.program_id(1) == pl.num_programs(1) - 1
    sh_b, sh_c = _tile_mod(modb_ref, modc_ref, 0)
    sc_b, sc_c = _tile_mod(modb_ref, modc_ref, 1)
    n_lat = tm - CTX_LEN
    h_ref[0:n_lat, :] = (x_ref[0, 0:n_lat, :] * (1.0 + sc_b) + sh_b).astype(BF16)
    h_ref[n_lat:tm, :] = (x_ref[0, n_lat:tm, :] * (1.0 + sc_c) + sh_c).astype(BF16)
    for r in range(n_slabs):
        rows = slice(r * TILE, (r + 1) * TILE)
        h = h_ref[rows, :]
        qt_ref[0, :, rows] = _dot(h, w_ref[:, 0:W_MIX]).T.astype(BF16)
        k_ref[0, rows, :] = (_dot(h, w_ref[:, W_MIX:2 * W_MIX]) * (HEAD_DIM ** -0.5)).astype(BF16)
        vt_ref[0, :, rows] = _dot(h, w_ref[:, 2 * W_MIX:3 * W_MIX]).T.astype(BF16)
        ot_ref[0, :, rows] = _dot(h, w_ref[:, C_O:C_B]).T
        zd = _dot(h, w_ref[:, C_D:C_G])
        zd_ref[0, rows, :] = zd
        _emit_s5_rows(zd.T, ut_ref, r * (TILE // S5_CHUNK))
        gt_ref[0, :, rows] = _dot_nt(wgt_ref[...], h)
        yb = _sgu_slab(_dot(h, w_ref[:, C_B:C_C]), sgw_ref, sgb_ref, slnw_ref, slnb_ref)
        yb_ref[0, rows, :] = yb.astype(yb_ref.dtype)
        zc = _dot(h, w_ref[:, C_C:C_D])
        a = zc[:, 0:W_MIX] * _sigmoid(zc[:, W_MIX:2 * W_MIX])

        def conv_out(seg_len, a=a, rows=rows, r=r):
            acc = _conv_slab(a, seg_len, cw_ref, sh_ref.at[r])
            yc = _silu(_layer_norm(acc + cb_ref[...], clnw_ref[...], clnb_ref[...]))
            yc_ref[0, rows, :] = yc.astype(yc_ref.dtype)

        if r < n_slabs - 1:
            conv_out(GRID_W)
        else:
            pl.when(jnp.logical_not(last))(functools.partial(conv_out, GRID_W))
            pl.when(last)(functools.partial(conv_out, CTX_LEN))


def _in_projection(xcat, mod, w_r, wg_t, sgu_w, sgu_b, sgu_ln_w, sgu_ln_b, conv_w, conv_b,
                   conv_ln_w, conv_ln_b):
    bsz, s, d = xcat.shape
    tm = TM_MAIN
    nc5 = s // S5_CHUNK
    cpt = tm // S5_CHUNK
    bias_full = jnp.repeat(sgu_b.T, SGU_GD, axis=1)
    row = lambda a: a.reshape(1, W_MIX)
    tok = lambda w: pl.BlockSpec((1, tm, w), lambda b, j: (b, j, 0))
    chan = lambda c: pl.BlockSpec((1, c, tm), lambda b, j: (b, 0, j))
    out_shape = (
        jax.ShapeDtypeStruct((bsz, s, W_MIX), BF16),
        jax.ShapeDtypeStruct((bsz, W_MIX, s), BF16),
        jax.ShapeDtypeStruct((bsz, W_MIX, s), BF16),
        jax.ShapeDtypeStruct((bsz, W_MIX, s), F32),
        jax.ShapeDtypeStruct((bsz, s, W_MIX), BF16),
        jax.ShapeDtypeStruct((bsz, s, W_MIX), BF16),
        jax.ShapeDtypeStruct((bsz, s, W_MIX), F32),
        jax.ShapeDtypeStruct((S5_GROUPS, nc5, bsz, S5_HALF, 2 * S5_CHUNK), BF16),
        jax.ShapeDtypeStruct((bsz, N_GATES, s), F32),
    )
    out_specs = (
        tok(W_MIX), chan(W_MIX), chan(W_MIX), chan(W_MIX), tok(W_MIX), tok(W_MIX), tok(W_MIX),
        pl.BlockSpec((S5_GROUPS, cpt, 1, S5_HALF, 2 * S5_CHUNK), lambda b, j: (0, j, b, 0, 0)),
        chan(N_GATES),
    )
    return pl.pallas_call(
        _inproj_kernel,
        grid=(bsz, s // tm),
        in_specs=[tok(d),
                  pl.BlockSpec((1, 6, d), lambda b, j: (b, 0, 0)),
                  pl.BlockSpec((1, 6, d), lambda b, j: (bsz, 0, 0)),
                  _const_spec(w_r.shape),
                  _const_spec(wg_t.shape),
                  _const_spec((SGU_GROUPS, SGU_CHUNK, SGU_CHUNK)),
                  _const_spec((SGU_CHUNK, W_MIX)),
                  _const_spec((1, W_MIX)), _const_spec((1, W_MIX)),
                  _const_spec((CONV_WIDTH, W_MIX)), _const_spec((1, W_MIX)),
                  _const_spec((1, W_MIX)), _const_spec((1, W_MIX))],
        out_specs=out_specs,
        out_shape=out_shape,
        scratch_shapes=[pltpu.VMEM((tm, d), BF16),
                        pltpu.VMEM((tm // TILE, 8, CONV_ROWS, W_MIX), F32)],
        compiler_params=_params("arbitrary", "arbitrary"),
        name="in_projection",
    )(xcat, mod, mod, w_r, wg_t, sgu_w.astype(BF16), bias_full, row(sgu_ln_w), row(sgu_ln_b),
      conv_w, row(conv_b), row(conv_ln_w), row(conv_ln_b))


def _mlstm_kernel(k_ref, qt_ref, vt_ref, ot_ref, gt_ref, gbt_ref, nw_ref, y_ref,
                  ht_ref, src_ref, lgt_ref, cumt_ref, pm_ref, c_ref, n_ref, m_ref):
    L = MLSTM_CHUNK
    n_chunks = k_ref.shape[1] // L
    row = lax.broadcasted_iota(jnp.int32, (L, L), 0)
    col = lax.broadcasted_iota(jnp.int32, (L, L), 1)
    tri = (col <= row).astype(BF16)
    lane_g = lax.broadcasted_iota(jnp.int32, (1, GATE_PAD), 1)
    sub_g = lax.broadcasted_iota(jnp.int32, (N_GATES, 1), 0)
    is_f_sub = (sub_g % 8) >= HEADS
    lane_w = lax.broadcasted_iota(jnp.int32, (1, W_MIX), 1) // HEAD_DIM
    sub_w = lax.broadcasted_iota(jnp.int32, (W_MIX, 1), 0) // HEAD_DIM
    bd_mask = sub_w == lane_w
    n_mask = sub_g == lane_w
    row4 = lax.broadcasted_iota(jnp.int32, (HEADS * L, L), 0) % L
    col4 = lax.broadcasted_iota(jnp.int32, (HEADS * L, L), 1)
    mask4 = (row4 <= col4, row4 >= col4)

    for c in range(n_chunks):
        cols = slice(c * L, (c + 1) * L)
        grawt = gt_ref[0, :, cols] + gbt_ref[...]
        lgt = jnp.where(is_f_sub, _log_sigmoid(grawt), grawt)
        lgt_ref[:, cols] = lgt
        hi, mid, lo = _split3(lgt)
        pret = _dot_nt(hi, tri) + _dot_nt(mid, tri) + _dot_nt(lo, tri)
        cumt = jnp.where(sub_g >= 8, pret[:, L - 1:L] - pret + lgt, pret)
        cumt_ref[:, cols] = cumt
        srct = jnp.where(is_f_sub, -jnp.inf, lgt - pltpu.roll(cumt, N_GATES - HEADS, 0))
        pm_ref[:, cols] = srct
        padded = jnp.concatenate([srct, jnp.zeros((GATE_PAD - N_GATES, L), F32)], axis=0)
        src_ref[cols, :] = padded.T
    pos = lax.broadcasted_iota(jnp.int32, (1, pm_ref.shape[1]), 1) % L
    pm = pm_ref[...]
    sh = 1
    while sh < L:
        fwd_v = jnp.where(pos >= sh, pltpu.roll(pm, sh, 1), -jnp.inf)
        bwd_v = jnp.where(pos < L - sh, pltpu.roll(pm, pm.shape[1] - sh, 1), -jnp.inf)
        pm = jnp.maximum(pm, jnp.where(sub_g >= 8, bwd_v, fwd_v))
        sh *= 2
    pm_ref[...] = pm

    def chunk(c, d):
        r0 = pl.multiple_of(c * L, L)
        k = k_ref[0, pl.ds(r0, L), :]
        qt = qt_ref[0, :, pl.ds(r0, L)]
        vt = vt_ref[0, :, pl.ds(r0, L)]
        src_cols = src_ref[pl.ds(r0, L), :]
        lgt = lgt_ref[:, pl.ds(r0, L)]
        cumt = cumt_ref[:, pl.ds(r0, L)]
        pm = pm_ref[:, pl.ds(r0, L)]
        tot = cumt[:, L - 1:L] if d == 0 else cumt[:, 0:1]
        m_vec = m_ref[d]
        c_old = c_ref[d]
        n_old = n_ref[d]
        qc = _dot(c_old.astype(BF16), qt)
        qn = _dot(n_old.astype(BF16), qt)
        e_slabs = []
        e_small = jnp.zeros((N_GATES, L), F32)
        decay_row = jnp.zeros((1, W_MIX), F32)
        m_new_vec = m_vec
        srcs, col_terms, khs, w_inters, floors = [], [], [], [], []
        for h in range(HEADS):
            ic = d * 8 + h
            fc = d * 8 + HEADS + h
            m_h = m_vec[:, fc:fc + 1]
            cct = cumt[fc:fc + 1, :]
            inter = cct + m_h
            m_t = jnp.maximum(inter, cct + pm[ic:ic + 1, :])
            w_inters.append(jnp.exp(inter - m_t))
            floors.append(jnp.exp(-m_t))
            col_terms.append(jnp.broadcast_to(cct - m_t, (L, L)))
            srcs.append(src_cols[:, ic:ic + 1])
            khs.append(jnp.where(lane_w == h, k, jnp.zeros_like(k)))
        arg = jnp.concatenate(srcs, axis=0) + jnp.concatenate(col_terms, axis=0)
        arg = jnp.where(mask4[d], arg, -jnp.inf)
        sc = _dot(jnp.concatenate(khs, axis=0), qt) * jnp.exp(arg)
        scb = sc.astype(BF16)
        for h in range(HEADS):
            ic = d * 8 + h
            fc = d * 8 + HEADS + h
            head_lanes = lane_w == h
            sl = slice(h * HEAD_DIM, (h + 1) * HEAD_DIM)
            hrows = slice(h * L, (h + 1) * L)
            m_h = m_vec[:, fc:fc + 1]
            cct = cumt[fc:fc + 1, :]
            den = jnp.sum(sc[hrows, :], axis=0, keepdims=True) + w_inters[h] * qn[h:h + 1, :]
            num = _dot(vt[sl, :], scb[hrows, :]) + w_inters[h] * qc[sl, :]
            ht_ref[d, sl, pl.ds(r0, L)] = num / jnp.maximum(jnp.abs(den), floors[h])
            tot_h = tot[fc:fc + 1, :]
            logu = tot_h - cct + lgt[ic:ic + 1, :]
            m_new = jnp.maximum(tot_h + m_h, jnp.max(logu, axis=1, keepdims=True))
            e_h = jnp.exp(logu - m_new)
            e_slabs.append(jnp.broadcast_to(e_h, (HEAD_DIM, L)))
            e_small = jnp.where(sub_g == h, e_h, e_small)
            decay_row = jnp.where(head_lanes, jnp.exp(tot_h + m_h - m_new), decay_row)
            m_new_vec = jnp.where(lane_g == fc, m_new, m_new_vec)
        vet = (vt.astype(F32) * jnp.concatenate(e_slabs, axis=0)).astype(BF16)
        c_ref[d] = decay_row * c_old + jnp.where(bd_mask, _dot(vet, k), 0.0)
        n_ref[d] = decay_row * n_old + jnp.where(n_mask, _dot(e_small.astype(BF16), k), 0.0)
        m_ref[d] = m_new_vec

    c_ref[...] = jnp.zeros_like(c_ref)
    n_ref[...] = jnp.zeros_like(n_ref)
    m_ref[...] = jnp.zeros_like(m_ref)
    n_lat = n_chunks - CTX_LEN // L
    for j in range(n_chunks - n_lat):
        chunk(n_lat + j, 0)
        chunk(n_chunks - 1 - j, 1)

    def scan_body(i, carry):
        chunk(i, 0)
        chunk(n_lat - 1 - i, 1)
        return carry

    lax.fori_loop(0, n_lat, scan_body, 0)

    def readout(c, carry):
        r0 = pl.multiple_of(c * L, L)
        slabs = []
        for h in range(HEADS):
            sl = slice(h * HEAD_DIM, (h + 1) * HEAD_DIM)
            hs = ht_ref[0, sl, pl.ds(r0, L)] + ht_ref[1, sl, pl.ds(r0, L)]
            mu = jnp.mean(hs, axis=0, keepdims=True)
            dl = hs - mu
            var = jnp.mean(dl * dl, axis=0, keepdims=True)
            hn = dl * lax.rsqrt(var + LN_EPS) * nw_ref[sl, :]
            slabs.append(hn * _sigmoid(ot_ref[0, sl, pl.ds(r0, L)]))
        y_ref[0, pl.ds(r0, L), :] = jnp.concatenate(slabs, axis=0).T.astype(y_ref.dtype)
        return carry

    lax.fori_loop(0, n_chunks, readout, 0)


def _mlstm(k, qt, vt, ot, gt, gate_bias, norm_w):
    bsz, s, _ = k.shape
    gbt = gate_bias.reshape(N_GATES, 1)
    nw_full = jnp.broadcast_to(norm_w.reshape(W_MIX, 1), (W_MIX, MLSTM_CHUNK))
    tok = lambda w: pl.BlockSpec((1, s, w), lambda b: (b, 0, 0))
    chan = lambda c: pl.BlockSpec((1, c, s), lambda b: (b, 0, 0))
    return pl.pallas_call(
        _mlstm_kernel,
        grid=(bsz,),
        in_specs=[tok(W_MIX), chan(W_MIX), chan(W_MIX), chan(W_MIX), chan(N_GATES),
                  pl.BlockSpec((N_GATES, 1), lambda b: (0, 0)),
                  pl.BlockSpec((W_MIX, MLSTM_CHUNK), lambda b: (0, 0))],
        out_specs=tok(W_MIX),
        out_shape=jax.ShapeDtypeStruct((bsz, s, W_MIX), BF16),
        scratch_shapes=[pltpu.VMEM((2, W_MIX, s), F32),
                        pltpu.VMEM((s, GATE_PAD), F32),
                        pltpu.VMEM((N_GATES, s), F32),
                        pltpu.VMEM((N_GATES, s), F32),
                        pltpu.VMEM((N_GATES, s), F32),
                        pltpu.VMEM((2, W_MIX, W_MIX), F32),
                        pltpu.VMEM((2, N_GATES, W_MIX), F32),
                        pltpu.VMEM((2, 1, GATE_PAD), F32)],
        compiler_params=_params("arbitrary"),
        name="mlstm",
    )(k, qt, vt, ot, gt, gbt, nw_full)


def _cpow(ar, ai, ld, k):
    dt = jnp.exp(ld)
    mag = jnp.exp(k * (ar * dt))
    ang = k * (ai * dt)
    return mag * jnp.cos(ang), mag * jnp.sin(ang)


def _zoh_coef(ar, ai, ld):
    er, ei = _cpow(ar, ai, ld, 1.0)
    den = ar * ar + ai * ai
    return ((er - 1.0) * ar + ei * ai) / den, (ei * ar - (er - 1.0) * ai) / den


def _dot_x3(a, b):
    a_hi, a_mid, _ = _split3(a)
    b_hi, b_mid, _ = _split3(b)
    return _dot(a_hi, b_hi) + _dot(a_hi, b_mid) + _dot(a_mid, b_hi)


def _tap_table(lag, col, cf):
    fwd = lag >= 0
    k = jnp.abs(lag).astype(F32)
    sel = lambda a, b: jnp.where(fwd, a, b)
    pre, pim = _cpow(sel(col(0), col(3)), sel(col(1), col(4)), sel(col(2), col(5)), k)
    cre, cim = sel(cf[0][0], cf[1][0]), sel(cf[0][1], cf[1][1])
    both = lag == 0
    return (pre * cre - pim * cim + jnp.where(both, cf[1][0], 0.0),
            pre * cim + pim * cre + jnp.where(both, cf[1][1], 0.0))


def _s5_ops_kernel(cols_ref, rows_ref, btre_ref, btim_ref, bdre_ref, bdim_ref, cre_ref, cim_ref,
                   ctre_ref, ctim_ref, kb_ref, wloc_ref, wout_ref, ar_ref,
                   mt_ref, pt_ref, pw_ref, v_ref):
    L, H, N = S5_CHUNK, S5_HALF, S5_N
    i = pl.program_id(1)
    lane = lax.broadcasted_iota(jnp.int32, (1, 2 * L), 1)
    lo = lane < L

    @pl.when(i == 0)
    def _():
        col = lambda k: cols_ref[0, :, k:k + 1]
        row = lambda k: rows_ref[0, k:k + 1, :]
        cf = [_zoh_coef(col(3 * d), col(3 * d + 1), col(3 * d + 2)) for d in range(2)]
        mt_ref[0], mt_ref[1] = _tap_table(jnp.where(lo, lane, lane - 2 * L), col, cf)
        mt_ref[2], mt_ref[3] = _tap_table(lane - L, col, cf)
        tt = lane % L
        ss = lax.broadcasted_iota(jnp.int32, (L, 1), 0)
        for d in range(2):
            kk = (tt + 1 if d == 0 else L - tt).astype(F32)
            pt_ref[d, 0], pt_ref[d, 1] = _cpow(col(3 * d), col(3 * d + 1), col(3 * d + 2), kk)
            ks = (L - 1 - ss if d == 0 else ss).astype(F32)
            qre, qim = _cpow(row(3 * d), row(3 * d + 1), row(3 * d + 2), ks)
            fre, fim = _zoh_coef(row(3 * d), row(3 * d + 1), row(3 * d + 2))
            pw_ref[d, 0] = qre * fre - qim * fim
            pw_ref[d, 1] = qre * fim + qim * fre
            ar_ref[0, 2 * d:2 * d + 1, :], ar_ref[0, 2 * d + 1:2 * d + 2, :] = _cpow(
                row(3 * d), row(3 * d + 1), row(3 * d + 2), float(L))
        ar_ref[0, 4:8, :] = jnp.zeros((4, 2 * N), F32)

    for half in range(2):
        cre = cre_ref[0, pl.ds(i + half * H, 1), :]
        cim = cim_ref[0, pl.ds(i + half * H, 1), :]
        g_re = cre * btre_ref[0] - cim * btim_ref[0]
        g_im = cre * btim_ref[0] + cim * btre_ref[0]
        v_ref[half] = _dot_x3(g_re, mt_ref[2 * half]) - _dot_x3(g_im, mt_ref[2 * half + 1])
    for q in range(S5_P):
        r0 = (q % H) * 2 * L + (q // H) * L
        toe = [pltpu.roll(jnp.broadcast_to(v_ref[half, q:q + 1, :], (L, 2 * L)), 0, 1,
                          stride=1, stride_axis=0) for half in range(2)]
        kb_ref[0, r0:r0 + L, :] = jnp.where(lo, toe[0], toe[1]).astype(BF16)

    lane_p = lax.broadcasted_iota(jnp.int32, (1, S5_P), 1)
    pick = lambda ref, p: jnp.sum(jnp.where(lane_p == p, ref[0], 0.0), axis=1, keepdims=True)
    ccre = jnp.where(lo, pick(ctre_ref, i), pick(ctre_ref, i + H))
    ccim = jnp.where(lo, pick(ctim_ref, i), pick(ctim_ref, i + H))
    pad = jnp.zeros((N, 2 * L), BF16)
    for d in range(2):
        pre, pim = pt_ref[d, 0], pt_ref[d, 1]
        base = d * 4 * N
        wout_ref[0, base:base + N, :] = (ccre * pre - ccim * pim).astype(BF16)
        wout_ref[0, base + N:base + 2 * N, :] = pad
        wout_ref[0, base + 2 * N:base + 3 * N, :] = (-(ccre * pim + ccim * pre)).astype(BF16)
        wout_ref[0, base + 3 * N:base + 4 * N, :] = pad
    for half in range(2):
        bre = bdre_ref[0, pl.ds(i + half * H, 1), :]
        bim = bdim_ref[0, pl.ds(i + half * H, 1), :]
        rows = slice(half * L, (half + 1) * L)
        for d in range(2):
            base = d * 4 * N
            re = pw_ref[d, 0] * bre - pw_ref[d, 1] * bim
            im = pw_ref[d, 0] * bim + pw_ref[d, 1] * bre
            wloc_ref[0, rows, base:base + 2 * N] = jnp.where(lo, re, 0.0).astype(BF16)
            wloc_ref[0, rows, base + 2 * N:base + 4 * N] = jnp.where(lo, im, 0.0).astype(BF16)


def _s5_operators(a_re, a_im, log_dt, b_re, b_im, c_re, c_im):
    L, H, n2 = S5_CHUNK, S5_HALF, 2 * S5_N
    g = S5_GROUPS
    ld = jnp.broadcast_to(log_dt[:, :, None], a_re.shape)
    per_dir = [a_re[0], a_im[0], ld[0], a_re[1], a_im[1], ld[1]]
    zero = jnp.zeros_like(a_re[0])
    cols = jnp.stack(per_dir + [zero, zero], axis=-1)
    dup = lambda v: jnp.concatenate([v, v], axis=-1)
    rows = jnp.stack([dup(v) for v in per_dir + [zero, zero]], axis=1)
    bt_re, bt_im = b_re.transpose(0, 2, 1), b_im.transpose(0, 2, 1)
    ct_re, ct_im = c_re.transpose(0, 2, 1), c_im.transpose(0, 2, 1)
    per_g = lambda a: pl.BlockSpec((1,) + a.shape[1:], lambda gi, i: (gi, 0, 0))
    ins = (cols, rows, bt_re, bt_im, dup(bt_re), dup(bt_im), c_re, c_im, ct_re, ct_im)
    return pl.pallas_call(
        _s5_ops_kernel,
        grid=(g, H),
        in_specs=[per_g(a) for a in ins],
        out_specs=(pl.BlockSpec((1, S5_FLAT, 2 * L), lambda gi, i: (gi, 0, i)),
                   pl.BlockSpec((1, 2 * L, S5_STATE), lambda gi, i: (gi, i, 0)),
                   pl.BlockSpec((1, S5_STATE, 2 * L), lambda gi, i: (gi, 0, i)),
                   pl.BlockSpec((1, 8, n2), lambda gi, i: (gi, 0, 0))),
        out_shape=(jax.ShapeDtypeStruct((g, S5_FLAT, S5_FLAT), BF16),
                   jax.ShapeDtypeStruct((g, S5_FLAT, S5_STATE), BF16),
                   jax.ShapeDtypeStruct((g, S5_STATE, S5_FLAT), BF16),
                   jax.ShapeDtypeStruct((g, 8, n2), F32)),
        scratch_shapes=[pltpu.VMEM((4, S5_N, 2 * L), F32),
                        pltpu.VMEM((2, 2, S5_N, 2 * L), F32),
                        pltpu.VMEM((2, 2, L, n2), F32),
                        pltpu.VMEM((2, S5_P, 2 * L), F32)],
        compiler_params=_params("arbitrary", "arbitrary"),
        name="s5_operators",
    )(*ins)


def _s5_kernel(u_ref, kb_ref, wloc_ref, wout_ref, ar_ref, y_ref, sin_ref, *, bsz, n_ctx_chunks):
    n_chunks = u_ref.shape[1] // bsz
    n2 = 2 * S5_N
    u = u_ref[0]
    sloc = _dot(u, wloc_ref[0])
    ar = ar_ref[0]
    n_lat = n_chunks - n_ctx_chunks
    fwd_order = list(range(n_lat, n_chunks)) + list(range(n_lat))
    bwd_order = list(range(n_chunks - 1, n_lat - 1, -1)) + list(range(n_lat - 1, -1, -1))
    for d, order in ((0, fwd_order), (1, bwd_order)):
        a_re = ar[2 * d:2 * d + 1, :]
        a_im = ar[2 * d + 1:2 * d + 2, :]
        re_cols = slice(2 * d * n2, (2 * d + 1) * n2)
        im_cols = slice((2 * d + 1) * n2, (2 * d + 2) * n2)
        st_re = jnp.zeros((bsz, n2), F32)
        st_im = jnp.zeros((bsz, n2), F32)
        for c in order:
            rows = slice(c * bsz, (c + 1) * bsz)
            sin_ref[rows, re_cols] = st_re.astype(BF16)
            sin_ref[rows, im_cols] = st_im.astype(BF16)
            st_re, st_im = (a_re * st_re - a_im * st_im + sloc[rows, re_cols],
                            a_re * st_im + a_im * st_re + sloc[rows, im_cols])
    y_ref[0] = _dot(u, kb_ref[0]) + _dot(sin_ref[...], wout_ref[0])


def _s5_mix(ut, kbig, w_loc, w_out, a_rows, bsz):
    g, rows, flat = ut.shape
    kern = functools.partial(_s5_kernel, bsz=bsz, n_ctx_chunks=CTX_LEN // S5_CHUNK)
    per_g = lambda a: pl.BlockSpec((1,) + a.shape[1:], lambda gi: (gi, 0, 0))
    return pl.pallas_call(
        kern,
        grid=(g,),
        in_specs=[per_g(a) for a in (ut, kbig, w_loc, w_out, a_rows)],
        out_specs=pl.BlockSpec((1, rows, flat), lambda gi: (gi, 0, 0)),
        out_shape=jax.ShapeDtypeStruct((g, rows, flat), F32),
        scratch_shapes=[pltpu.VMEM((rows, S5_STATE), BF16)],
        compiler_params=_params("arbitrary"),
        name="s5_mix",
    )(ut, kbig, w_loc, w_out, a_rows)


def _s5_readout(y5_ref, zd_ref, d_ref, gw_ref, gb_ref, yd_ref):
    lo = lax.broadcasted_iota(jnp.int32, (1, 2 * S5_CHUNK), 1) < S5_CHUNK
    half_rows = (S5_GROUPS, 1, S5_HALF, 2 * S5_CHUNK)
    for j in range(zd_ref.shape[1] // (2 * S5_CHUNK)):
        rows = slice(j * 2 * S5_CHUNK, (j + 1) * 2 * S5_CHUNK)
        a = y5_ref[:, 2 * j, 0, :, :].reshape(S5_GROUPS * S5_HALF, 2 * S5_CHUNK)
        b = y5_ref[:, 2 * j + 1, 0, :, :].reshape(S5_GROUPS * S5_HALF, 2 * S5_CHUNK)
        first = jnp.where(lo, a, pltpu.roll(b, S5_CHUNK, 1))
        second = jnp.where(lo, pltpu.roll(a, S5_CHUNK, 1), b)
        yt = jnp.concatenate([first.reshape(half_rows), second.reshape(half_rows)],
                             axis=1).reshape(W_MIX, 2 * S5_CHUNK)
        y = _gelu(yt.T + d_ref[...] * zd_ref[0, rows, :])
        gate = _sigmoid(_dot(y.astype(BF16), gw_ref[...]) + gb_ref[...])
        yd_ref[rows, :] = (y * gate).astype(yd_ref.dtype)


def _out_ffn_kernel(x_ref, ya_ref, yb_ref, yc_ref, y5_ref, zd_ref, modb_ref, modc_ref, d5_ref,
                    gw_ref, gb_ref, wo_ref, l1w_ref, l1b_ref, wi_ref, wd_ref, l2w_ref, l2b_ref,
                    o_ref, yd_ref, x1_ref, h2_ref, a_ref, *, ctx_rows):
    tm = x_ref.shape[1]
    n_lat = tm - ctx_rows
    segs = [(0, n_lat, 0)] + ([(n_lat, tm, 1)] if ctx_rows else [])
    _s5_readout(y5_ref, zd_ref, d5_ref, gw_ref, gb_ref, yd_ref)
    y = (_dot(ya_ref[0], wo_ref[0:W_MIX, :]) + _dot(yb_ref[0], wo_ref[W_MIX:2 * W_MIX, :]) +
         _dot(yc_ref[0], wo_ref[2 * W_MIX:3 * W_MIX, :]) + _dot(yd_ref[...], wo_ref[3 * W_MIX:4 * W_MIX, :]))
    for r0, r1, which in segs:
        g1 = _tile_mod(modb_ref, modc_ref, 2)[which]
        sh2 = _tile_mod(modb_ref, modc_ref, 3)[which]
        sc2 = _tile_mod(modb_ref, modc_ref, 4)[which]
        x1 = _layer_norm(ALPHA * x_ref[0, r0:r1, :] + g1 * y[r0:r1, :], l1w_ref[...], l1b_ref[...])
        x1_ref[r0:r1, :] = x1
        h2_ref[r0:r1, :] = (x1 * (1.0 + sc2) + sh2).astype(BF16)
    h2 = h2_ref[...]
    for kk in range(D_FF // FF_CHUNK):
        cols = slice(kk * FF_CHUNK, (kk + 1) * FF_CHUNK)
        gk = _dot(h2, wi_ref[:, cols])
        uk = _dot(h2, wi_ref[:, D_FF + kk * FF_CHUNK:D_FF + (kk + 1) * FF_CHUNK])
        a_ref[:, cols] = (_silu(gk) * uk).astype(BF16)
    f = _dot(a_ref[...], wd_ref[...])
    for r0, r1, which in segs:
        g2 = _tile_mod(modb_ref, modc_ref, 5)[which]
        o_ref[0, r0:r1, :] = _layer_norm(ALPHA * x1_ref[r0:r1, :] + g2 * f[r0:r1, :],
                                         l2w_ref[...], l2b_ref[...])


def _out_ffn(xcat, ya, yb, yc, y5, zd, mod, s5_d, glu_w, glu_b, w_out, ln1_w, ln1_b, w_ffn_in,
             w_ffn_out, ln2_w, ln2_b, tm, n_out):
    bsz, s, d = xcat.shape
    ctx_rows = CTX_LEN if n_out == s else 0
    tok = lambda w: pl.BlockSpec((1, tm, w), lambda b, j: (b, j, 0))
    row = lambda a: a.reshape(1, -1)
    return pl.pallas_call(
        functools.partial(_out_ffn_kernel, ctx_rows=ctx_rows),
        grid=(bsz, n_out // tm),
        in_specs=[tok(d), tok(W_MIX), tok(W_MIX), tok(W_MIX),
                  pl.BlockSpec((S5_GROUPS, tm // S5_CHUNK, 1, S5_HALF, 2 * S5_CHUNK),
                               lambda b, j: (0, j, b, 0, 0)),
                  tok(W_MIX),
                  pl.BlockSpec((1, 6, d), lambda b, j: (b, 0, 0)),
                  pl.BlockSpec((1, 6, d), lambda b, j: (bsz, 0, 0)),
                  _const_spec((1, W_MIX)), _const_spec((W_MIX, W_MIX)), _const_spec((1, W_MIX)),
                  _const_spec(w_out.shape), _const_spec((1, d)), _const_spec((1, d)),
                  _const_spec(w_ffn_in.shape), _const_spec(w_ffn_out.shape),
                  _const_spec((1, d)), _const_spec((1, d))],
        out_specs=tok(d),
        out_shape=jax.ShapeDtypeStruct((bsz, n_out, d), F32),
        scratch_shapes=[pltpu.VMEM((tm, W_MIX), BF16), pltpu.VMEM((tm, d), F32),
                        pltpu.VMEM((tm, d), BF16), pltpu.VMEM((tm, D_FF), BF16)],
        compiler_params=_params("arbitrary", "arbitrary"),
        name="out_ffn",
    )(xcat, ya, yb, yc, y5, zd, mod, mod, row(s5_d), glu_w.astype(BF16), row(glu_b), w_out,
      row(ln1_w), row(ln1_b), w_ffn_in, w_ffn_out, row(ln2_w), row(ln2_b))


_S5_LOCAL = np.arange(S5_P).reshape(S5_HALF, 2).T.reshape(-1)
_S5_ORDER = (np.arange(S5_GROUPS)[:, None] * S5_P + _S5_LOCAL[None, :]).reshape(-1)


def _reorder_w_in(w):
    off_b = 4 * W_MIX + N_GATES
    off_d = w.shape[1] - W_MIX
    w_r = jnp.concatenate([w[:, :4 * W_MIX], w[:, off_b:off_d], w[:, off_d:][:, _S5_ORDER]], axis=1)
    return w_r.astype(BF16), w[:, 4 * W_MIX:off_b].T.astype(BF16)


def kernel(x, c, ctx, c_ctx, w_mod, b_mod, w_in, mlstm_gate_bias, mlstm_norm_w, sgu_ln_w, sgu_ln_b, sgu_w, sgu_b, conv_w, conv_b, conv_ln_w, conv_ln_b, s5_a_re, s5_a_im, s5_log_dt, s5_b_re, s5_b_im, s5_c_re, s5_c_im, s5_d, s5_glu_w, s5_glu_b, w_out, ln1_w, ln1_b, w_ffn_in, w_ffn_out, ln2_w, ln2_b):
    bsz, seq, d = x.shape
    s = seq + CTX_LEN
    xcat = jnp.concatenate([x, ctx], axis=1)
    mod_rows = -(-(bsz + 1) // 8) * 8
    cvec = jnp.zeros((mod_rows, d), F32).at[:bsz].set(c).at[bsz].set(c_ctx)
    mod_all = _modulation(cvec, w_mod, b_mod).reshape(DEPTH, mod_rows, 6, d)
    nc5 = s // S5_CHUNK
    for l in range(DEPTH):
        mod = mod_all[l]
        w_r, wg_t = _reorder_w_in(w_in[l])
        k, qt, vt, ot, yb, yc, zd, ut, gt = _in_projection(
            xcat, mod, w_r, wg_t, sgu_w[l], sgu_b[l], sgu_ln_w[l], sgu_ln_b[l],
            conv_w[l], conv_b[l], conv_ln_w[l], conv_ln_b[l])
        ya = _mlstm(k, qt, vt, ot, gt, mlstm_gate_bias[l], mlstm_norm_w[l])
        kbig, w_loc, w_so, a_rows = _s5_operators(
            s5_a_re[l], s5_a_im[l], s5_log_dt[l], s5_b_re[l][:, :, _S5_LOCAL], s5_b_im[l][:, :, _S5_LOCAL],
            s5_c_re[l][:, _S5_LOCAL, :], s5_c_im[l][:, _S5_LOCAL, :])
        y5 = _s5_mix(ut.reshape(S5_GROUPS, nc5 * bsz, S5_FLAT), kbig, w_loc, w_so, a_rows, bsz)
        y5 = y5.reshape(S5_GROUPS, nc5, bsz, S5_HALF, 2 * S5_CHUNK)
        last = l == DEPTH - 1
        w_out_l = jnp.concatenate([w_out[l][:3 * W_MIX], w_out[l][3 * W_MIX:][_S5_ORDER]], axis=0)
        xcat = _out_ffn(xcat, ya, yb, yc, y5, zd, mod, s5_d[l][_S5_ORDER],
                        s5_glu_w[l][_S5_ORDER][:, _S5_ORDER], s5_glu_b[l][_S5_ORDER],
                        w_out_l.astype(BF16), ln1_w[l], ln1_b[l],
                        w_ffn_in[l].astype(BF16), w_ffn_out[l].astype(BF16), ln2_w[l], ln2_b[l],
                        tm=TM_LATENT if last else TM_MAIN, n_out=seq if last else s)
    return xcat
```

```python
import functools

import numpy as np
import jax
import jax.numpy as jnp
from jax import lax
from jax.experimental import pallas as pl
from jax.experimental.pallas import tpu as pltpu

F32 = jnp.float32
BF16 = jnp.bfloat16

D_MODEL = 1024
DEPTH = 2
CTX_LEN = 256
SEQ_LATENT = 2048
GRID_W = 64
W_MIX = 256
HEADS = 4
HEAD_DIM = W_MIX // HEADS
SGU_GROUPS = 4
SGU_GD = W_MIX // SGU_GROUPS
SGU_CHUNK = 128
CONV_WIDTH = 31
CONV_PAD = CONV_WIDTH // 2
CONV_HALO = 32
S5_P = 16
S5_GROUPS = W_MIX // S5_P
S5_N = 64
S5_CHUNK = 64
S5_HALF = S5_P // 2
S5_FLAT = S5_P * S5_CHUNK
S5_STATE = 8 * S5_N
D_FF = 2816
FF_CHUNK = 256
ALPHA = (2 * DEPTH) ** 0.25
LN_EPS = 1e-5
TILE = 256
MLSTM_CHUNK = 256
TM_MAIN = 3 * TILE
TM_LATENT = 2 * TILE
CONV_ROWS = max((TILE // GRID_W) * (GRID_W + CONV_HALO), CTX_LEN + CONV_HALO)
N_GATES = 4 * HEADS
GATE_PAD = 128
C_O = 3 * W_MIX
C_B = 4 * W_MIX
C_C = C_B + 2 * W_MIX
C_D = C_C + 2 * W_MIX
C_G = C_D + W_MIX
VMEM_LIMIT = 56 * 1024 * 1024


def _dot(a, b):
    return jnp.dot(a, b, preferred_element_type=F32)


def _dot_nt(a, b):
    return lax.dot_general(a, b, (((1,), (1,)), ((), ())), preferred_element_type=F32)


def _split3(x):
    hi = x.astype(BF16)
    r = x - hi.astype(F32)
    mid = r.astype(BF16)
    lo = (r - mid.astype(F32)).astype(BF16)
    return hi, mid, lo


def _dot3_r(w_bf16, x):
    hi, mid, lo = _split3(x)
    return _dot(w_bf16, hi) + _dot(w_bf16, mid) + _dot(w_bf16, lo)


def _layer_norm(x, w, b):
    mu = jnp.mean(x, axis=-1, keepdims=True)
    d = x - mu
    var = jnp.mean(d * d, axis=-1, keepdims=True)
    return d * lax.rsqrt(var + LN_EPS) * w + b


def _sigmoid(x):
    return jax.nn.sigmoid(x)


def _silu(x):
    return x * jax.nn.sigmoid(x)


def _gelu(x):
    return jax.nn.gelu(x, approximate=True)


def _log_sigmoid(x):
    return jnp.minimum(x, 0.0) - jnp.log1p(jnp.exp(-jnp.abs(x)))


def _params(*sem):
    return pltpu.CompilerParams(dimension_semantics=sem, vmem_limit_bytes=VMEM_LIMIT)


def _const_spec(shape):
    nd = len(shape)
    return pl.BlockSpec(shape, lambda *_: (0,) * nd, pipeline_mode=pl.Buffered(1))


def _mod_kernel(c_ref, w_ref, b_ref, o_ref):
    s = _silu(c_ref[...])
    s_hi = s.astype(BF16)
    s_lo = (s - s_hi.astype(F32)).astype(BF16)
    w = w_ref[0]
    w_hi = w.astype(BF16)
    w_lo = (w - w_hi.astype(F32)).astype(BF16)
    o_ref[0] = _dot(s_hi, w_hi) + _dot(s_hi, w_lo) + _dot(s_lo, w_hi) + b_ref[0]


def _modulation(cvec, w_mod, b_mod):
    depth, d, n = w_mod.shape
    rows = cvec.shape[0]
    tn = 1024
    return pl.pallas_call(
        _mod_kernel,
        grid=(depth, n // tn),
        in_specs=[pl.BlockSpec((rows, d), lambda l, j: (0, 0)),
                  pl.BlockSpec((1, d, tn), lambda l, j: (l, 0, j)),
                  pl.BlockSpec((1, 1, tn), lambda l, j: (l, 0, j))],
        out_specs=pl.BlockSpec((1, rows, tn), lambda l, j: (l, 0, j)),
        out_shape=jax.ShapeDtypeStruct((depth, rows, n), F32),
        compiler_params=_params("arbitrary", "arbitrary"),
        name="modulation",
    )(cvec, w_mod, b_mod.reshape(depth, 1, n))


def _tile_mod(modb_ref, modc_ref, i):
    last = pl.program_id(1) == pl.num_programs(1) - 1
    row_b = modb_ref[0, i:i + 1, :]
    return row_b, jnp.where(last, modc_ref[0, i:i + 1, :], row_b)


def _sgu_slab(zb, sgw_ref, sgb_ref, slnw_ref, slnb_ref):
    u = _gelu(zb[:, 0:W_MIX])
    vn = _layer_norm(_gelu(zb[:, W_MIX:2 * W_MIX]), slnw_ref[...], slnb_ref[...]).astype(BF16)
    lane_grp = lax.broadcasted_iota(jnp.int32, (1, W_MIX), 1) // SGU_GD
    out = []
    for cl in range(TILE // SGU_CHUNK):
        vc = vn[cl * SGU_CHUNK:(cl + 1) * SGU_CHUNK, :]
        acc = sgb_ref[...]
        for gi in range(SGU_GROUPS):
            acc = acc + _dot(sgw_ref[gi], jnp.where(lane_grp == gi, vc, jnp.zeros_like(vc)))
        out.append(acc)
    return u * jnp.concatenate(out, axis=0)


def _conv_slab(a, seg_len, cw_ref, sh_ref):
    n_seg = TILE // seg_len
    stride = seg_len + CONV_HALO
    zeros16 = jnp.zeros((16, W_MIX), F32)
    for sg in range(n_seg):
        base = sg * stride
        seg = a[sg * seg_len:(sg + 1) * seg_len, :]
        for r in range(8):
            sh_ref[r, pl.ds(base, 16), :] = zeros16
            sh_ref[r, pl.ds(base + seg_len + 8, 16), :] = zeros16
            sh_ref[r, pl.ds(base + 16 - r, seg_len), :] = seg
    out = []
    for sg in range(n_seg):
        base = sg * stride
        acc = jnp.zeros((seg_len, W_MIX), F32)
        for t in range(CONV_WIDTH):
            off = t - CONV_PAD
            r = off % 8
            acc = acc + cw_ref[t:t + 1, :] * sh_ref[r, pl.ds(base + (off - r) + 16, seg_len), :]
        out.append(acc)
    return out[0] if n_seg == 1 else jnp.concatenate(out, axis=0)


def _emit_s5_rows(zdt, ut_ref, chunk0):
    n_tok = zdt.shape[1]
    lo = lax.broadcasted_iota(jnp.int32, (1, 2 * S5_CHUNK), 1) < S5_CHUNK
    halves = zdt.reshape(S5_GROUPS, 2, S5_HALF, n_tok)
    first = halves[:, 0].reshape(S5_GROUPS * S5_HALF, n_tok)
    second = halves[:, 1].reshape(S5_GROUPS * S5_HALF, n_tok)
    for j in range(n_tok // (2 * S5_CHUNK)):
        a = first[:, j * 2 * S5_CHUNK:(j + 1) * 2 * S5_CHUNK]
        b = second[:, j * 2 * S5_CHUNK:(j + 1) * 2 * S5_CHUNK]
        pieces = (jnp.where(lo, a, pltpu.roll(b, S5_CHUNK, 1)),
                  jnp.where(lo, pltpu.roll(a, S5_CHUNK, 1), b))
        for cc, piece in enumerate(pieces):
            ut_ref[:, chunk0 + 2 * j + cc, 0, :, :] = (
                piece.reshape(S5_GROUPS, S5_HALF, 2 * S5_CHUNK).astype(BF16))


def _inproj_kernel(*refs, n_slabs, emit_stream):
    x_refs = refs[:n_slabs]
    (ctx_ref, modb_ref, modc_ref, w_ref, wgt_ref, sgw_ref, sgb_ref, slnw_ref, slnb_ref,
     cw_ref, cb_ref, clnw_ref, clnb_ref) = refs[n_slabs:n_slabs + 13]
    outs = refs[n_slabs + 13:]
    if emit_stream:
        xo_ref, outs = outs[0], outs[1:]
    k_ref, qt_ref, vt_ref, ot_ref, yb_ref, yc_ref, zd_ref, ut_ref, gt_ref, h_ref, sh_ref = outs
    last = pl.program_id(1) == pl.num_programs(1) - 1
    sh_b, sh_c = _tile_mod(modb_ref, modc_ref, 0)
    sc_b, sc_c = _tile_mod(modb_ref, modc_ref, 1)
    for r in range(n_slabs):
        rows = slice(r * TILE, (r + 1) * TILE)
        x = x_refs[r][0]
        sc, sh = sc_b, sh_b
        if r == n_slabs - 1:
            x = jnp.where(last, ctx_ref[0], x)
            sc, sh = sc_c, sh_c
        if emit_stream:
            xo_ref[0, rows, :] = x
        h_ref[rows, :] = (x * (1.0 + sc) + sh).astype(BF16)
    for r in range(n_slabs):
        rows = slice(r * TILE, (r + 1) * TILE)
        h = h_ref[rows, :]
        qt_ref[0, :, rows] = _dot(h, w_ref[:, 0:W_MIX]).T.astype(BF16)
        k_ref[0, rows, :] = (_dot(h, w_ref[:, W_MIX:2 * W_MIX]) * (HEAD_DIM ** -0.5)).astype(BF16)
        vt_ref[0, :, rows] = _dot(h, w_ref[:, 2 * W_MIX:3 * W_MIX]).T.astype(BF16)
        ot_ref[0, :, rows] = _dot(h, w_ref[:, C_O:C_B]).T
        zd = _dot(h, w_ref[:, C_D:C_G])
        zd_ref[0, rows, :] = zd
        _emit_s5_rows(zd.T, ut_ref, r * (TILE // S5_CHUNK))
        gt_ref[0, :, rows] = _dot_nt(wgt_ref[...], h)
        yb = _sgu_slab(_dot(h, w_ref[:, C_B:C_C]), sgw_ref, sgb_ref, slnw_ref, slnb_ref)
        yb_ref[0, rows, :] = yb.astype(yb_ref.dtype)
        zc = _dot(h, w_ref[:, C_C:C_D])
        a = zc[:, 0:W_MIX] * _sigmoid(zc[:, W_MIX:2 * W_MIX])

        def conv_out(seg_len, a=a, rows=rows, r=r):
            acc = _conv_slab(a, seg_len, cw_ref, sh_ref.at[r])
            yc = _silu(_layer_norm(acc + cb_ref[...], clnw_ref[...], clnb_ref[...]))
            yc_ref[0, rows, :] = yc.astype(yc_ref.dtype)

        if r < n_slabs - 1:
            conv_out(GRID_W)
        else:
            pl.when(jnp.logical_not(last))(functools.partial(conv_out, GRID_W))
            pl.when(last)(functools.partial(conv_out, CTX_LEN))


def _in_projection(lat, ctx, ctx_block, emit_stream, mod, w_r, wg_t, sgu_w, sgu_b, sgu_ln_w, sgu_ln_b,
                   conv_w, conv_b, conv_ln_w, conv_ln_b):
    bsz, _, d = lat.shape
    s = SEQ_LATENT + CTX_LEN
    tm = TM_MAIN
    n_slabs = tm // TILE
    last_lat = SEQ_LATENT // TILE - 1
    slab = lambda r: pl.BlockSpec((1, TILE, d), lambda b, j: (b, jnp.minimum(n_slabs * j + r, last_lat), 0))
    nc5 = s // S5_CHUNK
    cpt = tm // S5_CHUNK
    bias_full = jnp.repeat(sgu_b.T, SGU_GD, axis=1)
    row = lambda a: a.reshape(1, W_MIX)
    tok = lambda w: pl.BlockSpec((1, tm, w), lambda b, j: (b, j, 0))
    chan = lambda c: pl.BlockSpec((1, c, tm), lambda b, j: (b, 0, j))
    out_shape = (
        jax.ShapeDtypeStruct((bsz, s, W_MIX), BF16),
        jax.ShapeDtypeStruct((bsz, W_MIX, s), BF16),
        jax.ShapeDtypeStruct((bsz, W_MIX, s), BF16),
        jax.ShapeDtypeStruct((bsz, W_MIX, s), F32),
        jax.ShapeDtypeStruct((bsz, s, W_MIX), BF16),
        jax.ShapeDtypeStruct((bsz, s, W_MIX), BF16),
        jax.ShapeDtypeStruct((bsz, s, W_MIX), F32),
        jax.ShapeDtypeStruct((S5_GROUPS, nc5, bsz, S5_HALF, 2 * S5_CHUNK), BF16),
        jax.ShapeDtypeStruct((bsz, N_GATES, s), F32),
    )
    out_specs = (
        tok(W_MIX), chan(W_MIX), chan(W_MIX), chan(W_MIX), tok(W_MIX), tok(W_MIX), tok(W_MIX),
        pl.BlockSpec((S5_GROUPS, cpt, 1, S5_HALF, 2 * S5_CHUNK), lambda b, j: (0, j, b, 0, 0)),
        chan(N_GATES),
    )
    if emit_stream:
        out_shape = (jax.ShapeDtypeStruct((bsz, s, d), F32),) + out_shape
        out_specs = (tok(d),) + out_specs
    return pl.pallas_call(
        functools.partial(_inproj_kernel, n_slabs=n_slabs, emit_stream=emit_stream),
        grid=(bsz, s // tm),
        in_specs=[slab(r) for r in range(n_slabs)] + [
                  pl.BlockSpec((1, TILE, d), lambda b, j: (b, ctx_block, 0)),
                  pl.BlockSpec((1, 6, d), lambda b, j: (b, 0, 0)),
                  pl.BlockSpec((1, 6, d), lambda b, j: (bsz, 0, 0)),
                  _const_spec(w_r.shape),
                  _const_spec(wg_t.shape),
                  _const_spec((SGU_GROUPS, SGU_CHUNK, SGU_CHUNK)),
                  _const_spec((SGU_CHUNK, W_MIX)),
                  _const_spec((1, W_MIX)), _const_spec((1, W_MIX)),
                  _const_spec((CONV_WIDTH, W_MIX)), _const_spec((1, W_MIX)),
                  _const_spec((1, W_MIX)), _const_spec((1, W_MIX))],
        out_specs=out_specs,
        out_shape=out_shape,
        scratch_shapes=[pltpu.VMEM((tm, d), BF16),
                        pltpu.VMEM((tm // TILE, 8, CONV_ROWS, W_MIX), F32)],
        compiler_params=_params("arbitrary", "arbitrary"),
        name="in_projection",
    )(*([lat] * n_slabs), ctx, mod, mod, w_r, wg_t, sgu_w.astype(BF16), bias_full, row(sgu_ln_w),
      row(sgu_ln_b), conv_w, row(conv_b), row(conv_ln_w), row(conv_ln_b))


def _mlstm_kernel(k_ref, qt_ref, vt_ref, ot_ref, gt_ref, gbt_ref, nw_ref, y_ref,
                  ht_ref, src_ref, lgt_ref, cumt_ref, pm_ref, c_ref, n_ref, m_ref):
    L = MLSTM_CHUNK
    n_chunks = k_ref.shape[1] // L
    row = lax.broadcasted_iota(jnp.int32, (L, L), 0)
    col = lax.broadcasted_iota(jnp.int32, (L, L), 1)
    tri = (col <= row).astype(BF16)
    lane_g = lax.broadcasted_iota(jnp.int32, (1, GATE_PAD), 1)
    sub_g = lax.broadcasted_iota(jnp.int32, (N_GATES, 1), 0)
    is_f_sub = (sub_g % 8) >= HEADS
    lane_w = lax.broadcasted_iota(jnp.int32, (1, W_MIX), 1) // HEAD_DIM
    sub_w = lax.broadcasted_iota(jnp.int32, (W_MIX, 1), 0) // HEAD_DIM
    bd_mask = sub_w == lane_w
    n_mask = sub_g == lane_w
    row4 = lax.broadcasted_iota(jnp.int32, (HEADS * L, L), 0) % L
    col4 = lax.broadcasted_iota(jnp.int32, (HEADS * L, L), 1)
    mask4 = (row4 <= col4, row4 >= col4)

    for c in range(n_chunks):
        cols = slice(c * L, (c + 1) * L)
        grawt = gt_ref[0, :, cols] + gbt_ref[...]
        lgt = jnp.where(is_f_sub, _log_sigmoid(grawt), grawt)
        lgt_ref[:, cols] = lgt
        hi, mid, lo = _split3(lgt)
        pret = _dot_nt(hi, tri) + _dot_nt(mid, tri) + _dot_nt(lo, tri)
        cumt = jnp.where(sub_g >= 8, pret[:, L - 1:L] - pret + lgt, pret)
        cumt_ref[:, cols] = cumt
        srct = jnp.where(is_f_sub, -jnp.inf, lgt - pltpu.roll(cumt, N_GATES - HEADS, 0))
        pm_ref[:, cols] = srct
        padded = jnp.concatenate([srct, jnp.zeros((GATE_PAD - N_GATES, L), F32)], axis=0)
        src_ref[cols, :] = padded.T
    pos = lax.broadcasted_iota(jnp.int32, (1, pm_ref.shape[1]), 1) % L
    pm = pm_ref[...]
    sh = 1
    while sh < L:
        fwd_v = jnp.where(pos >= sh, pltpu.roll(pm, sh, 1), -jnp.inf)
        bwd_v = jnp.where(pos < L - sh, pltpu.roll(pm, pm.shape[1] - sh, 1), -jnp.inf)
        pm = jnp.maximum(pm, jnp.where(sub_g >= 8, bwd_v, fwd_v))
        sh *= 2
    pm_ref[...] = pm

    def chunk(c, d):
        r0 = pl.multiple_of(c * L, L)
        k = k_ref[0, pl.ds(r0, L), :]
        qt = qt_ref[0, :, pl.ds(r0, L)]
        vt = vt_ref[0, :, pl.ds(r0, L)]
        src_cols = src_ref[pl.ds(r0, L), :]
        lgt = lgt_ref[:, pl.ds(r0, L)]
        cumt = cumt_ref[:, pl.ds(r0, L)]
        pm = pm_ref[:, pl.ds(r0, L)]
        tot = cumt[:, L - 1:L] if d == 0 else cumt[:, 0:1]
        m_vec = m_ref[d]
        c_old = c_ref[d]
        n_old = n_ref[d]
        qc = _dot(c_old.astype(BF16), qt)
        qn = _dot(n_old.astype(BF16), qt)
        e_slabs = []
        e_small = jnp.zeros((N_GATES, L), F32)
        decay_row = jnp.zeros((1, W_MIX), F32)
        m_new_vec = m_vec
        srcs, col_terms, khs, w_inters, floors = [], [], [], [], []
        for h in range(HEADS):
            ic = d * 8 + h
            fc = d * 8 + HEADS + h
            m_h = m_vec[:, fc:fc + 1]
            cct = cumt[fc:fc + 1, :]
            inter = cct + m_h
            m_t = jnp.maximum(inter, cct + pm[ic:ic + 1, :])
            w_inters.append(jnp.exp(inter - m_t))
            floors.append(jnp.exp(-m_t))
            col_terms.append(jnp.broadcast_to(cct - m_t, (L, L)))
            srcs.append(src_cols[:, ic:ic + 1])
            khs.append(jnp.where(lane_w == h, k, jnp.zeros_like(k)))
        arg = jnp.concatenate(srcs, axis=0) + jnp.concatenate(col_terms, axis=0)
        arg = jnp.where(mask4[d], arg, -jnp.inf)
        sc = _dot(jnp.concatenate(khs, axis=0), qt) * jnp.exp(arg)
        scb = sc.astype(BF16)
        for h in range(HEADS):
            ic = d * 8 + h
            fc = d * 8 + HEADS + h
            head_lanes = lane_w == h
            sl = slice(h * HEAD_DIM, (h + 1) * HEAD_DIM)
            hrows = slice(h * L, (h + 1) * L)
            m_h = m_vec[:, fc:fc + 1]
            cct = cumt[fc:fc + 1, :]
            den = jnp.sum(sc[hrows, :], axis=0, keepdims=True) + w_inters[h] * qn[h:h + 1, :]
            num = _dot(vt[sl, :], scb[hrows, :]) + w_inters[h] * qc[sl, :]
            ht_ref[d, sl, pl.ds(r0, L)] = num / jnp.maximum(jnp.abs(den), floors[h])
            tot_h = tot[fc:fc + 1, :]
            logu = tot_h - cct + lgt[ic:ic + 1, :]
            m_new = jnp.maximum(tot_h + m_h, jnp.max(logu, axis=1, keepdims=True))
            e_h = jnp.exp(logu - m_new)
            e_slabs.append(jnp.broadcast_to(e_h, (HEAD_DIM, L)))
            e_small = jnp.where(sub_g == h, e_h, e_small)
            decay_row = jnp.where(head_lanes, jnp.exp(tot_h + m_h - m_new), decay_row)
            m_new_vec = jnp.where(lane_g == fc, m_new, m_new_vec)
        vet = (vt.astype(F32) * jnp.concatenate(e_slabs, axis=0)).astype(BF16)
        c_ref[d] = decay_row * c_old + jnp.where(bd_mask, _dot(vet, k), 0.0)
        n_ref[d] = decay_row * n_old + jnp.where(n_mask, _dot(e_small.astype(BF16), k), 0.0)
        m_ref[d] = m_new_vec

    c_ref[...] = jnp.zeros_like(c_ref)
    n_ref[...] = jnp.zeros_like(n_ref)
    m_ref[...] = jnp.zeros_like(m_ref)
    n_lat = n_chunks - CTX_LEN // L
    for j in range(n_chunks - n_lat):
        chunk(n_lat + j, 0)
        chunk(n_chunks - 1 - j, 1)

    def scan_body(i, carry):
        chunk(i, 0)
        chunk(n_lat - 1 - i, 1)
        return carry

    lax.fori_loop(0, n_lat, scan_body, 0)

    def readout(c, carry):
        r0 = pl.multiple_of(c * L, L)
        slabs = []
        for h in range(HEADS):
            sl = slice(h * HEAD_DIM, (h + 1) * HEAD_DIM)
            hs = ht_ref[0, sl, pl.ds(r0, L)] + ht_ref[1, sl, pl.ds(r0, L)]
            mu = jnp.mean(hs, axis=0, keepdims=True)
            dl = hs - mu
            var = jnp.mean(dl * dl, axis=0, keepdims=True)
            hn = dl * lax.rsqrt(var + LN_EPS) * nw_ref[sl, :]
            slabs.append(hn * _sigmoid(ot_ref[0, sl, pl.ds(r0, L)]))
        y_ref[0, pl.ds(r0, L), :] = jnp.concatenate(slabs, axis=0).T.astype(y_ref.dtype)
        return carry

    lax.fori_loop(0, n_chunks, readout, 0)


def _mlstm(k, qt, vt, ot, gt, gate_bias, norm_w):
    bsz, s, _ = k.shape
    gbt = gate_bias.reshape(N_GATES, 1)
    nw_full = jnp.broadcast_to(norm_w.reshape(W_MIX, 1), (W_MIX, MLSTM_CHUNK))
    tok = lambda w: pl.BlockSpec((1, s, w), lambda b: (b, 0, 0))
    chan = lambda c: pl.BlockSpec((1, c, s), lambda b: (b, 0, 0))
    return pl.pallas_call(
        _mlstm_kernel,
        grid=(bsz,),
        in_specs=[tok(W_MIX), chan(W_MIX), chan(W_MIX), chan(W_MIX), chan(N_GATES),
                  pl.BlockSpec((N_GATES, 1), lambda b: (0, 0)),
                  pl.BlockSpec((W_MIX, MLSTM_CHUNK), lambda b: (0, 0))],
        out_specs=tok(W_MIX),
        out_shape=jax.ShapeDtypeStruct((bsz, s, W_MIX), BF16),
        scratch_shapes=[pltpu.VMEM((2, W_MIX, s), F32),
                        pltpu.VMEM((s, GATE_PAD), F32),
                        pltpu.VMEM((N_GATES, s), F32),
                        pltpu.VMEM((N_GATES, s), F32),
                        pltpu.VMEM((N_GATES, s), F32),
                        pltpu.VMEM((2, W_MIX, W_MIX), F32),
                        pltpu.VMEM((2, N_GATES, W_MIX), F32),
                        pltpu.VMEM((2, 1, GATE_PAD), F32)],
        compiler_params=_params("arbitrary"),
        name="mlstm",
    )(k, qt, vt, ot, gt, gbt, nw_full)


def _cpow(ar, ai, ld, k):
    dt = jnp.exp(ld)
    mag = jnp.exp(k * (ar * dt))
    ang = k * (ai * dt)
    return mag * jnp.cos(ang), mag * jnp.sin(ang)


def _zoh_coef(ar, ai, ld):
    er, ei = _cpow(ar, ai, ld, 1.0)
    den = ar * ar + ai * ai
    return ((er - 1.0) * ar + ei * ai) / den, (ei * ar - (er - 1.0) * ai) / den


def _dot_x3(a, b):
    a_hi, a_mid, _ = _split3(a)
    b_hi, b_mid, _ = _split3(b)
    return _dot(a_hi, b_hi) + _dot(a_hi, b_mid) + _dot(a_mid, b_hi)


def _tap_table(lag, col, cf):
    fwd = lag >= 0
    k = jnp.abs(lag).astype(F32)
    sel = lambda a, b: jnp.where(fwd, a, b)
    pre, pim = _cpow(sel(col(0), col(3)), sel(col(1), col(4)), sel(col(2), col(5)), k)
    cre, cim = sel(cf[0][0], cf[1][0]), sel(cf[0][1], cf[1][1])
    both = lag == 0
    return (pre * cre - pim * cim + jnp.where(both, cf[1][0], 0.0),
            pre * cim + pim * cre + jnp.where(both, cf[1][1], 0.0))


def _s5_ops_kernel(cols_ref, rows_ref, btre_ref, btim_ref, bdre_ref, bdim_ref, cre_ref, cim_ref,
                   ctre_ref, ctim_ref, kb_ref, wloc_ref, wout_ref, ar_ref,
                   mt_ref, pt_ref, pw_ref, v_ref):
    L, H, N = S5_CHUNK, S5_HALF, S5_N
    i = pl.program_id(1)
    lane = lax.broadcasted_iota(jnp.int32, (1, 2 * L), 1)
    lo = lane < L

    @pl.when(i == 0)
    def _():
        col = lambda k: cols_ref[0, :, k:k + 1]
        row = lambda k: rows_ref[0, k:k + 1, :]
        cf = [_zoh_coef(col(3 * d), col(3 * d + 1), col(3 * d + 2)) for d in range(2)]
        mt_ref[0], mt_ref[1] = _tap_table(jnp.where(lo, lane, lane - 2 * L), col, cf)
        mt_ref[2], mt_ref[3] = _tap_table(lane - L, col, cf)
        tt = lane % L
        ss = lax.broadcasted_iota(jnp.int32, (L, 1), 0)
        for d in range(2):
            kk = (tt + 1 if d == 0 else L - tt).astype(F32)
            pt_ref[d, 0], pt_ref[d, 1] = _cpow(col(3 * d), col(3 * d + 1), col(3 * d + 2), kk)
            ks = (L - 1 - ss if d == 0 else ss).astype(F32)
            qre, qim = _cpow(row(3 * d), row(3 * d + 1), row(3 * d + 2), ks)
            fre, fim = _zoh_coef(row(3 * d), row(3 * d + 1), row(3 * d + 2))
            pw_ref[d, 0] = qre * fre - qim * fim
            pw_ref[d, 1] = qre * fim + qim * fre
            ar_ref[0, 2 * d:2 * d + 1, :], ar_ref[0, 2 * d + 1:2 * d + 2, :] = _cpow(
                row(3 * d), row(3 * d + 1), row(3 * d + 2), float(L))
        ar_ref[0, 4:8, :] = jnp.zeros((4, 2 * N), F32)

    for half in range(2):
        cre = cre_ref[0, pl.ds(i + half * H, 1), :]
        cim = cim_ref[0, pl.ds(i + half * H, 1), :]
        g_re = cre * btre_ref[0] - cim * btim_ref[0]
        g_im = cre * btim_ref[0] + cim * btre_ref[0]
        v_ref[half] = _dot_x3(g_re, mt_ref[2 * half]) - _dot_x3(g_im, mt_ref[2 * half + 1])
    for q in range(S5_P):
        r0 = (q % H) * 2 * L + (q // H) * L
        toe = [pltpu.roll(jnp.broadcast_to(v_ref[half, q:q + 1, :], (L, 2 * L)), 0, 1,
                          stride=1, stride_axis=0) for half in range(2)]
        kb_ref[0, r0:r0 + L, :] = jnp.where(lo, toe[0], toe[1]).astype(BF16)

    lane_p = lax.broadcasted_iota(jnp.int32, (1, S5_P), 1)
    pick = lambda ref, p: jnp.sum(jnp.where(lane_p == p, ref[0], 0.0), axis=1, keepdims=True)
    ccre = jnp.where(lo, pick(ctre_ref, i), pick(ctre_ref, i + H))
    ccim = jnp.where(lo, pick(ctim_ref, i), pick(ctim_ref, i + H))
    pad = jnp.zeros((N, 2 * L), BF16)
    for d in range(2):
        pre, pim = pt_ref[d, 0], pt_ref[d, 1]
        base = d * 4 * N
        wout_ref[0, base:base + N, :] = (ccre * pre - ccim * pim).astype(BF16)
        wout_ref[0, base + N:base + 2 * N, :] = pad
        wout_ref[0, base + 2 * N:base + 3 * N, :] = (-(ccre * pim + ccim * pre)).astype(BF16)
        wout_ref[0, base + 3 * N:base + 4 * N, :] = pad
    for half in range(2):
        bre = bdre_ref[0, pl.ds(i + half * H, 1), :]
        bim = bdim_ref[0, pl.ds(i + half * H, 1), :]
        rows = slice(half * L, (half + 1) * L)
        for d in range(2):
            base = d * 4 * N
            re = pw_ref[d, 0] * bre - pw_ref[d, 1] * bim
            im = pw_ref[d, 0] * bim + pw_ref[d, 1] * bre
            wloc_ref[0, rows, base:base + 2 * N] = jnp.where(lo, re, 0.0).astype(BF16)
            wloc_ref[0, rows, base + 2 * N:base + 4 * N] = jnp.where(lo, im, 0.0).astype(BF16)


def _s5_operators(a_re, a_im, log_dt, b_re, b_im, c_re, c_im):
    L, H, n2 = S5_CHUNK, S5_HALF, 2 * S5_N
    g = S5_GROUPS
    ld = jnp.broadcast_to(log_dt[:, :, None], a_re.shape)
    per_dir = [a_re[0], a_im[0], ld[0], a_re[1], a_im[1], ld[1]]
    zero = jnp.zeros_like(a_re[0])
    cols = jnp.stack(per_dir + [zero, zero], axis=-1)
    dup = lambda v: jnp.concatenate([v, v], axis=-1)
    rows = jnp.stack([dup(v) for v in per_dir + [zero, zero]], axis=1)
    bt_re, bt_im = b_re.transpose(0, 2, 1), b_im.transpose(0, 2, 1)
    ct_re, ct_im = c_re.transpose(0, 2, 1), c_im.transpose(0, 2, 1)
    per_g = lambda a: pl.BlockSpec((1,) + a.shape[1:], lambda gi, i: (gi, 0, 0))
    ins = (cols, rows, bt_re, bt_im, dup(bt_re), dup(bt_im), c_re, c_im, ct_re, ct_im)
    return pl.pallas_call(
        _s5_ops_kernel,
        grid=(g, H),
        in_specs=[per_g(a) for a in ins],
        out_specs=(pl.BlockSpec((1, S5_FLAT, 2 * L), lambda gi, i: (gi, 0, i)),
                   pl.BlockSpec((1, 2 * L, S5_STATE), lambda gi, i: (gi, i, 0)),
                   pl.BlockSpec((1, S5_STATE, 2 * L), lambda gi, i: (gi, 0, i)),
                   pl.BlockSpec((1, 8, n2), lambda gi, i: (gi, 0, 0))),
        out_shape=(jax.ShapeDtypeStruct((g, S5_FLAT, S5_FLAT), BF16),
                   jax.ShapeDtypeStruct((g, S5_FLAT, S5_STATE), BF16),
                   jax.ShapeDtypeStruct((g, S5_STATE, S5_FLAT), BF16),
                   jax.ShapeDtypeStruct((g, 8, n2), F32)),
        scratch_shapes=[pltpu.VMEM((4, S5_N, 2 * L), F32),
                        pltpu.VMEM((2, 2, S5_N, 2 * L), F32),
                        pltpu.VMEM((2, 2, L, n2), F32),
                        pltpu.VMEM((2, S5_P, 2 * L), F32)],
        compiler_params=_params("arbitrary", "arbitrary"),
        name="s5_operators",
    )(*ins)


def _s5_kernel(u_ref, kb_ref, wloc_ref, wout_ref, ar_ref, y_ref, sin_ref, *, bsz, n_ctx_chunks):
    n_chunks = u_ref.shape[1] // bsz
    n2 = 2 * S5_N
    u = u_ref[0]
    sloc = _dot(u, wloc_ref[0])
    ar = ar_ref[0]
    n_lat = n_chunks - n_ctx_chunks
    fwd_order = list(range(n_lat, n_chunks)) + list(range(n_lat))
    bwd_order = list(range(n_chunks - 1, n_lat - 1, -1)) + list(range(n_lat - 1, -1, -1))
    for d, order in ((0, fwd_order), (1, bwd_order)):
        a_re = ar[2 * d:2 * d + 1, :]
        a_im = ar[2 * d + 1:2 * d + 2, :]
        re_cols = slice(2 * d * n2, (2 * d + 1) * n2)
        im_cols = slice((2 * d + 1) * n2, (2 * d + 2) * n2)
        st_re = jnp.zeros((bsz, n2), F32)
        st_im = jnp.zeros((bsz, n2), F32)
        for c in order:
            rows = slice(c * bsz, (c + 1) * bsz)
            sin_ref[rows, re_cols] = st_re.astype(BF16)
            sin_ref[rows, im_cols] = st_im.astype(BF16)
            st_re, st_im = (a_re * st_re - a_im * st_im + sloc[rows, re_cols],
                            a_re * st_im + a_im * st_re + sloc[rows, im_cols])
    y = _dot(u, kb_ref[0]) + _dot(sin_ref[...], wout_ref[0])
    for i in range(S5_HALF):
        y_ref[0, :, :, i, :] = y[:, i * n2:(i + 1) * n2].reshape(n_chunks, bsz, n2)


def _s5_mix(ut, kbig, w_loc, w_out, a_rows, bsz):
    g, rows, flat = ut.shape
    kern = functools.partial(_s5_kernel, bsz=bsz, n_ctx_chunks=CTX_LEN // S5_CHUNK)
    per_g = lambda a: pl.BlockSpec((1,) + a.shape[1:], lambda gi: (gi, 0, 0))
    return pl.pallas_call(
        kern,
        grid=(g,),
        in_specs=[per_g(a) for a in (ut, kbig, w_loc, w_out, a_rows)],
        out_specs=pl.BlockSpec((1, rows // bsz, bsz, S5_HALF, flat // S5_HALF), lambda gi: (gi, 0, 0, 0, 0)),
        out_shape=jax.ShapeDtypeStruct((g, rows // bsz, bsz, S5_HALF, flat // S5_HALF), F32),
        scratch_shapes=[pltpu.VMEM((rows, S5_STATE), BF16)],
        compiler_params=_params("arbitrary"),
        name="s5_mix",
    )(ut, kbig, w_loc, w_out, a_rows)


def _s5_readout(y5_ref, zd_ref, d_ref, gw_ref, gb_ref, yd_ref):
    lo = lax.broadcasted_iota(jnp.int32, (1, 2 * S5_CHUNK), 1) < S5_CHUNK
    half_rows = (S5_GROUPS, 1, S5_HALF, 2 * S5_CHUNK)
    for j in range(zd_ref.shape[1] // (2 * S5_CHUNK)):
        rows = slice(j * 2 * S5_CHUNK, (j + 1) * 2 * S5_CHUNK)
        a = y5_ref[:, 2 * j, 0, :, :].reshape(S5_GROUPS * S5_HALF, 2 * S5_CHUNK)
        b = y5_ref[:, 2 * j + 1, 0, :, :].reshape(S5_GROUPS * S5_HALF, 2 * S5_CHUNK)
        first = jnp.where(lo, a, pltpu.roll(b, S5_CHUNK, 1))
        second = jnp.where(lo, pltpu.roll(a, S5_CHUNK, 1), b)
        yt = jnp.concatenate([first.reshape(half_rows), second.reshape(half_rows)],
                             axis=1).reshape(W_MIX, 2 * S5_CHUNK)
        y = _gelu(yt.T + d_ref[...] * zd_ref[0, rows, :])
        gate = _sigmoid(_dot(y.astype(BF16), gw_ref[...]) + gb_ref[...])
        yd_ref[rows, :] = (y * gate).astype(yd_ref.dtype)


def _out_ffn_kernel(x_ref, ya_ref, yb_ref, yc_ref, y5_ref, zd_ref, modb_ref, modc_ref, d5_ref,
                    gw_ref, gb_ref, wo_ref, l1w_ref, l1b_ref, wi_ref, wd_ref, l2w_ref, l2b_ref,
                    o_ref, yd_ref, x1_ref, h2_ref, a_ref, *, ctx_rows):
    tm = x_ref.shape[1]
    n_lat = tm - ctx_rows
    segs = [(0, n_lat, 0)] + ([(n_lat, tm, 1)] if ctx_rows else [])
    _s5_readout(y5_ref, zd_ref, d5_ref, gw_ref, gb_ref, yd_ref)
    y = (_dot(ya_ref[0], wo_ref[0:W_MIX, :]) + _dot(yb_ref[0], wo_ref[W_MIX:2 * W_MIX, :]) +
         _dot(yc_ref[0], wo_ref[2 * W_MIX:3 * W_MIX, :]) + _dot(yd_ref[...], wo_ref[3 * W_MIX:4 * W_MIX, :]))
    for r0, r1, which in segs:
        g1 = _tile_mod(modb_ref, modc_ref, 2)[which]
        sh2 = _tile_mod(modb_ref, modc_ref, 3)[which]
        sc2 = _tile_mod(modb_ref, modc_ref, 4)[which]
        x1 = _layer_norm(ALPHA * x_ref[0, r0:r1, :] + g1 * y[r0:r1, :], l1w_ref[...], l1b_ref[...])
        x1_ref[r0:r1, :] = x1
        h2_ref[r0:r1, :] = (x1 * (1.0 + sc2) + sh2).astype(BF16)
    h2 = h2_ref[...]
    for kk in range(D_FF // FF_CHUNK):
        cols = slice(kk * FF_CHUNK, (kk + 1) * FF_CHUNK)
        gk = _dot(h2, wi_ref[:, cols])
        uk = _dot(h2, wi_ref[:, D_FF + kk * FF_CHUNK:D_FF + (kk + 1) * FF_CHUNK])
        a_ref[:, cols] = (_silu(gk) * uk).astype(BF16)
    f = _dot(a_ref[...], wd_ref[...])
    for r0, r1, which in segs:
        g2 = _tile_mod(modb_ref, modc_ref, 5)[which]
        o_ref[0, r0:r1, :] = _layer_norm(ALPHA * x1_ref[r0:r1, :] + g2 * f[r0:r1, :],
                                         l2w_ref[...], l2b_ref[...])


def _out_ffn(xcat, ya, yb, yc, y5, zd, mod, s5_d, glu_w, glu_b, w_out, ln1_w, ln1_b, w_ffn_in,
             w_ffn_out, ln2_w, ln2_b, tm, n_out):
    bsz, s, d = xcat.shape
    ctx_rows = CTX_LEN if n_out == s else 0
    tok = lambda w: pl.BlockSpec((1, tm, w), lambda b, j: (b, j, 0))
    row = lambda a: a.reshape(1, -1)
    return pl.pallas_call(
        functools.partial(_out_ffn_kernel, ctx_rows=ctx_rows),
        grid=(bsz, n_out // tm),
        in_specs=[tok(d), tok(W_MIX), tok(W_MIX), tok(W_MIX),
                  pl.BlockSpec((S5_GROUPS, tm // S5_CHUNK, 1, S5_HALF, 2 * S5_CHUNK),
                               lambda b, j: (0, j, b, 0, 0)),
                  tok(W_MIX),
                  pl.BlockSpec((1, 6, d), lambda b, j: (b, 0, 0)),
                  pl.BlockSpec((1, 6, d), lambda b, j: (bsz, 0, 0)),
                  _const_spec((1, W_MIX)), _const_spec((W_MIX, W_MIX)), _const_spec((1, W_MIX)),
                  _const_spec(w_out.shape), _const_spec((1, d)), _const_spec((1, d)),
                  _const_spec(w_ffn_in.shape), _const_spec(w_ffn_out.shape),
                  _const_spec((1, d)), _const_spec((1, d))],
        out_specs=tok(d),
        out_shape=jax.ShapeDtypeStruct((bsz, n_out, d), F32),
        scratch_shapes=[pltpu.VMEM((tm, W_MIX), BF16), pltpu.VMEM((tm, d), F32),
                        pltpu.VMEM((tm, d), BF16), pltpu.VMEM((tm, D_FF), BF16)],
        compiler_params=_params("arbitrary", "arbitrary"),
        name="out_ffn",
    )(xcat, ya, yb, yc, y5, zd, mod, mod, row(s5_d), glu_w.astype(BF16), row(glu_b), w_out,
      row(ln1_w), row(ln1_b), w_ffn_in, w_ffn_out, row(ln2_w), row(ln2_b))


_S5_LOCAL = np.arange(S5_P).reshape(S5_HALF, 2).T.reshape(-1)
_S5_ORDER = (np.arange(S5_GROUPS)[:, None] * S5_P + _S5_LOCAL[None, :]).reshape(-1)


def _reorder_w_in(w):
    off_b = 4 * W_MIX + N_GATES
    off_d = w.shape[1] - W_MIX
    w_r = jnp.concatenate([w[:, :4 * W_MIX], w[:, off_b:off_d], w[:, off_d:][:, _S5_ORDER]], axis=1)
    return w_r.astype(BF16), w[:, 4 * W_MIX:off_b].T.astype(BF16)


def kernel(x, c, ctx, c_ctx, w_mod, b_mod, w_in, mlstm_gate_bias, mlstm_norm_w, sgu_ln_w, sgu_ln_b, sgu_w, sgu_b, conv_w, conv_b, conv_ln_w, conv_ln_b, s5_a_re, s5_a_im, s5_log_dt, s5_b_re, s5_b_im, s5_c_re, s5_c_im, s5_d, s5_glu_w, s5_glu_b, w_out, ln1_w, ln1_b, w_ffn_in, w_ffn_out, ln2_w, ln2_b):
    bsz, seq, d = x.shape
    s = seq + CTX_LEN
    assert seq == SEQ_LATENT and ctx.shape[1] == CTX_LEN
    mod_rows = -(-(bsz + 1) // 8) * 8
    cvec = jnp.zeros((mod_rows, d), F32).at[:bsz].set(c).at[bsz].set(c_ctx)
    mod_all = _modulation(cvec, w_mod, b_mod).reshape(DEPTH, mod_rows, 6, d)
    nc5 = s // S5_CHUNK
    for l in range(DEPTH):
        mod = mod_all[l]
        w_r, wg_t = _reorder_w_in(w_in[l])
        src = (x, ctx, 0, True) if l == 0 else (xcat, xcat, seq // TILE, False)
        outs = _in_projection(*src, mod, w_r, wg_t, sgu_w[l], sgu_b[l], sgu_ln_w[l], sgu_ln_b[l],
                              conv_w[l], conv_b[l], conv_ln_w[l], conv_ln_b[l])
        if l == 0:
            xcat, outs = outs[0], outs[1:]
        k, qt, vt, ot, yb, yc, zd, ut, gt = outs
        ya = _mlstm(k, qt, vt, ot, gt, mlstm_gate_bias[l], mlstm_norm_w[l])
        kbig, w_loc, w_so, a_rows = _s5_operators(
            s5_a_re[l], s5_a_im[l], s5_log_dt[l], s5_b_re[l][:, :, _S5_LOCAL], s5_b_im[l][:, :, _S5_LOCAL],
            s5_c_re[l][:, _S5_LOCAL, :], s5_c_im[l][:, _S5_LOCAL, :])
        y5 = _s5_mix(ut.reshape(S5_GROUPS, nc5 * bsz, S5_FLAT), kbig, w_loc, w_so, a_rows, bsz)
        last = l == DEPTH - 1
        w_out_l = jnp.concatenate([w_out[l][:3 * W_MIX], w_out[l][3 * W_MIX:][_S5_ORDER]], axis=0)
        xcat = _out_ffn(xcat, ya, yb, yc, y5, zd, mod, s5_d[l][_S5_ORDER],
                        s5_glu_w[l][_S5_ORDER][:, _S5_ORDER], s5_glu_b[l][_S5_ORDER],
                        w_out_l.astype(BF16), ln1_w[l], ln1_b[l],
                        w_ffn_in[l].astype(BF16), w_ffn_out[l].astype(BF16), ln2_w[l], ln2_b[l],
                        tm=TM_LATENT if last else TM_MAIN, n_out=seq if last else s)
    return xcat
```

```python
import functools

import numpy as np
import jax
import jax.numpy as jnp
from jax import lax
from jax.experimental import pallas as pl
from jax.experimental.pallas import tpu as pltpu

F32 = jnp.float32
BF16 = jnp.bfloat16

D_MODEL = 1024
DEPTH = 2
CTX_LEN = 256
SEQ_LATENT = 2048
GRID_W = 64
W_MIX = 256
HEADS = 4
HEAD_DIM = W_MIX // HEADS
SGU_GROUPS = 4
SGU_GD = W_MIX // SGU_GROUPS
SGU_CHUNK = 128
CONV_WIDTH = 31
CONV_PAD = CONV_WIDTH // 2
CONV_HALO = 32
S5_P = 16
S5_GROUPS = W_MIX // S5_P
S5_N = 64
S5_CHUNK = 64
S5_HALF = S5_P // 2
S5_FLAT = S5_P * S5_CHUNK
S5_STATE = 8 * S5_N
D_FF = 2816
FF_CHUNK = 256
ALPHA = (2 * DEPTH) ** 0.25
LN_EPS = 1e-5
TILE = 256
MLSTM_CHUNK = 256
TM_MAIN = 3 * TILE
TM_LATENT = 2 * TILE
CONV_ROWS = max((TILE // GRID_W) * (GRID_W + CONV_HALO), CTX_LEN + CONV_HALO)
N_GATES = 4 * HEADS
GATE_PAD = 128
C_O = 3 * W_MIX
C_B = 4 * W_MIX
C_C = C_B + 2 * W_MIX
C_D = C_C + 2 * W_MIX
C_G = C_D + W_MIX
VMEM_LIMIT = 56 * 1024 * 1024


def _dot(a, b):
    return jnp.dot(a, b, preferred_element_type=F32)


def _dot_nt(a, b):
    return lax.dot_general(a, b, (((1,), (1,)), ((), ())), preferred_element_type=F32)


def _split3(x):
    hi = x.astype(BF16)
    r = x - hi.astype(F32)
    mid = r.astype(BF16)
    lo = (r - mid.astype(F32)).astype(BF16)
    return hi, mid, lo


def _dot3_r(w_bf16, x):
    hi, mid, lo = _split3(x)
    return _dot(w_bf16, hi) + _dot(w_bf16, mid) + _dot(w_bf16, lo)


def _layer_norm(x, w, b):
    mu = jnp.mean(x, axis=-1, keepdims=True)
    d = x - mu
    var = jnp.mean(d * d, axis=-1, keepdims=True)
    return d * lax.rsqrt(var + LN_EPS) * w + b


def _sigmoid(x):
    return jax.nn.sigmoid(x)


def _silu(x):
    return x * jax.nn.sigmoid(x)


def _gelu(x):
    return jax.nn.gelu(x, approximate=True)


def _log_sigmoid(x):
    return jnp.minimum(x, 0.0) - jnp.log1p(jnp.exp(-jnp.abs(x)))


def _params(*sem):
    return pltpu.CompilerParams(dimension_semantics=sem, vmem_limit_bytes=VMEM_LIMIT)


def _const_spec(shape):
    nd = len(shape)
    return pl.BlockSpec(shape, lambda *_: (0,) * nd, pipeline_mode=pl.Buffered(1))


def _layer_spec(stacked, l):
    nd = stacked.ndim
    return pl.BlockSpec((1,) + stacked.shape[1:], lambda *_: (l,) + (0,) * (nd - 1),
                        pipeline_mode=pl.Buffered(1))


def _mod_kernel(c_ref, w_ref, b_ref, o_ref):
    s = _silu(c_ref[...])
    s_hi = s.astype(BF16)
    s_lo = (s - s_hi.astype(F32)).astype(BF16)
    w = w_ref[0]
    w_hi = w.astype(BF16)
    w_lo = (w - w_hi.astype(F32)).astype(BF16)
    o_ref[0] = _dot(s_hi, w_hi) + _dot(s_hi, w_lo) + _dot(s_lo, w_hi) + b_ref[0]


def _modulation(cvec, w_mod, b_mod):
    depth, d, n = w_mod.shape
    rows = cvec.shape[0]
    tn = 1024
    return pl.pallas_call(
        _mod_kernel,
        grid=(depth, n // tn),
        in_specs=[pl.BlockSpec((rows, d), lambda l, j: (0, 0)),
                  pl.BlockSpec((1, d, tn), lambda l, j: (l, 0, j)),
                  pl.BlockSpec((1, 1, tn), lambda l, j: (l, 0, j))],
        out_specs=pl.BlockSpec((1, rows, tn), lambda l, j: (l, 0, j)),
        out_shape=jax.ShapeDtypeStruct((depth, rows, n), F32),
        compiler_params=_params("arbitrary", "arbitrary"),
        name="modulation",
    )(cvec, w_mod, b_mod.reshape(depth, 1, n))


def _tile_mod(modb_ref, modc_ref, i):
    last = pl.program_id(1) == pl.num_programs(1) - 1
    row_b = modb_ref[0, i:i + 1, :]
    return row_b, jnp.where(last, modc_ref[0, i:i + 1, :], row_b)


def _sgu_slab(zb, sgw_ref, sgb_ref, slnw_ref, slnb_ref):
    u = _gelu(zb[:, 0:W_MIX])
    vn = _layer_norm(_gelu(zb[:, W_MIX:2 * W_MIX]), slnw_ref[...], slnb_ref[...]).astype(BF16)
    lane_grp = lax.broadcasted_iota(jnp.int32, (1, W_MIX), 1) // SGU_GD
    out = []
    for cl in range(TILE // SGU_CHUNK):
        vc = vn[cl * SGU_CHUNK:(cl + 1) * SGU_CHUNK, :]
        acc = sgb_ref[...]
        for gi in range(SGU_GROUPS):
            acc = acc + _dot(sgw_ref[gi], jnp.where(lane_grp == gi, vc, jnp.zeros_like(vc)))
        out.append(acc)
    return u * jnp.concatenate(out, axis=0)


def _conv_slab(a, seg_len, cw_ref, sh_ref):
    n_seg = TILE // seg_len
    stride = seg_len + CONV_HALO
    zeros16 = jnp.zeros((16, W_MIX), F32)
    for sg in range(n_seg):
        base = sg * stride
        seg = a[sg * seg_len:(sg + 1) * seg_len, :]
        for r in range(8):
            sh_ref[r, pl.ds(base, 16), :] = zeros16
            sh_ref[r, pl.ds(base + seg_len + 8, 16), :] = zeros16
            sh_ref[r, pl.ds(base + 16 - r, seg_len), :] = seg
    out = []
    for sg in range(n_seg):
        base = sg * stride
        acc = jnp.zeros((seg_len, W_MIX), F32)
        for t in range(CONV_WIDTH):
            off = t - CONV_PAD
            r = off % 8
            acc = acc + cw_ref[t:t + 1, :] * sh_ref[r, pl.ds(base + (off - r) + 16, seg_len), :]
        out.append(acc)
    return out[0] if n_seg == 1 else jnp.concatenate(out, axis=0)


def _emit_s5_rows(zdt, ut_ref, chunk0):
    n_tok = zdt.shape[1]
    lo = lax.broadcasted_iota(jnp.int32, (1, 2 * S5_CHUNK), 1) < S5_CHUNK
    halves = zdt.reshape(S5_GROUPS, 2, S5_HALF, n_tok)
    first = halves[:, 0].reshape(S5_GROUPS * S5_HALF, n_tok)
    second = halves[:, 1].reshape(S5_GROUPS * S5_HALF, n_tok)
    for j in range(n_tok // (2 * S5_CHUNK)):
        a = first[:, j * 2 * S5_CHUNK:(j + 1) * 2 * S5_CHUNK]
        b = second[:, j * 2 * S5_CHUNK:(j + 1) * 2 * S5_CHUNK]
        pieces = (jnp.where(lo, a, pltpu.roll(b, S5_CHUNK, 1)),
                  jnp.where(lo, pltpu.roll(a, S5_CHUNK, 1), b))
        for cc, piece in enumerate(pieces):
            ut_ref[:, chunk0 + 2 * j + cc, 0, :, :] = (
                piece.reshape(S5_GROUPS, S5_HALF, 2 * S5_CHUNK).astype(BF16))


def _inproj_kernel(*refs, n_slabs, emit_stream):
    x_refs = refs[:n_slabs]
    (ctx_ref, modb_ref, modc_ref, w_ref, wgt_ref, sgw_ref, sgb_ref, slnw_ref, slnb_ref,
     cw_ref, cb_ref, clnw_ref, clnb_ref) = refs[n_slabs:n_slabs + 13]
    outs = refs[n_slabs + 13:]
    if emit_stream:
        xo_ref, outs = outs[0], outs[1:]
    k_ref, qt_ref, vt_ref, ot_ref, yb_ref, yc_ref, zd_ref, ut_ref, gt_ref, h_ref, sh_ref = outs
    last = pl.program_id(1) == pl.num_programs(1) - 1
    sh_b, sh_c = _tile_mod(modb_ref, modc_ref, 0)
    sc_b, sc_c = _tile_mod(modb_ref, modc_ref, 1)
    for r in range(n_slabs):
        rows = slice(r * TILE, (r + 1) * TILE)
        x = x_refs[r][0]
        sc, sh = sc_b, sh_b
        if r == n_slabs - 1:
            x = jnp.where(last, ctx_ref[0], x)
            sc, sh = sc_c, sh_c
        if emit_stream:
            xo_ref[0, rows, :] = x
        h_ref[rows, :] = (x * (1.0 + sc) + sh).astype(BF16)
    for r in range(n_slabs):
        rows = slice(r * TILE, (r + 1) * TILE)
        h = h_ref[rows, :]
        qt_ref[0, :, rows] = _dot(h, w_ref[0, :, 0:W_MIX]).T.astype(BF16)
        k_ref[0, rows, :] = (_dot(h, w_ref[0, :, W_MIX:2 * W_MIX]) * (HEAD_DIM ** -0.5)).astype(BF16)
        vt_ref[0, :, rows] = _dot(h, w_ref[0, :, 2 * W_MIX:3 * W_MIX]).T.astype(BF16)
        ot_ref[0, :, rows] = _dot(h, w_ref[0, :, C_O:C_B]).T
        zd = _dot(h, w_ref[0, :, C_D:C_G])
        zd_ref[0, rows, :] = zd
        _emit_s5_rows(zd.T, ut_ref, r * (TILE // S5_CHUNK))
        gt_ref[0, :, rows] = _dot_nt(wgt_ref[0], h)
        yb = _sgu_slab(_dot(h, w_ref[0, :, C_B:C_C]), sgw_ref, sgb_ref, slnw_ref, slnb_ref)
        yb_ref[0, rows, :] = yb.astype(yb_ref.dtype)
        zc = _dot(h, w_ref[0, :, C_C:C_D])
        a = zc[:, 0:W_MIX] * _sigmoid(zc[:, W_MIX:2 * W_MIX])

        def conv_out(seg_len, a=a, rows=rows, r=r):
            acc = _conv_slab(a, seg_len, cw_ref, sh_ref.at[r])
            yc = _silu(_layer_norm(acc + cb_ref[...], clnw_ref[...], clnb_ref[...]))
            yc_ref[0, rows, :] = yc.astype(yc_ref.dtype)

        if r < n_slabs - 1:
            conv_out(GRID_W)
        else:
            pl.when(jnp.logical_not(last))(functools.partial(conv_out, GRID_W))
            pl.when(last)(functools.partial(conv_out, CTX_LEN))


def _in_projection(lat, ctx, ctx_block, emit_stream, mod, layer, w_r, wg_t, sgu_w, sgu_b, sgu_ln_w,
                   sgu_ln_b, conv_w, conv_b, conv_ln_w, conv_ln_b):
    bsz, _, d = lat.shape
    s = SEQ_LATENT + CTX_LEN
    tm = TM_MAIN
    n_slabs = tm // TILE
    last_lat = SEQ_LATENT // TILE - 1
    slab = lambda r: pl.BlockSpec((1, TILE, d), lambda b, j: (b, jnp.minimum(n_slabs * j + r, last_lat), 0))
    nc5 = s // S5_CHUNK
    cpt = tm // S5_CHUNK
    bias_full = jnp.repeat(sgu_b.T, SGU_GD, axis=1)
    row = lambda a: a.reshape(1, W_MIX)
    tok = lambda w: pl.BlockSpec((1, tm, w), lambda b, j: (b, j, 0))
    chan = lambda c: pl.BlockSpec((1, c, tm), lambda b, j: (b, 0, j))
    out_shape = (
        jax.ShapeDtypeStruct((bsz, s, W_MIX), BF16),
        jax.ShapeDtypeStruct((bsz, W_MIX, s), BF16),
        jax.ShapeDtypeStruct((bsz, W_MIX, s), BF16),
        jax.ShapeDtypeStruct((bsz, W_MIX, s), F32),
        jax.ShapeDtypeStruct((bsz, s, W_MIX), BF16),
        jax.ShapeDtypeStruct((bsz, s, W_MIX), BF16),
        jax.ShapeDtypeStruct((bsz, s, W_MIX), F32),
        jax.ShapeDtypeStruct((S5_GROUPS, nc5, bsz, S5_HALF, 2 * S5_CHUNK), BF16),
        jax.ShapeDtypeStruct((bsz, N_GATES, s), F32),
    )
    out_specs = (
        tok(W_MIX), chan(W_MIX), chan(W_MIX), chan(W_MIX), tok(W_MIX), tok(W_MIX), tok(W_MIX),
        pl.BlockSpec((S5_GROUPS, cpt, 1, S5_HALF, 2 * S5_CHUNK), lambda b, j: (0, j, b, 0, 0)),
        chan(N_GATES),
    )
    if emit_stream:
        out_shape = (jax.ShapeDtypeStruct((bsz, s, d), F32),) + out_shape
        out_specs = (tok(d),) + out_specs
    return pl.pallas_call(
        functools.partial(_inproj_kernel, n_slabs=n_slabs, emit_stream=emit_stream),
        grid=(bsz, s // tm),
        in_specs=[slab(r) for r in range(n_slabs)] + [
                  pl.BlockSpec((1, TILE, d), lambda b, j: (b, ctx_block, 0)),
                  pl.BlockSpec((1, 6, d), lambda b, j: (b, 0, 0)),
                  pl.BlockSpec((1, 6, d), lambda b, j: (bsz, 0, 0)),
                  _layer_spec(w_r, layer),
                  _layer_spec(wg_t, layer),
                  _const_spec((SGU_GROUPS, SGU_CHUNK, SGU_CHUNK)),
                  _const_spec((SGU_CHUNK, W_MIX)),
                  _const_spec((1, W_MIX)), _const_spec((1, W_MIX)),
                  _const_spec((CONV_WIDTH, W_MIX)), _const_spec((1, W_MIX)),
                  _const_spec((1, W_MIX)), _const_spec((1, W_MIX))],
        out_specs=out_specs,
        out_shape=out_shape,
        scratch_shapes=[pltpu.VMEM((tm, d), BF16),
                        pltpu.VMEM((tm // TILE, 8, CONV_ROWS, W_MIX), F32)],
        compiler_params=_params("arbitrary", "arbitrary"),
        name="in_projection",
    )(*([lat] * n_slabs), ctx, mod, mod, w_r, wg_t, sgu_w.astype(BF16), bias_full, row(sgu_ln_w),
      row(sgu_ln_b), conv_w, row(conv_b), row(conv_ln_w), row(conv_ln_b))


def _mlstm_kernel(k_ref, qt_ref, vt_ref, ot_ref, gt_ref, gbt_ref, nw_ref, y_ref,
                  ht_ref, src_ref, lgt_ref, cumt_ref, pm_ref, c_ref, n_ref, m_ref):
    L = MLSTM_CHUNK
    n_chunks = k_ref.shape[1] // L
    row = lax.broadcasted_iota(jnp.int32, (L, L), 0)
    col = lax.broadcasted_iota(jnp.int32, (L, L), 1)
    tri = (col <= row).astype(BF16)
    lane_g = lax.broadcasted_iota(jnp.int32, (1, GATE_PAD), 1)
    sub_g = lax.broadcasted_iota(jnp.int32, (N_GATES, 1), 0)
    is_f_sub = (sub_g % 8) >= HEADS
    lane_w = lax.broadcasted_iota(jnp.int32, (1, W_MIX), 1) // HEAD_DIM
    sub_w = lax.broadcasted_iota(jnp.int32, (W_MIX, 1), 0) // HEAD_DIM
    bd_mask = sub_w == lane_w
    n_mask = sub_g == lane_w
    row4 = lax.broadcasted_iota(jnp.int32, (HEADS * L, L), 0) % L
    col4 = lax.broadcasted_iota(jnp.int32, (HEADS * L, L), 1)
    mask4 = (row4 <= col4, row4 >= col4)

    for c in range(n_chunks):
        cols = slice(c * L, (c + 1) * L)
        grawt = gt_ref[0, :, cols] + gbt_ref[...]
        lgt = jnp.where(is_f_sub, _log_sigmoid(grawt), grawt)
        lgt_ref[:, cols] = lgt
        hi, mid, lo = _split3(lgt)
        pret = _dot_nt(hi, tri) + _dot_nt(mid, tri) + _dot_nt(lo, tri)
        cumt = jnp.where(sub_g >= 8, pret[:, L - 1:L] - pret + lgt, pret)
        cumt_ref[:, cols] = cumt
        srct = jnp.where(is_f_sub, -jnp.inf, lgt - pltpu.roll(cumt, N_GATES - HEADS, 0))
        pm_ref[:, cols] = srct
        padded = jnp.concatenate([srct, jnp.zeros((GATE_PAD - N_GATES, L), F32)], axis=0)
        src_ref[cols, :] = padded.T
    pos = lax.broadcasted_iota(jnp.int32, (1, pm_ref.shape[1]), 1) % L
    pm = pm_ref[...]
    sh = 1
    while sh < L:
        fwd_v = jnp.where(pos >= sh, pltpu.roll(pm, sh, 1), -jnp.inf)
        bwd_v = jnp.where(pos < L - sh, pltpu.roll(pm, pm.shape[1] - sh, 1), -jnp.inf)
        pm = jnp.maximum(pm, jnp.where(sub_g >= 8, bwd_v, fwd_v))
        sh *= 2
    pm_ref[...] = pm

    def chunk(c, d):
        r0 = pl.multiple_of(c * L, L)
        k = k_ref[0, pl.ds(r0, L), :]
        qt = qt_ref[0, :, pl.ds(r0, L)]
        vt = vt_ref[0, :, pl.ds(r0, L)]
        src_cols = src_ref[pl.ds(r0, L), :]
        lgt = lgt_ref[:, pl.ds(r0, L)]
        cumt = cumt_ref[:, pl.ds(r0, L)]
        pm = pm_ref[:, pl.ds(r0, L)]
        tot = cumt[:, L - 1:L] if d == 0 else cumt[:, 0:1]
        m_vec = m_ref[d]
        c_old = c_ref[d]
        n_old = n_ref[d]
        qc = _dot(c_old.astype(BF16), qt)
        qn = _dot(n_old.astype(BF16), qt)
        e_slabs = []
        e_small = jnp.zeros((N_GATES, L), F32)
        decay_row = jnp.zeros((1, W_MIX), F32)
        m_new_vec = m_vec
        srcs, col_terms, khs, w_inters, floors = [], [], [], [], []
        for h in range(HEADS):
            ic = d * 8 + h
            fc = d * 8 + HEADS + h
            m_h = m_vec[:, fc:fc + 1]
            cct = cumt[fc:fc + 1, :]
            inter = cct + m_h
            m_t = jnp.maximum(inter, cct + pm[ic:ic + 1, :])
            w_inters.append(jnp.exp(inter - m_t))
            floors.append(jnp.exp(-m_t))
            col_terms.append(jnp.broadcast_to(cct - m_t, (L, L)))
            srcs.append(src_cols[:, ic:ic + 1])
            khs.append(jnp.where(lane_w == h, k, jnp.zeros_like(k)))
        arg = jnp.concatenate(srcs, axis=0) + jnp.concatenate(col_terms, axis=0)
        arg = jnp.where(mask4[d], arg, -jnp.inf)
        sc = _dot(jnp.concatenate(khs, axis=0), qt) * jnp.exp(arg)
        scb = sc.astype(BF16)
        for h in range(HEADS):
            ic = d * 8 + h
            fc = d * 8 + HEADS + h
            head_lanes = lane_w == h
            sl = slice(h * HEAD_DIM, (h + 1) * HEAD_DIM)
            hrows = slice(h * L, (h + 1) * L)
            m_h = m_vec[:, fc:fc + 1]
            cct = cumt[fc:fc + 1, :]
            den = jnp.sum(sc[hrows, :], axis=0, keepdims=True) + w_inters[h] * qn[h:h + 1, :]
            num = _dot(vt[sl, :], scb[hrows, :]) + w_inters[h] * qc[sl, :]
            ht_ref[d, sl, pl.ds(r0, L)] = num / jnp.maximum(jnp.abs(den), floors[h])
            tot_h = tot[fc:fc + 1, :]
            logu = tot_h - cct + lgt[ic:ic + 1, :]
            m_new = jnp.maximum(tot_h + m_h, jnp.max(logu, axis=1, keepdims=True))
            e_h = jnp.exp(logu - m_new)
            e_slabs.append(jnp.broadcast_to(e_h, (HEAD_DIM, L)))
            e_small = jnp.where(sub_g == h, e_h, e_small)
            decay_row = jnp.where(head_lanes, jnp.exp(tot_h + m_h - m_new), decay_row)
            m_new_vec = jnp.where(lane_g == fc, m_new, m_new_vec)
        vet = (vt.astype(F32) * jnp.concatenate(e_slabs, axis=0)).astype(BF16)
        c_ref[d] = decay_row * c_old + jnp.where(bd_mask, _dot(vet, k), 0.0)
        n_ref[d] = decay_row * n_old + jnp.where(n_mask, _dot(e_small.astype(BF16), k), 0.0)
        m_ref[d] = m_new_vec

    c_ref[...] = jnp.zeros_like(c_ref)
    n_ref[...] = jnp.zeros_like(n_ref)
    m_ref[...] = jnp.zeros_like(m_ref)
    n_lat = n_chunks - CTX_LEN // L
    for j in range(n_chunks - n_lat):
        chunk(n_lat + j, 0)
        chunk(n_chunks - 1 - j, 1)

    def scan_body(i, carry):
        chunk(i, 0)
        chunk(n_lat - 1 - i, 1)
        return carry

    lax.fori_loop(0, n_lat, scan_body, 0)

    def readout(c, carry):
        r0 = pl.multiple_of(c * L, L)
        slabs = []
        for h in range(HEADS):
            sl = slice(h * HEAD_DIM, (h + 1) * HEAD_DIM)
            hs = ht_ref[0, sl, pl.ds(r0, L)] + ht_ref[1, sl, pl.ds(r0, L)]
            mu = jnp.mean(hs, axis=0, keepdims=True)
            dl = hs - mu
            var = jnp.mean(dl * dl, axis=0, keepdims=True)
            hn = dl * lax.rsqrt(var + LN_EPS) * nw_ref[sl, :]
            slabs.append(hn * _sigmoid(ot_ref[0, sl, pl.ds(r0, L)]))
        y_ref[0, pl.ds(r0, L), :] = jnp.concatenate(slabs, axis=0).T.astype(y_ref.dtype)
        return carry

    lax.fori_loop(0, n_chunks, readout, 0)


def _mlstm(k, qt, vt, ot, gt, gate_bias, norm_w):
    bsz, s, _ = k.shape
    gbt = gate_bias.reshape(N_GATES, 1)
    nw_full = jnp.broadcast_to(norm_w.reshape(W_MIX, 1), (W_MIX, MLSTM_CHUNK))
    tok = lambda w: pl.BlockSpec((1, s, w), lambda b: (b, 0, 0))
    chan = lambda c: pl.BlockSpec((1, c, s), lambda b: (b, 0, 0))
    return pl.pallas_call(
        _mlstm_kernel,
        grid=(bsz,),
        in_specs=[tok(W_MIX), chan(W_MIX), chan(W_MIX), chan(W_MIX), chan(N_GATES),
                  pl.BlockSpec((N_GATES, 1), lambda b: (0, 0)),
                  pl.BlockSpec((W_MIX, MLSTM_CHUNK), lambda b: (0, 0))],
        out_specs=tok(W_MIX),
        out_shape=jax.ShapeDtypeStruct((bsz, s, W_MIX), BF16),
        scratch_shapes=[pltpu.VMEM((2, W_MIX, s), F32),
                        pltpu.VMEM((s, GATE_PAD), F32),
                        pltpu.VMEM((N_GATES, s), F32),
                        pltpu.VMEM((N_GATES, s), F32),
                        pltpu.VMEM((N_GATES, s), F32),
                        pltpu.VMEM((2, W_MIX, W_MIX), F32),
                        pltpu.VMEM((2, N_GATES, W_MIX), F32),
                        pltpu.VMEM((2, 1, GATE_PAD), F32)],
        compiler_params=_params("arbitrary"),
        name="mlstm",
    )(k, qt, vt, ot, gt, gbt, nw_full)


def _cpow(ar, ai, ld, k):
    dt = jnp.exp(ld)
    mag = jnp.exp(k * (ar * dt))
    ang = k * (ai * dt)
    return mag * jnp.cos(ang), mag * jnp.sin(ang)


def _zoh_coef(ar, ai, ld):
    er, ei = _cpow(ar, ai, ld, 1.0)
    den = ar * ar + ai * ai
    return ((er - 1.0) * ar + ei * ai) / den, (ei * ar - (er - 1.0) * ai) / den


def _dot_x3(a, b):
    a_hi, a_mid, _ = _split3(a)
    b_hi, b_mid, _ = _split3(b)
    return _dot(a_hi, b_hi) + _dot(a_hi, b_mid) + _dot(a_mid, b_hi)


def _tap_table(lag, col, cf):
    fwd = lag >= 0
    k = jnp.abs(lag).astype(F32)
    sel = lambda a, b: jnp.where(fwd, a, b)
    pre, pim = _cpow(sel(col(0), col(3)), sel(col(1), col(4)), sel(col(2), col(5)), k)
    cre, cim = sel(cf[0][0], cf[1][0]), sel(cf[0][1], cf[1][1])
    both = lag == 0
    return (pre * cre - pim * cim + jnp.where(both, cf[1][0], 0.0),
            pre * cim + pim * cre + jnp.where(both, cf[1][1], 0.0))


def _s5_ops_kernel(cols_ref, rows_ref, btre_ref, btim_ref, bdre_ref, bdim_ref, cre_ref, cim_ref,
                   ctre_ref, ctim_ref, kb_ref, wloc_ref, wout_ref, ar_ref,
                   mt_ref, pt_ref, pw_ref, v_ref):
    L, H, N = S5_CHUNK, S5_HALF, S5_N
    lane = lax.broadcasted_iota(jnp.int32, (1, 2 * L), 1)
    lo = lane < L

    def tables():
        col = lambda k: cols_ref[0, :, k:k + 1]
        row = lambda k: rows_ref[0, k:k + 1, :]
        cf = [_zoh_coef(col(3 * d), col(3 * d + 1), col(3 * d + 2)) for d in range(2)]
        mt_ref[0], mt_ref[1] = _tap_table(jnp.where(lo, lane, lane - 2 * L), col, cf)
        mt_ref[2], mt_ref[3] = _tap_table(lane - L, col, cf)
        tt = lane % L
        ss = lax.broadcasted_iota(jnp.int32, (L, 1), 0)
        for d in range(2):
            kk = (tt + 1 if d == 0 else L - tt).astype(F32)
            pt_ref[d, 0], pt_ref[d, 1] = _cpow(col(3 * d), col(3 * d + 1), col(3 * d + 2), kk)
            ks = (L - 1 - ss if d == 0 else ss).astype(F32)
            qre, qim = _cpow(row(3 * d), row(3 * d + 1), row(3 * d + 2), ks)
            fre, fim = _zoh_coef(row(3 * d), row(3 * d + 1), row(3 * d + 2))
            pw_ref[d, 0] = qre * fre - qim * fim
            pw_ref[d, 1] = qre * fim + qim * fre
            ar_ref[0, 2 * d:2 * d + 1, :], ar_ref[0, 2 * d + 1:2 * d + 2, :] = _cpow(
                row(3 * d), row(3 * d + 1), row(3 * d + 2), float(L))
        ar_ref[0, 4:8, :] = jnp.zeros((4, 2 * N), F32)

    def lane_tile(i):
        tile = slice(i * 2 * L, (i + 1) * 2 * L)
        for half in range(2):
            p = i + half * H
            cre = cre_ref[0, p:p + 1, :]
            cim = cim_ref[0, p:p + 1, :]
            g_re = cre * btre_ref[0] - cim * btim_ref[0]
            g_im = cre * btim_ref[0] + cim * btre_ref[0]
            v_ref[i, half] = _dot_x3(g_re, mt_ref[2 * half]) - _dot_x3(g_im, mt_ref[2 * half + 1])
        for q in range(S5_P):
            r0 = (q % H) * 2 * L + (q // H) * L
            toe = [pltpu.roll(jnp.broadcast_to(v_ref[i, half, q:q + 1, :], (L, 2 * L)), 0, 1,
                              stride=1, stride_axis=0) for half in range(2)]
            kb_ref[0, r0:r0 + L, tile] = jnp.where(lo, toe[0], toe[1]).astype(BF16)
        ccre = jnp.where(lo, ctre_ref[0, :, i:i + 1], ctre_ref[0, :, i + H:i + H + 1])
        ccim = jnp.where(lo, ctim_ref[0, :, i:i + 1], ctim_ref[0, :, i + H:i + H + 1])
        pad = jnp.zeros((N, 2 * L), BF16)
        for d in range(2):
            pre, pim = pt_ref[d, 0], pt_ref[d, 1]
            base = d * 4 * N
            wout_ref[0, base:base + N, tile] = (ccre * pre - ccim * pim).astype(BF16)
            wout_ref[0, base + N:base + 2 * N, tile] = pad
            wout_ref[0, base + 2 * N:base + 3 * N, tile] = (-(ccre * pim + ccim * pre)).astype(BF16)
            wout_ref[0, base + 3 * N:base + 4 * N, tile] = pad
        for half in range(2):
            q = i + half * H
            bre = bdre_ref[0, q:q + 1, :]
            bim = bdim_ref[0, q:q + 1, :]
            rows = slice(i * 2 * L + half * L, i * 2 * L + (half + 1) * L)
            for d in range(2):
                base = d * 4 * N
                re = pw_ref[d, 0] * bre - pw_ref[d, 1] * bim
                im = pw_ref[d, 0] * bim + pw_ref[d, 1] * bre
                wloc_ref[0, rows, base:base + 2 * N] = jnp.where(lo, re, 0.0).astype(BF16)
                wloc_ref[0, rows, base + 2 * N:base + 4 * N] = jnp.where(lo, im, 0.0).astype(BF16)

    tables()
    for i in range(H):
        lane_tile(i)


def _s5_operators(a_re, a_im, log_dt, b_re, b_im, c_re, c_im):
    L, H, n2 = S5_CHUNK, S5_HALF, 2 * S5_N
    g = a_re.shape[0] * S5_GROUPS
    by_dir = lambda a: jnp.moveaxis(a, 1, 0).reshape((2, g) + a.shape[3:])
    a_re, a_im, log_dt = by_dir(a_re), by_dir(a_im), by_dir(log_dt)
    b_re, b_im, c_re, c_im = (a.reshape((g,) + a.shape[2:]) for a in (b_re, b_im, c_re, c_im))
    ld = jnp.broadcast_to(log_dt[:, :, None], a_re.shape)
    per_dir = [a_re[0], a_im[0], ld[0], a_re[1], a_im[1], ld[1]]
    zero = jnp.zeros_like(a_re[0])
    cols = jnp.stack(per_dir + [zero, zero], axis=-1)
    dup = lambda v: jnp.concatenate([v, v], axis=-1)
    rows = jnp.stack([dup(v) for v in per_dir + [zero, zero]], axis=1)
    bt_re, bt_im = b_re.transpose(0, 2, 1), b_im.transpose(0, 2, 1)
    ct_re, ct_im = c_re.transpose(0, 2, 1), c_im.transpose(0, 2, 1)
    per_g = lambda shape: pl.BlockSpec((1,) + tuple(shape[1:]), lambda gi: (gi, 0, 0))
    ins = (cols, rows, bt_re, bt_im, dup(bt_re), dup(bt_im), c_re, c_im, ct_re, ct_im)
    return pl.pallas_call(
        _s5_ops_kernel,
        grid=(g,),
        in_specs=[per_g(a.shape) for a in ins],
        out_specs=(per_g((g, S5_FLAT, S5_FLAT)), per_g((g, S5_FLAT, S5_STATE)),
                   per_g((g, S5_STATE, S5_FLAT)), per_g((g, 8, n2))),
        out_shape=(jax.ShapeDtypeStruct((g, S5_FLAT, S5_FLAT), BF16),
                   jax.ShapeDtypeStruct((g, S5_FLAT, S5_STATE), BF16),
                   jax.ShapeDtypeStruct((g, S5_STATE, S5_FLAT), BF16),
                   jax.ShapeDtypeStruct((g, 8, n2), F32)),
        scratch_shapes=[pltpu.VMEM((4, S5_N, 2 * L), F32),
                        pltpu.VMEM((2, 2, S5_N, 2 * L), F32),
                        pltpu.VMEM((2, 2, L, n2), F32),
                        pltpu.VMEM((H, 2, S5_P, 2 * L), F32)],
        compiler_params=_params("arbitrary"),
        name="s5_operators",
    )(*ins)


def _s5_kernel(u_ref, kb_ref, wloc_ref, wout_ref, ar_ref, y_ref, sin_ref, *, bsz, n_ctx_chunks):
    n_chunks = u_ref.shape[1] // bsz
    n2 = 2 * S5_N
    u = u_ref[0]
    sloc = _dot(u, wloc_ref[0])
    ar = ar_ref[0]
    n_lat = n_chunks - n_ctx_chunks
    fwd_order = list(range(n_lat, n_chunks)) + list(range(n_lat))
    bwd_order = list(range(n_chunks - 1, n_lat - 1, -1)) + list(range(n_lat - 1, -1, -1))
    for d, order in ((0, fwd_order), (1, bwd_order)):
        a_re = ar[2 * d:2 * d + 1, :]
        a_im = ar[2 * d + 1:2 * d + 2, :]
        re_cols = slice(2 * d * n2, (2 * d + 1) * n2)
        im_cols = slice((2 * d + 1) * n2, (2 * d + 2) * n2)
        st_re = jnp.zeros((bsz, n2), F32)
        st_im = jnp.zeros((bsz, n2), F32)
        for c in order:
            rows = slice(c * bsz, (c + 1) * bsz)
            sin_ref[rows, re_cols] = st_re.astype(BF16)
            sin_ref[rows, im_cols] = st_im.astype(BF16)
            st_re, st_im = (a_re * st_re - a_im * st_im + sloc[rows, re_cols],
                            a_re * st_im + a_im * st_re + sloc[rows, im_cols])
    y_ref[0] = _dot(u, kb_ref[0]) + _dot(sin_ref[...], wout_ref[0])


def _s5_mix(ut, layer, kbig, w_loc, w_out, a_rows, bsz):
    g, rows, flat = ut.shape
    kern = functools.partial(_s5_kernel, bsz=bsz, n_ctx_chunks=CTX_LEN // S5_CHUNK)
    per_g = lambda a, off=layer * g: pl.BlockSpec((1,) + a.shape[1:], lambda gi: (off + gi, 0, 0))
    return pl.pallas_call(
        kern,
        grid=(g,),
        in_specs=[per_g(ut, 0)] + [per_g(a) for a in (kbig, w_loc, w_out, a_rows)],
        out_specs=pl.BlockSpec((1, rows, flat), lambda gi: (gi, 0, 0)),
        out_shape=jax.ShapeDtypeStruct((g, rows, flat), F32),
        scratch_shapes=[pltpu.VMEM((rows, S5_STATE), BF16)],
        compiler_params=_params("arbitrary"),
        name="s5_mix",
    )(ut, kbig, w_loc, w_out, a_rows)


def _s5_readout(y5_ref, zd_ref, d_ref, gw_ref, gb_ref, yd_ref):
    lo = lax.broadcasted_iota(jnp.int32, (1, 2 * S5_CHUNK), 1) < S5_CHUNK
    half_rows = (S5_GROUPS, 1, S5_HALF, 2 * S5_CHUNK)
    for j in range(zd_ref.shape[1] // (2 * S5_CHUNK)):
        rows = slice(j * 2 * S5_CHUNK, (j + 1) * 2 * S5_CHUNK)
        a = y5_ref[:, 2 * j, 0, :, :].reshape(S5_GROUPS * S5_HALF, 2 * S5_CHUNK)
        b = y5_ref[:, 2 * j + 1, 0, :, :].reshape(S5_GROUPS * S5_HALF, 2 * S5_CHUNK)
        first = jnp.where(lo, a, pltpu.roll(b, S5_CHUNK, 1))
        second = jnp.where(lo, pltpu.roll(a, S5_CHUNK, 1), b)
        yt = jnp.concatenate([first.reshape(half_rows), second.reshape(half_rows)],
                             axis=1).reshape(W_MIX, 2 * S5_CHUNK)
        y = _gelu(yt.T + d_ref[...] * zd_ref[0, rows, :])
        gate = _sigmoid(_dot(y.astype(BF16), gw_ref[...]) + gb_ref[...])
        yd_ref[rows, :] = (y * gate).astype(yd_ref.dtype)


def _out_ffn_kernel(x_ref, ya_ref, yb_ref, yc_ref, y5_ref, zd_ref, modb_ref, modc_ref, d5_ref,
                    gw_ref, gb_ref, wo_ref, l1w_ref, l1b_ref, wi_ref, wd_ref, l2w_ref, l2b_ref,
                    o_ref, yd_ref, x1_ref, h2_ref, a_ref, *, ctx_rows):
    tm = x_ref.shape[1]
    n_lat = tm - ctx_rows
    segs = [(0, n_lat, 0)] + ([(n_lat, tm, 1)] if ctx_rows else [])
    _s5_readout(y5_ref, zd_ref, d5_ref, gw_ref, gb_ref, yd_ref)
    y = (_dot(ya_ref[0], wo_ref[0, 0:W_MIX, :]) + _dot(yb_ref[0], wo_ref[0, W_MIX:2 * W_MIX, :]) +
         _dot(yc_ref[0], wo_ref[0, 2 * W_MIX:3 * W_MIX, :]) +
         _dot(yd_ref[...], wo_ref[0, 3 * W_MIX:4 * W_MIX, :]))
    for r0, r1, which in segs:
        g1 = _tile_mod(modb_ref, modc_ref, 2)[which]
        sh2 = _tile_mod(modb_ref, modc_ref, 3)[which]
        sc2 = _tile_mod(modb_ref, modc_ref, 4)[which]
        x1 = _layer_norm(ALPHA * x_ref[0, r0:r1, :] + g1 * y[r0:r1, :], l1w_ref[...], l1b_ref[...])
        x1_ref[r0:r1, :] = x1
        h2_ref[r0:r1, :] = (x1 * (1.0 + sc2) + sh2).astype(BF16)
    h2 = h2_ref[...]
    for kk in range(D_FF // FF_CHUNK):
        cols = slice(kk * FF_CHUNK, (kk + 1) * FF_CHUNK)
        gk = _dot(h2, wi_ref[0, :, cols])
        uk = _dot(h2, wi_ref[0, :, D_FF + kk * FF_CHUNK:D_FF + (kk + 1) * FF_CHUNK])
        a_ref[:, cols] = (_silu(gk) * uk).astype(BF16)
    f = _dot(a_ref[...], wd_ref[0])
    for r0, r1, which in segs:
        g2 = _tile_mod(modb_ref, modc_ref, 5)[which]
        o_ref[0, r0:r1, :] = _layer_norm(ALPHA * x1_ref[r0:r1, :] + g2 * f[r0:r1, :],
                                         l2w_ref[...], l2b_ref[...])


def _out_ffn(xcat, ya, yb, yc, y5, zd, mod, layer, s5_d, glu_w, glu_b, w_out, ln1_w, ln1_b, w_ffn_in,
             w_ffn_out, ln2_w, ln2_b, tm, n_out):
    bsz, s, d = xcat.shape
    ctx_rows = CTX_LEN if n_out == s else 0
    tok = lambda w: pl.BlockSpec((1, tm, w), lambda b, j: (b, j, 0))
    row = lambda a: a.reshape(1, -1)
    return pl.pallas_call(
        functools.partial(_out_ffn_kernel, ctx_rows=ctx_rows),
        grid=(bsz, n_out // tm),
        in_specs=[tok(d), tok(W_MIX), tok(W_MIX), tok(W_MIX),
                  pl.BlockSpec((S5_GROUPS, tm // S5_CHUNK, 1, S5_HALF, 2 * S5_CHUNK),
                               lambda b, j: (0, j, b, 0, 0)),
                  tok(W_MIX),
                  pl.BlockSpec((1, 6, d), lambda b, j: (b, 0, 0)),
                  pl.BlockSpec((1, 6, d), lambda b, j: (bsz, 0, 0)),
                  _const_spec((1, W_MIX)), _const_spec((W_MIX, W_MIX)), _const_spec((1, W_MIX)),
                  _layer_spec(w_out, layer), _const_spec((1, d)), _const_spec((1, d)),
                  _layer_spec(w_ffn_in, layer), _layer_spec(w_ffn_out, layer),
                  _const_spec((1, d)), _const_spec((1, d))],
        out_specs=tok(d),
        out_shape=jax.ShapeDtypeStruct((bsz, n_out, d), F32),
        scratch_shapes=[pltpu.VMEM((tm, W_MIX), BF16), pltpu.VMEM((tm, d), F32),
                        pltpu.VMEM((tm, d), BF16), pltpu.VMEM((tm, D_FF), BF16)],
        compiler_params=_params("arbitrary", "arbitrary"),
        name="out_ffn",
    )(xcat, ya, yb, yc, y5, zd, mod, mod, row(s5_d), glu_w.astype(BF16), row(glu_b), w_out,
      row(ln1_w), row(ln1_b), w_ffn_in, w_ffn_out, row(ln2_w), row(ln2_b))


_S5_LOCAL = np.arange(S5_P).reshape(S5_HALF, 2).T.reshape(-1)
_S5_ORDER = (np.arange(S5_GROUPS)[:, None] * S5_P + _S5_LOCAL[None, :]).reshape(-1)


def _reorder_w_in(w):
    off_b = 4 * W_MIX + N_GATES
    off_d = w.shape[-1] - W_MIX
    w_r = jnp.concatenate([w[..., :4 * W_MIX], w[..., off_b:off_d], w[..., off_d:][..., _S5_ORDER]], axis=-1)
    return w_r.astype(BF16), jnp.swapaxes(w[..., 4 * W_MIX:off_b], -1, -2).astype(BF16)


def kernel(x, c, ctx, c_ctx, w_mod, b_mod, w_in, mlstm_gate_bias, mlstm_norm_w, sgu_ln_w, sgu_ln_b, sgu_w, sgu_b, conv_w, conv_b, conv_ln_w, conv_ln_b, s5_a_re, s5_a_im, s5_log_dt, s5_b_re, s5_b_im, s5_c_re, s5_c_im, s5_d, s5_glu_w, s5_glu_b, w_out, ln1_w, ln1_b, w_ffn_in, w_ffn_out, ln2_w, ln2_b):
    bsz, seq, d = x.shape
    s = seq + CTX_LEN
    assert seq == SEQ_LATENT and ctx.shape[1] == CTX_LEN
    mod_rows = -(-(bsz + 1) // 8) * 8
    cvec = jnp.zeros((mod_rows, d), F32).at[:bsz].set(c).at[bsz].set(c_ctx)
    mod_all = _modulation(cvec, w_mod, b_mod).reshape(DEPTH, mod_rows, 6, d)
    nc5 = s // S5_CHUNK
    w_r, wg_t = _reorder_w_in(w_in)
    w_out_b = jnp.concatenate([w_out[:, :3 * W_MIX], w_out[:, 3 * W_MIX:][:, _S5_ORDER]], axis=1).astype(BF16)
    w_ffn_in_b, w_ffn_out_b = w_ffn_in.astype(BF16), w_ffn_out.astype(BF16)
    kbig, w_loc, w_so, a_rows = _s5_operators(
        s5_a_re, s5_a_im, s5_log_dt, s5_b_re[..., _S5_LOCAL], s5_b_im[..., _S5_LOCAL],
        s5_c_re[:, :, _S5_LOCAL, :], s5_c_im[:, :, _S5_LOCAL, :])
    for l in range(DEPTH):
        mod = mod_all[l]
        src = (x, ctx, 0, True) if l == 0 else (xcat, xcat, seq // TILE, False)
        outs = _in_projection(*src, mod, l, w_r, wg_t, sgu_w[l], sgu_b[l], sgu_ln_w[l], sgu_ln_b[l],
                              conv_w[l], conv_b[l], conv_ln_w[l], conv_ln_b[l])
        if l == 0:
            xcat, outs = outs[0], outs[1:]
        k, qt, vt, ot, yb, yc, zd, ut, gt = outs
        ya = _mlstm(k, qt, vt, ot, gt, mlstm_gate_bias[l], mlstm_norm_w[l])
        y5 = _s5_mix(ut.reshape(S5_GROUPS, nc5 * bsz, S5_FLAT), l, kbig, w_loc, w_so, a_rows, bsz)
        y5 = y5.reshape(S5_GROUPS, nc5, bsz, S5_HALF, 2 * S5_CHUNK)
        last = l == DEPTH - 1
        xcat = _out_ffn(xcat, ya, yb, yc, y5, zd, mod, l, s5_d[l][_S5_ORDER],
                        s5_glu_w[l][_S5_ORDER][:, _S5_ORDER], s5_glu_b[l][_S5_ORDER],
                        w_out_b, ln1_w[l], ln1_b[l], w_ffn_in_b, w_ffn_out_b, ln2_w[l], ln2_b[l],
                        tm=TM_LATENT if last else TM_MAIN, n_out=seq if last else s)
    return xcat
```

```python
import functools

import numpy as np
import jax
import jax.numpy as jnp
from jax import lax
from jax.experimental import pallas as pl
from jax.experimental.pallas import tpu as pltpu

F32 = jnp.float32
BF16 = jnp.bfloat16

DEPTH = 2
CTX_LEN = 256
SEQ_LATENT = 2048
GRID_W = 64
W_MIX = 256
HEADS = 4
HEAD_DIM = W_MIX // HEADS
SGU_GROUPS = 4
SGU_GD = W_MIX // SGU_GROUPS
SGU_CHUNK = 128
CONV_WIDTH = 31
CONV_PAD = CONV_WIDTH // 2
CONV_HALO = 32
S5_P = 16
S5_GROUPS = W_MIX // S5_P
S5_N = 64
S5_CHUNK = 64
S5_HALF = S5_P // 2
S5_FLAT = S5_P * S5_CHUNK
S5_STATE = 8 * S5_N
D_FF = 2816
FF_CHUNK = 256
ALPHA = (2 * DEPTH) ** 0.25
LN_EPS = 1e-5
TILE = 256
MLSTM_CHUNK = 256
TM_MAIN = 3 * TILE
TM_LATENT = 2 * TILE
CONV_ROWS = max((TILE // GRID_W) * (GRID_W + CONV_HALO), CTX_LEN + CONV_HALO)
N_GATES = 4 * HEADS
GATE_PAD = 128
C_O = 3 * W_MIX
C_B = 4 * W_MIX
C_C = C_B + 2 * W_MIX
C_D = C_C + 2 * W_MIX
C_G = C_D + W_MIX
VMEM_LIMIT = 56 * 1024 * 1024


def _dot(a, b):
    return jnp.dot(a, b, preferred_element_type=F32)


def _dot_nt(a, b):
    return lax.dot_general(a, b, (((1,), (1,)), ((), ())), preferred_element_type=F32)


def _split3(x):
    hi = x.astype(BF16)
    r = x - hi.astype(F32)
    mid = r.astype(BF16)
    lo = (r - mid.astype(F32)).astype(BF16)
    return hi, mid, lo


def _layer_norm(x, w, b):
    mu = jnp.mean(x, axis=-1, keepdims=True)
    d = x - mu
    var = jnp.mean(d * d, axis=-1, keepdims=True)
    return d * lax.rsqrt(var + LN_EPS) * w + b


def _sigmoid(x):
    return jax.nn.sigmoid(x)


def _silu(x):
    return x * jax.nn.sigmoid(x)


def _gelu(x):
    return jax.nn.gelu(x, approximate=True)


def _log_sigmoid(x):
    return jnp.minimum(x, 0.0) - jnp.log1p(jnp.exp(-jnp.abs(x)))


def _params(*sem):
    return pltpu.CompilerParams(dimension_semantics=sem, vmem_limit_bytes=VMEM_LIMIT)


def _const_spec(shape):
    nd = len(shape)
    return pl.BlockSpec(shape, lambda *_: (0,) * nd, pipeline_mode=pl.Buffered(1))


def _layer_spec(stacked, l):
    nd = stacked.ndim
    return pl.BlockSpec((1,) + stacked.shape[1:], lambda *_: (l,) + (0,) * (nd - 1),
                        pipeline_mode=pl.Buffered(1))


def _mod_kernel(c_ref, w_ref, b_ref, o_ref):
    s = _silu(c_ref[...])
    s_hi = s.astype(BF16)
    s_lo = (s - s_hi.astype(F32)).astype(BF16)
    w = w_ref[0]
    w_hi = w.astype(BF16)
    w_lo = (w - w_hi.astype(F32)).astype(BF16)
    o_ref[0] = _dot(s_hi, w_hi) + _dot(s_hi, w_lo) + _dot(s_lo, w_hi) + b_ref[0]


def _modulation(cvec, w_mod, b_mod):
    depth, d, n = w_mod.shape
    rows = cvec.shape[0]
    tn = 1024
    return pl.pallas_call(
        _mod_kernel,
        grid=(depth, n // tn),
        in_specs=[pl.BlockSpec((rows, d), lambda l, j: (0, 0)),
                  pl.BlockSpec((1, d, tn), lambda l, j: (l, 0, j)),
                  pl.BlockSpec((1, 1, tn), lambda l, j: (l, 0, j))],
        out_specs=pl.BlockSpec((1, rows, tn), lambda l, j: (l, 0, j)),
        out_shape=jax.ShapeDtypeStruct((depth, rows, n), F32),
        compiler_params=_params("arbitrary", "arbitrary"),
        name="modulation",
    )(cvec, w_mod, b_mod.reshape(depth, 1, n))


def _tile_mod(modb_ref, modc_ref, i):
    last = pl.program_id(1) == pl.num_programs(1) - 1
    row_b = modb_ref[0, i:i + 1, :]
    return row_b, jnp.where(last, modc_ref[0, i:i + 1, :], row_b)


def _sgu_slab(zb, sgw_ref, sgb_ref, slnw_ref, slnb_ref):
    u = _gelu(zb[:, 0:W_MIX])
    vn = _layer_norm(_gelu(zb[:, W_MIX:2 * W_MIX]), slnw_ref[...], slnb_ref[...]).astype(BF16)
    lane_grp = lax.broadcasted_iota(jnp.int32, (1, W_MIX), 1) // SGU_GD
    out = []
    for cl in range(TILE // SGU_CHUNK):
        vc = vn[cl * SGU_CHUNK:(cl + 1) * SGU_CHUNK, :]
        acc = sgb_ref[...]
        for gi in range(SGU_GROUPS):
            acc = acc + _dot(sgw_ref[gi], jnp.where(lane_grp == gi, vc, jnp.zeros_like(vc)))
        out.append(acc)
    return u * jnp.concatenate(out, axis=0)


def _conv_slab(a, seg_len, cw_ref, sh_ref):
    n_seg = TILE // seg_len
    stride = seg_len + CONV_HALO
    zeros16 = jnp.zeros((16, W_MIX), F32)
    for sg in range(n_seg):
        base = sg * stride
        seg = a[sg * seg_len:(sg + 1) * seg_len, :]
        for r in range(8):
            sh_ref[r, pl.ds(base, 16), :] = zeros16
            sh_ref[r, pl.ds(base + seg_len + 8, 16), :] = zeros16
            sh_ref[r, pl.ds(base + 16 - r, seg_len), :] = seg
    out = []
    for sg in range(n_seg):
        base = sg * stride
        acc = jnp.zeros((seg_len, W_MIX), F32)
        for t in range(CONV_WIDTH):
            off = t - CONV_PAD
            r = off % 8
            acc = acc + cw_ref[t:t + 1, :] * sh_ref[r, pl.ds(base + (off - r) + 16, seg_len), :]
        out.append(acc)
    return out[0] if n_seg == 1 else jnp.concatenate(out, axis=0)


def _emit_s5_rows(zdt, ut_ref, chunk0):
    n_tok = zdt.shape[1]
    lo = lax.broadcasted_iota(jnp.int32, (1, 2 * S5_CHUNK), 1) < S5_CHUNK
    halves = zdt.reshape(S5_GROUPS, 2, S5_HALF, n_tok)
    first = halves[:, 0].reshape(S5_GROUPS * S5_HALF, n_tok)
    second = halves[:, 1].reshape(S5_GROUPS * S5_HALF, n_tok)
    for j in range(n_tok // (2 * S5_CHUNK)):
        a = first[:, j * 2 * S5_CHUNK:(j + 1) * 2 * S5_CHUNK]
        b = second[:, j * 2 * S5_CHUNK:(j + 1) * 2 * S5_CHUNK]
        pieces = (jnp.where(lo, a, pltpu.roll(b, S5_CHUNK, 1)),
                  jnp.where(lo, pltpu.roll(a, S5_CHUNK, 1), b))
        for cc, piece in enumerate(pieces):
            ut_ref[:, chunk0 + 2 * j + cc, 0, :, :] = (
                piece.reshape(S5_GROUPS, S5_HALF, 2 * S5_CHUNK).astype(BF16))


def _inproj_kernel(*refs, n_slabs, emit_stream):
    x_refs = refs[:n_slabs]
    (ctx_ref, modb_ref, modc_ref, w_ref, wgt_ref, sgw_ref, sgb_ref, slnw_ref, slnb_ref,
     cw_ref, cb_ref, clnw_ref, clnb_ref) = refs[n_slabs:n_slabs + 13]
    outs = refs[n_slabs + 13:]
    if emit_stream:
        xo_ref, outs = outs[0], outs[1:]
    k_ref, qt_ref, vt_ref, ot_ref, yb_ref, yc_ref, zd_ref, ut_ref, gt_ref, h_ref, sh_ref = outs
    last = pl.program_id(1) == pl.num_programs(1) - 1
    sh_b, sh_c = _tile_mod(modb_ref, modc_ref, 0)
    sc_b, sc_c = _tile_mod(modb_ref, modc_ref, 1)
    for r in range(n_slabs):
        rows = slice(r * TILE, (r + 1) * TILE)
        x = x_refs[r][0]
        sc, sh = sc_b, sh_b
        if r == n_slabs - 1:
            x = jnp.where(last, ctx_ref[0], x)
            sc, sh = sc_c, sh_c
        if emit_stream:
            xo_ref[0, rows, :] = x
        h_ref[rows, :] = (x * (1.0 + sc) + sh).astype(BF16)
    for r in range(n_slabs):
        rows = slice(r * TILE, (r + 1) * TILE)
        h = h_ref[rows, :]
        qt_ref[0, :, rows] = _dot(h, w_ref[0, :, 0:W_MIX]).T.astype(BF16)
        k_ref[0, rows, :] = (_dot(h, w_ref[0, :, W_MIX:2 * W_MIX]) * (HEAD_DIM ** -0.5)).astype(BF16)
        vt_ref[0, :, rows] = _dot(h, w_ref[0, :, 2 * W_MIX:3 * W_MIX]).T.astype(BF16)
        ot_ref[0, :, rows] = _dot(h, w_ref[0, :, C_O:C_B]).T
        zd = _dot(h, w_ref[0, :, C_D:C_G])
        zd_ref[0, rows, :] = zd
        _emit_s5_rows(zd.T, ut_ref, r * (TILE // S5_CHUNK))
        gt_ref[0, :, rows] = _dot_nt(wgt_ref[0], h)
        yb = _sgu_slab(_dot(h, w_ref[0, :, C_B:C_C]), sgw_ref, sgb_ref, slnw_ref, slnb_ref)
        yb_ref[0, rows, :] = yb.astype(yb_ref.dtype)
        zc = _dot(h, w_ref[0, :, C_C:C_D])
        a = zc[:, 0:W_MIX] * _sigmoid(zc[:, W_MIX:2 * W_MIX])

        def conv_out(seg_len, a=a, rows=rows, r=r):
            acc = _conv_slab(a, seg_len, cw_ref, sh_ref.at[r])
            yc = _silu(_layer_norm(acc + cb_ref[...], clnw_ref[...], clnb_ref[...]))
            yc_ref[0, rows, :] = yc.astype(yc_ref.dtype)

        if r < n_slabs - 1:
            conv_out(GRID_W)
        else:
            pl.when(jnp.logical_not(last))(functools.partial(conv_out, GRID_W))
            pl.when(last)(functools.partial(conv_out, CTX_LEN))


def _in_projection(lat, ctx, ctx_block, emit_stream, mod, layer, w_r, wg_t, sgu_w, sgu_b, sgu_ln_w,
                   sgu_ln_b, conv_w, conv_b, conv_ln_w, conv_ln_b):
    bsz, _, d = lat.shape
    s = SEQ_LATENT + CTX_LEN
    tm = TM_MAIN
    n_slabs = tm // TILE
    last_lat = SEQ_LATENT // TILE - 1
    slab = lambda r: pl.BlockSpec((1, TILE, d), lambda b, j: (b, jnp.minimum(n_slabs * j + r, last_lat), 0))
    nc5 = s // S5_CHUNK
    cpt = tm // S5_CHUNK
    bias_full = jnp.repeat(sgu_b.T, SGU_GD, axis=1)
    row = lambda a: a.reshape(1, W_MIX)
    tok = lambda w: pl.BlockSpec((1, tm, w), lambda b, j: (b, j, 0))
    chan = lambda c: pl.BlockSpec((1, c, tm), lambda b, j: (b, 0, j))
    out_shape = (
        jax.ShapeDtypeStruct((bsz, s, W_MIX), BF16),
        jax.ShapeDtypeStruct((bsz, W_MIX, s), BF16),
        jax.ShapeDtypeStruct((bsz, W_MIX, s), BF16),
        jax.ShapeDtypeStruct((bsz, W_MIX, s), F32),
        jax.ShapeDtypeStruct((bsz, s, W_MIX), BF16),
        jax.ShapeDtypeStruct((bsz, s, W_MIX), BF16),
        jax.ShapeDtypeStruct((bsz, s, W_MIX), F32),
        jax.ShapeDtypeStruct((S5_GROUPS, nc5, bsz, S5_HALF, 2 * S5_CHUNK), BF16),
        jax.ShapeDtypeStruct((bsz, N_GATES, s), F32),
    )
    out_specs = (
        tok(W_MIX), chan(W_MIX), chan(W_MIX), chan(W_MIX), tok(W_MIX), tok(W_MIX), tok(W_MIX),
        pl.BlockSpec((S5_GROUPS, cpt, 1, S5_HALF, 2 * S5_CHUNK), lambda b, j: (0, j, b, 0, 0)),
        chan(N_GATES),
    )
    if emit_stream:
        out_shape = (jax.ShapeDtypeStruct((bsz, s, d), F32),) + out_shape
        out_specs = (tok(d),) + out_specs
    return pl.pallas_call(
        functools.partial(_inproj_kernel, n_slabs=n_slabs, emit_stream=emit_stream),
        grid=(bsz, s // tm),
        in_specs=[slab(r) for r in range(n_slabs)] + [
                  pl.BlockSpec((1, TILE, d), lambda b, j: (b, ctx_block, 0)),
                  pl.BlockSpec((1, 6, d), lambda b, j: (b, 0, 0)),
                  pl.BlockSpec((1, 6, d), lambda b, j: (bsz, 0, 0)),
                  _layer_spec(w_r, layer),
                  _layer_spec(wg_t, layer),
                  _const_spec((SGU_GROUPS, SGU_CHUNK, SGU_CHUNK)),
                  _const_spec((SGU_CHUNK, W_MIX)),
                  _const_spec((1, W_MIX)), _const_spec((1, W_MIX)),
                  _const_spec((CONV_WIDTH, W_MIX)), _const_spec((1, W_MIX)),
                  _const_spec((1, W_MIX)), _const_spec((1, W_MIX))],
        out_specs=out_specs,
        out_shape=out_shape,
        scratch_shapes=[pltpu.VMEM((tm, d), BF16),
                        pltpu.VMEM((tm // TILE, 8, CONV_ROWS, W_MIX), F32)],
        compiler_params=_params("arbitrary", "arbitrary"),
        name="in_projection",
    )(*([lat] * n_slabs), ctx, mod, mod, w_r, wg_t, sgu_w.astype(BF16), bias_full, row(sgu_ln_w),
      row(sgu_ln_b), conv_w, row(conv_b), row(conv_ln_w), row(conv_ln_b))


def _mlstm_kernel(k_ref, qt_ref, vt_ref, ot_ref, gt_ref, gbt_ref, nw_ref, y_ref,
                  ht_ref, src_ref, lgt_ref, cumt_ref, pm_ref, c_ref, n_ref, m_ref):
    L = MLSTM_CHUNK
    n_chunks = k_ref.shape[1] // L
    row = lax.broadcasted_iota(jnp.int32, (L, L), 0)
    col = lax.broadcasted_iota(jnp.int32, (L, L), 1)
    tri = (col <= row).astype(BF16)
    lane_g = lax.broadcasted_iota(jnp.int32, (1, GATE_PAD), 1)
    sub_g = lax.broadcasted_iota(jnp.int32, (N_GATES, 1), 0)
    is_f_sub = (sub_g % 8) >= HEADS
    lane_w = lax.broadcasted_iota(jnp.int32, (1, W_MIX), 1) // HEAD_DIM
    sub_w = lax.broadcasted_iota(jnp.int32, (W_MIX, 1), 0) // HEAD_DIM
    bd_mask = sub_w == lane_w
    n_mask = sub_g == lane_w
    row4 = lax.broadcasted_iota(jnp.int32, (HEADS * L, L), 0) % L
    col4 = lax.broadcasted_iota(jnp.int32, (HEADS * L, L), 1)
    mask4 = (row4 <= col4, row4 >= col4)

    for c in range(n_chunks):
        cols = slice(c * L, (c + 1) * L)
        grawt = gt_ref[0, :, cols] + gbt_ref[...]
        lgt = jnp.where(is_f_sub, _log_sigmoid(grawt), grawt)
        lgt_ref[:, cols] = lgt
        hi, mid, lo = _split3(lgt)
        pret = _dot_nt(hi, tri) + _dot_nt(mid, tri) + _dot_nt(lo, tri)
        cumt = jnp.where(sub_g >= 8, pret[:, L - 1:L] - pret + lgt, pret)
        cumt_ref[:, cols] = cumt
        srct = jnp.where(is_f_sub, -jnp.inf, lgt - pltpu.roll(cumt, N_GATES - HEADS, 0))
        pm_ref[:, cols] = srct
        padded = jnp.concatenate([srct, jnp.zeros((GATE_PAD - N_GATES, L), F32)], axis=0)
        src_ref[cols, :] = padded.T
    pos = lax.broadcasted_iota(jnp.int32, (1, pm_ref.shape[1]), 1) % L
    pm = pm_ref[...]
    sh = 1
    while sh < L:
        fwd_v = jnp.where(pos >= sh, pltpu.roll(pm, sh, 1), -jnp.inf)
        bwd_v = jnp.where(pos < L - sh, pltpu.roll(pm, pm.shape[1] - sh, 1), -jnp.inf)
        pm = jnp.maximum(pm, jnp.where(sub_g >= 8, bwd_v, fwd_v))
        sh *= 2
    pm_ref[...] = pm

    def chunk(c, d):
        r0 = pl.multiple_of(c * L, L)
        k = k_ref[0, pl.ds(r0, L), :]
        qt = qt_ref[0, :, pl.ds(r0, L)]
        vt = vt_ref[0, :, pl.ds(r0, L)]
        src_cols = src_ref[pl.ds(r0, L), :]
        lgt = lgt_ref[:, pl.ds(r0, L)]
        cumt = cumt_ref[:, pl.ds(r0, L)]
        pm = pm_ref[:, pl.ds(r0, L)]
        tot = cumt[:, L - 1:L] if d == 0 else cumt[:, 0:1]
        m_vec = m_ref[d]
        c_old = c_ref[d]
        n_old = n_ref[d]
        qc = _dot(c_old.astype(BF16), qt)
        qn = _dot(n_old.astype(BF16), qt)
        e_slabs = []
        e_small = jnp.zeros((N_GATES, L), F32)
        decay_row = jnp.zeros((1, W_MIX), F32)
        m_new_vec = m_vec
        srcs, col_terms, khs, w_inters, floors = [], [], [], [], []
        for h in range(HEADS):
            ic = d * 8 + h
            fc = d * 8 + HEADS + h
            m_h = m_vec[:, fc:fc + 1]
            cct = cumt[fc:fc + 1, :]
            inter = cct + m_h
            m_t = jnp.maximum(inter, cct + pm[ic:ic + 1, :])
            w_inters.append(jnp.exp(inter - m_t))
            floors.append(jnp.exp(-m_t))
            col_terms.append(jnp.broadcast_to(cct - m_t, (L, L)))
            srcs.append(src_cols[:, ic:ic + 1])
            khs.append(jnp.where(lane_w == h, k, jnp.zeros_like(k)))
        arg = jnp.concatenate(srcs, axis=0) + jnp.concatenate(col_terms, axis=0)
        arg = jnp.where(mask4[d], arg, -jnp.inf)
        sc = _dot(jnp.concatenate(khs, axis=0), qt) * jnp.exp(arg)
        scb = sc.astype(BF16)
        for h in range(HEADS):
            ic = d * 8 + h
            fc = d * 8 + HEADS + h
            head_lanes = lane_w == h
            sl = slice(h * HEAD_DIM, (h + 1) * HEAD_DIM)
            hrows = slice(h * L, (h + 1) * L)
            m_h = m_vec[:, fc:fc + 1]
            cct = cumt[fc:fc + 1, :]
            den = jnp.sum(sc[hrows, :], axis=0, keepdims=True) + w_inters[h] * qn[h:h + 1, :]
            num = _dot(vt[sl, :], scb[hrows, :]) + w_inters[h] * qc[sl, :]
            ht_ref[d, sl, pl.ds(r0, L)] = num / jnp.maximum(jnp.abs(den), floors[h])
            tot_h = tot[fc:fc + 1, :]
            logu = tot_h - cct + lgt[ic:ic + 1, :]
            m_new = jnp.maximum(tot_h + m_h, jnp.max(logu, axis=1, keepdims=True))
            e_h = jnp.exp(logu - m_new)
            e_slabs.append(jnp.broadcast_to(e_h, (HEAD_DIM, L)))
            e_small = jnp.where(sub_g == h, e_h, e_small)
            decay_row = jnp.where(head_lanes, jnp.exp(tot_h + m_h - m_new), decay_row)
            m_new_vec = jnp.where(lane_g == fc, m_new, m_new_vec)
        vet = (vt.astype(F32) * jnp.concatenate(e_slabs, axis=0)).astype(BF16)
        c_ref[d] = decay_row * c_old + jnp.where(bd_mask, _dot(vet, k), 0.0)
        n_ref[d] = decay_row * n_old + jnp.where(n_mask, _dot(e_small.astype(BF16), k), 0.0)
        m_ref[d] = m_new_vec

    c_ref[...] = jnp.zeros_like(c_ref)
    n_ref[...] = jnp.zeros_like(n_ref)
    m_ref[...] = jnp.zeros_like(m_ref)
    n_lat = n_chunks - CTX_LEN // L
    for j in range(n_chunks - n_lat):
        chunk(n_lat + j, 0)
        chunk(n_chunks - 1 - j, 1)

    def scan_body(i, carry):
        chunk(i, 0)
        chunk(n_lat - 1 - i, 1)
        return carry

    lax.fori_loop(0, n_lat, scan_body, 0)

    def readout(c, carry):
        r0 = pl.multiple_of(c * L, L)
        slabs = []
        for h in range(HEADS):
            sl = slice(h * HEAD_DIM, (h + 1) * HEAD_DIM)
            hs = ht_ref[0, sl, pl.ds(r0, L)] + ht_ref[1, sl, pl.ds(r0, L)]
            mu = jnp.mean(hs, axis=0, keepdims=True)
            dl = hs - mu
            var = jnp.mean(dl * dl, axis=0, keepdims=True)
            hn = dl * lax.rsqrt(var + LN_EPS) * nw_ref[sl, :]
            slabs.append(hn * _sigmoid(ot_ref[0, sl, pl.ds(r0, L)]))
        y_ref[0, pl.ds(r0, L), :] = jnp.concatenate(slabs, axis=0).T.astype(y_ref.dtype)
        return carry

    lax.fori_loop(0, n_chunks, readout, 0)


def _mlstm(k, qt, vt, ot, gt, gate_bias, norm_w):
    bsz, s, _ = k.shape
    gbt = gate_bias.reshape(N_GATES, 1)
    nw_full = jnp.broadcast_to(norm_w.reshape(W_MIX, 1), (W_MIX, MLSTM_CHUNK))
    tok = lambda w: pl.BlockSpec((1, s, w), lambda b: (b, 0, 0))
    chan = lambda c: pl.BlockSpec((1, c, s), lambda b: (b, 0, 0))
    return pl.pallas_call(
        _mlstm_kernel,
        grid=(bsz,),
        in_specs=[tok(W_MIX), chan(W_MIX), chan(W_MIX), chan(W_MIX), chan(N_GATES),
                  pl.BlockSpec((N_GATES, 1), lambda b: (0, 0)),
                  pl.BlockSpec((W_MIX, MLSTM_CHUNK), lambda b: (0, 0))],
        out_specs=tok(W_MIX),
        out_shape=jax.ShapeDtypeStruct((bsz, s, W_MIX), BF16),
        scratch_shapes=[pltpu.VMEM((2, W_MIX, s), F32),
                        pltpu.VMEM((s, GATE_PAD), F32),
                        pltpu.VMEM((N_GATES, s), F32),
                        pltpu.VMEM((N_GATES, s), F32),
                        pltpu.VMEM((N_GATES, s), F32),
                        pltpu.VMEM((2, W_MIX, W_MIX), F32),
                        pltpu.VMEM((2, N_GATES, W_MIX), F32),
                        pltpu.VMEM((2, 1, GATE_PAD), F32)],
        compiler_params=_params("arbitrary"),
        name="mlstm",
    )(k, qt, vt, ot, gt, gbt, nw_full)


def _cpow(ar, ai, ld, k):
    dt = jnp.exp(ld)
    mag = jnp.exp(k * (ar * dt))
    ang = k * (ai * dt)
    return mag * jnp.cos(ang), mag * jnp.sin(ang)


def _zoh_coef(ar, ai, ld):
    er, ei = _cpow(ar, ai, ld, 1.0)
    den = ar * ar + ai * ai
    return ((er - 1.0) * ar + ei * ai) / den, (ei * ar - (er - 1.0) * ai) / den


def _dot_x3(a, b):
    a_hi, a_mid, _ = _split3(a)
    b_hi, b_mid, _ = _split3(b)
    return _dot(a_hi, b_hi) + _dot(a_hi, b_mid) + _dot(a_mid, b_hi)


def _tap_table(lag, col, cf):
    fwd = lag >= 0
    k = jnp.abs(lag).astype(F32)
    sel = lambda a, b: jnp.where(fwd, a, b)
    pre, pim = _cpow(sel(col(0), col(3)), sel(col(1), col(4)), sel(col(2), col(5)), k)
    cre, cim = sel(cf[0][0], cf[1][0]), sel(cf[0][1], cf[1][1])
    both = lag == 0
    return (pre * cre - pim * cim + jnp.where(both, cf[1][0], 0.0),
            pre * cim + pim * cre + jnp.where(both, cf[1][1], 0.0))


def _s5_ops_kernel(cols_ref, rows_ref, btre_ref, btim_ref, bdre_ref, bdim_ref, cre_ref, cim_ref,
                   ctre_ref, ctim_ref, kb_ref, wloc_ref, wout_ref, ar_ref,
                   mt_ref, pt_ref, pw_ref, v_ref):
    L, H, N = S5_CHUNK, S5_HALF, S5_N
    lane = lax.broadcasted_iota(jnp.int32, (1, 2 * L), 1)
    lo = lane < L

    def tables():
        col = lambda k: cols_ref[0, :, k:k + 1]
        row = lambda k: rows_ref[0, k:k + 1, :]
        cf = [_zoh_coef(col(3 * d), col(3 * d + 1), col(3 * d + 2)) for d in range(2)]
        mt_ref[0], mt_ref[1] = _tap_table(jnp.where(lo, lane, lane - 2 * L), col, cf)
        mt_ref[2], mt_ref[3] = _tap_table(lane - L, col, cf)
        tt = lane % L
        ss = lax.broadcasted_iota(jnp.int32, (L, 1), 0)
        for d in range(2):
            kk = (tt + 1 if d == 0 else L - tt).astype(F32)
            pt_ref[d, 0], pt_ref[d, 1] = _cpow(col(3 * d), col(3 * d + 1), col(3 * d + 2), kk)
            ks = (L - 1 - ss if d == 0 else ss).astype(F32)
            qre, qim = _cpow(row(3 * d), row(3 * d + 1), row(3 * d + 2), ks)
            fre, fim = _zoh_coef(row(3 * d), row(3 * d + 1), row(3 * d + 2))
            pw_ref[d, 0] = qre * fre - qim * fim
            pw_ref[d, 1] = qre * fim + qim * fre
            ar_ref[0, 2 * d:2 * d + 1, :], ar_ref[0, 2 * d + 1:2 * d + 2, :] = _cpow(
                row(3 * d), row(3 * d + 1), row(3 * d + 2), float(L))
        ar_ref[0, 4:8, :] = jnp.zeros((4, 2 * N), F32)

    def lane_tile(i):
        tile = slice(i * 2 * L, (i + 1) * 2 * L)
        for half in range(2):
            p = i + half * H
            cre = cre_ref[0, p:p + 1, :]
            cim = cim_ref[0, p:p + 1, :]
            g_re = cre * btre_ref[0] - cim * btim_ref[0]
            g_im = cre * btim_ref[0] + cim * btre_ref[0]
            v_ref[i, half] = _dot_x3(g_re, mt_ref[2 * half]) - _dot_x3(g_im, mt_ref[2 * half + 1])
        for q in range(S5_P):
            r0 = (q % H) * 2 * L + (q // H) * L
            toe = [pltpu.roll(jnp.broadcast_to(v_ref[i, half, q:q + 1, :], (L, 2 * L)), 0, 1,
                              stride=1, stride_axis=0) for half in range(2)]
            kb_ref[0, r0:r0 + L, tile] = jnp.where(lo, toe[0], toe[1]).astype(BF16)
        ccre = jnp.where(lo, ctre_ref[0, :, i:i + 1], ctre_ref[0, :, i + H:i + H + 1])
        ccim = jnp.where(lo, ctim_ref[0, :, i:i + 1], ctim_ref[0, :, i + H:i + H + 1])
        pad = jnp.zeros((N, 2 * L), BF16)
        for d in range(2):
            pre, pim = pt_ref[d, 0], pt_ref[d, 1]
            base = d * 4 * N
            wout_ref[0, base:base + N, tile] = (ccre * pre - ccim * pim).astype(BF16)
            wout_ref[0, base + N:base + 2 * N, tile] = pad
            wout_ref[0, base + 2 * N:base + 3 * N, tile] = (-(ccre * pim + ccim * pre)).astype(BF16)
            wout_ref[0, base + 3 * N:base + 4 * N, tile] = pad
        for half in range(2):
            q = i + half * H
            bre = bdre_ref[0, q:q + 1, :]
            bim = bdim_ref[0, q:q + 1, :]
            rows = slice(i * 2 * L + half * L, i * 2 * L + (half + 1) * L)
            for d in range(2):
                base = d * 4 * N
                re = pw_ref[d, 0] * bre - pw_ref[d, 1] * bim
                im = pw_ref[d, 0] * bim + pw_ref[d, 1] * bre
                wloc_ref[0, rows, base:base + 2 * N] = jnp.where(lo, re, 0.0).astype(BF16)
                wloc_ref[0, rows, base + 2 * N:base + 4 * N] = jnp.where(lo, im, 0.0).astype(BF16)

    tables()
    for i in range(H):
        lane_tile(i)


def _s5_operators(a_re, a_im, log_dt, b_re, b_im, c_re, c_im):
    L, H, n2 = S5_CHUNK, S5_HALF, 2 * S5_N
    g = a_re.shape[0] * S5_GROUPS
    by_dir = lambda a: jnp.moveaxis(a, 1, 0).reshape((2, g) + a.shape[3:])
    a_re, a_im, log_dt = by_dir(a_re), by_dir(a_im), by_dir(log_dt)
    b_re, b_im, c_re, c_im = (a.reshape((g,) + a.shape[2:]) for a in (b_re, b_im, c_re, c_im))
    ld = jnp.broadcast_to(log_dt[:, :, None], a_re.shape)
    per_dir = [a_re[0], a_im[0], ld[0], a_re[1], a_im[1], ld[1]]
    zero = jnp.zeros_like(a_re[0])
    cols = jnp.stack(per_dir + [zero, zero], axis=-1)
    dup = lambda v: jnp.concatenate([v, v], axis=-1)
    rows = jnp.stack([dup(v) for v in per_dir + [zero, zero]], axis=1)
    bt_re, bt_im = b_re.transpose(0, 2, 1), b_im.transpose(0, 2, 1)
    ct_re, ct_im = c_re.transpose(0, 2, 1), c_im.transpose(0, 2, 1)
    per_g = lambda shape: pl.BlockSpec((1,) + tuple(shape[1:]), lambda gi: (gi, 0, 0))
    ins = (cols, rows, bt_re, bt_im, dup(bt_re), dup(bt_im), c_re, c_im, ct_re, ct_im)
    return pl.pallas_call(
        _s5_ops_kernel,
        grid=(g,),
        in_specs=[per_g(a.shape) for a in ins],
        out_specs=(per_g((g, S5_FLAT, S5_FLAT)), per_g((g, S5_FLAT, S5_STATE)),
                   per_g((g, S5_STATE, S5_FLAT)), per_g((g, 8, n2))),
        out_shape=(jax.ShapeDtypeStruct((g, S5_FLAT, S5_FLAT), BF16),
                   jax.ShapeDtypeStruct((g, S5_FLAT, S5_STATE), BF16),
                   jax.ShapeDtypeStruct((g, S5_STATE, S5_FLAT), BF16),
                   jax.ShapeDtypeStruct((g, 8, n2), F32)),
        scratch_shapes=[pltpu.VMEM((4, S5_N, 2 * L), F32),
                        pltpu.VMEM((2, 2, S5_N, 2 * L), F32),
                        pltpu.VMEM((2, 2, L, n2), F32),
                        pltpu.VMEM((H, 2, S5_P, 2 * L), F32)],
        compiler_params=_params("arbitrary"),
        name="s5_operators",
    )(*ins)


def _s5_kernel(u_ref, kb_ref, wloc_ref, wout_ref, ar_ref, y_ref, sin_ref, *, bsz, n_ctx_chunks):
    n_chunks = u_ref.shape[1] // bsz
    n2 = 2 * S5_N
    u = u_ref[0]
    sloc = _dot(u, wloc_ref[0])
    ar = ar_ref[0]
    n_lat = n_chunks - n_ctx_chunks
    fwd_order = list(range(n_lat, n_chunks)) + list(range(n_lat))
    bwd_order = list(range(n_chunks - 1, n_lat - 1, -1)) + list(range(n_lat - 1, -1, -1))
    for d, order in ((0, fwd_order), (1, bwd_order)):
        a_re = ar[2 * d:2 * d + 1, :]
        a_im = ar[2 * d + 1:2 * d + 2, :]
        re_cols = slice(2 * d * n2, (2 * d + 1) * n2)
        im_cols = slice((2 * d + 1) * n2, (2 * d + 2) * n2)
        st_re = jnp.zeros((bsz, n2), F32)
        st_im = jnp.zeros((bsz, n2), F32)
        for c in order:
            rows = slice(c * bsz, (c + 1) * bsz)
            sin_ref[rows, re_cols] = st_re.astype(BF16)
            sin_ref[rows, im_cols] = st_im.astype(BF16)
            st_re, st_im = (a_re * st_re - a_im * st_im + sloc[rows, re_cols],
                            a_re * st_im + a_im * st_re + sloc[rows, im_cols])
    y_ref[0] = _dot(u, kb_ref[0]) + _dot(sin_ref[...], wout_ref[0])


def _s5_mix(ut, layer, kbig, w_loc, w_out, a_rows, bsz):
    g, rows, flat = ut.shape
    kern = functools.partial(_s5_kernel, bsz=bsz, n_ctx_chunks=CTX_LEN // S5_CHUNK)
    per_g = lambda a, off=layer * g: pl.BlockSpec((1,) + a.shape[1:], lambda gi: (off + gi, 0, 0))
    return pl.pallas_call(
        kern,
        grid=(g,),
        in_specs=[per_g(ut, 0)] + [per_g(a) for a in (kbig, w_loc, w_out, a_rows)],
        out_specs=pl.BlockSpec((1, rows, flat), lambda gi: (gi, 0, 0)),
        out_shape=jax.ShapeDtypeStruct((g, rows, flat), F32),
        scratch_shapes=[pltpu.VMEM((rows, S5_STATE), BF16)],
        compiler_params=_params("arbitrary"),
        name="s5_mix",
    )(ut, kbig, w_loc, w_out, a_rows)


def _s5_readout(y5_ref, zd_ref, d_ref, gw_ref, gb_ref, yd_ref):
    lo = lax.broadcasted_iota(jnp.int32, (1, 2 * S5_CHUNK), 1) < S5_CHUNK
    half_rows = (S5_GROUPS, 1, S5_HALF, 2 * S5_CHUNK)
    for j in range(zd_ref.shape[1] // (2 * S5_CHUNK)):
        rows = slice(j * 2 * S5_CHUNK, (j + 1) * 2 * S5_CHUNK)
        a = y5_ref[:, 2 * j, 0, :, :].reshape(S5_GROUPS * S5_HALF, 2 * S5_CHUNK)
        b = y5_ref[:, 2 * j + 1, 0, :, :].reshape(S5_GROUPS * S5_HALF, 2 * S5_CHUNK)
        first = jnp.where(lo, a, pltpu.roll(b, S5_CHUNK, 1))
        second = jnp.where(lo, pltpu.roll(a, S5_CHUNK, 1), b)
        yt = jnp.concatenate([first.reshape(half_rows), second.reshape(half_rows)],
                             axis=1).reshape(W_MIX, 2 * S5_CHUNK)
        y = _gelu(yt.T + d_ref[...] * zd_ref[0, rows, :])
        gate = _sigmoid(_dot(y.astype(BF16), gw_ref[...]) + gb_ref[...])
        yd_ref[rows, :] = (y * gate).astype(yd_ref.dtype)


def _out_ffn_kernel(x_ref, ya_ref, yb_ref, yc_ref, y5_ref, zd_ref, modb_ref, modc_ref, d5_ref,
                    gw_ref, gb_ref, wo_ref, l1w_ref, l1b_ref, wi_ref, wd_ref, l2w_ref, l2b_ref,
                    o_ref, yd_ref, x1_ref, h2_ref, a_ref, *, ctx_rows):
    tm = x_ref.shape[1]
    n_lat = tm - ctx_rows
    segs = [(0, n_lat, 0)] + ([(n_lat, tm, 1)] if ctx_rows else [])
    _s5_readout(y5_ref, zd_ref, d5_ref, gw_ref, gb_ref, yd_ref)
    y = (_dot(ya_ref[0], wo_ref[0, 0:W_MIX, :]) + _dot(yb_ref[0], wo_ref[0, W_MIX:2 * W_MIX, :]) +
         _dot(yc_ref[0], wo_ref[0, 2 * W_MIX:3 * W_MIX, :]) +
         _dot(yd_ref[...], wo_ref[0, 3 * W_MIX:4 * W_MIX, :]))
    for r0, r1, which in segs:
        g1 = _tile_mod(modb_ref, modc_ref, 2)[which]
        sh2 = _tile_mod(modb_ref, modc_ref, 3)[which]
        sc2 = _tile_mod(modb_ref, modc_ref, 4)[which]
        x1 = _layer_norm(ALPHA * x_ref[0, r0:r1, :] + g1 * y[r0:r1, :], l1w_ref[...], l1b_ref[...])
        x1_ref[r0:r1, :] = x1
        h2_ref[r0:r1, :] = (x1 * (1.0 + sc2) + sh2).astype(BF16)
    h2 = h2_ref[...]
    for kk in range(D_FF // FF_CHUNK):
        cols = slice(kk * FF_CHUNK, (kk + 1) * FF_CHUNK)
        gk = _dot(h2, wi_ref[0, :, cols])
        uk = _dot(h2, wi_ref[0, :, D_FF + kk * FF_CHUNK:D_FF + (kk + 1) * FF_CHUNK])
        a_ref[:, cols] = (_silu(gk) * uk).astype(BF16)
    f = _dot(a_ref[...], wd_ref[0])
    for r0, r1, which in segs:
        g2 = _tile_mod(modb_ref, modc_ref, 5)[which]
        o_ref[0, r0:r1, :] = _layer_norm(ALPHA * x1_ref[r0:r1, :] + g2 * f[r0:r1, :],
                                         l2w_ref[...], l2b_ref[...])


def _out_ffn(xcat, ya, yb, yc, y5, zd, mod, layer, s5_d, glu_w, glu_b, w_out, ln1_w, ln1_b, w_ffn_in,
             w_ffn_out, ln2_w, ln2_b, tm, n_out):
    bsz, s, d = xcat.shape
    ctx_rows = CTX_LEN if n_out == s else 0
    tok = lambda w: pl.BlockSpec((1, tm, w), lambda b, j: (b, j, 0))
    row = lambda a: a.reshape(1, -1)
    return pl.pallas_call(
        functools.partial(_out_ffn_kernel, ctx_rows=ctx_rows),
        grid=(bsz, n_out // tm),
        in_specs=[tok(d), tok(W_MIX), tok(W_MIX), tok(W_MIX),
                  pl.BlockSpec((S5_GROUPS, tm // S5_CHUNK, 1, S5_HALF, 2 * S5_CHUNK),
                               lambda b, j: (0, j, b, 0, 0)),
                  tok(W_MIX),
                  pl.BlockSpec((1, 6, d), lambda b, j: (b, 0, 0)),
                  pl.BlockSpec((1, 6, d), lambda b, j: (bsz, 0, 0)),
                  _const_spec((1, W_MIX)), _const_spec((W_MIX, W_MIX)), _const_spec((1, W_MIX)),
                  _layer_spec(w_out, layer), _const_spec((1, d)), _const_spec((1, d)),
                  _layer_spec(w_ffn_in, layer), _layer_spec(w_ffn_out, layer),
                  _const_spec((1, d)), _const_spec((1, d))],
        out_specs=tok(d),
        out_shape=jax.ShapeDtypeStruct((bsz, n_out, d), F32),
        scratch_shapes=[pltpu.VMEM((tm, W_MIX), BF16), pltpu.VMEM((tm, d), F32),
                        pltpu.VMEM((tm, d), BF16), pltpu.VMEM((tm, D_FF), BF16)],
        compiler_params=_params("arbitrary", "arbitrary"),
        name="out_ffn",
    )(xcat, ya, yb, yc, y5, zd, mod, mod, row(s5_d), glu_w.astype(BF16), row(glu_b), w_out,
      row(ln1_w), row(ln1_b), w_ffn_in, w_ffn_out, row(ln2_w), row(ln2_b))


_S5_LOCAL = np.arange(S5_P).reshape(S5_HALF, 2).T.reshape(-1)
_S5_ORDER = (np.arange(S5_GROUPS)[:, None] * S5_P + _S5_LOCAL[None, :]).reshape(-1)


def _reorder_w_in(w):
    off_b = 4 * W_MIX + N_GATES
    off_d = w.shape[-1] - W_MIX
    w_r = jnp.concatenate([w[..., :4 * W_MIX], w[..., off_b:off_d], w[..., off_d:][..., _S5_ORDER]], axis=-1)
    return w_r.astype(BF16), jnp.swapaxes(w[..., 4 * W_MIX:off_b], -1, -2).astype(BF16)


def kernel(x, c, ctx, c_ctx, w_mod, b_mod, w_in, mlstm_gate_bias, mlstm_norm_w, sgu_ln_w, sgu_ln_b, sgu_w, sgu_b, conv_w, conv_b, conv_ln_w, conv_ln_b, s5_a_re, s5_a_im, s5_log_dt, s5_b_re, s5_b_im, s5_c_re, s5_c_im, s5_d, s5_glu_w, s5_glu_b, w_out, ln1_w, ln1_b, w_ffn_in, w_ffn_out, ln2_w, ln2_b):
    bsz, seq, d = x.shape
    s = seq + CTX_LEN
    assert seq == SEQ_LATENT and ctx.shape[1] == CTX_LEN
    mod_rows = -(-(bsz + 1) // 8) * 8
    cvec = jnp.zeros((mod_rows, d), F32).at[:bsz].set(c).at[bsz].set(c_ctx)
    mod_all = _modulation(cvec, w_mod, b_mod).reshape(DEPTH, mod_rows, 6, d)
    nc5 = s // S5_CHUNK
    w_r, wg_t = _reorder_w_in(w_in)
    w_out_b = jnp.concatenate([w_out[:, :3 * W_MIX], w_out[:, 3 * W_MIX:][:, _S5_ORDER]], axis=1).astype(BF16)
    w_ffn_in_b, w_ffn_out_b = w_ffn_in.astype(BF16), w_ffn_out.astype(BF16)
    kbig, w_loc, w_so, a_rows = _s5_operators(
        s5_a_re, s5_a_im, s5_log_dt, s5_b_re[..., _S5_LOCAL], s5_b_im[..., _S5_LOCAL],
        s5_c_re[:, :, _S5_LOCAL, :], s5_c_im[:, :, _S5_LOCAL, :])
    for l in range(DEPTH):
        mod = mod_all[l]
        src = (x, ctx, 0, True) if l == 0 else (xcat, xcat, seq // TILE, False)
        outs = _in_projection(*src, mod, l, w_r, wg_t, sgu_w[l], sgu_b[l], sgu_ln_w[l], sgu_ln_b[l],
                              conv_w[l], conv_b[l], conv_ln_w[l], conv_ln_b[l])
        if l == 0:
            xcat, outs = outs[0], outs[1:]
        k, qt, vt, ot, yb, yc, zd, ut, gt = outs
        ya = _mlstm(k, qt, vt, ot, gt, mlstm_gate_bias[l], mlstm_norm_w[l])
        y5 = _s5_mix(ut.reshape(S5_GROUPS, nc5 * bsz, S5_FLAT), l, kbig, w_loc, w_so, a_rows, bsz)
        y5 = y5.reshape(S5_GROUPS, nc5, bsz, S5_HALF, 2 * S5_CHUNK)
        last = l == DEPTH - 1
        xcat = _out_ffn(xcat, ya, yb, yc, y5, zd, mod, l, s5_d[l][_S5_ORDER],
                        s5_glu_w[l][_S5_ORDER][:, _S5_ORDER], s5_glu_b[l][_S5_ORDER],
                        w_out_b, ln1_w[l], ln1_b[l], w_ffn_in_b, w_ffn_out_b, ln2_w[l], ln2_b[l],
                        tm=TM_LATENT if last else TM_MAIN, n_out=seq if last else s)
    return xcat
```

```python
import functools

import numpy as np
import jax
import jax.numpy as jnp
from jax import lax
from jax.experimental import pallas as pl
from jax.experimental.pallas import tpu as pltpu

F32 = jnp.float32
BF16 = jnp.bfloat16

DEPTH = 2
CTX_LEN = 256
SEQ_LATENT = 2048
GRID_W = 64
W_MIX = 256
HEADS = 4
HEAD_DIM = W_MIX // HEADS
SGU_GROUPS = 4
SGU_GD = W_MIX // SGU_GROUPS
SGU_CHUNK = 128
CONV_WIDTH = 31
CONV_PAD = CONV_WIDTH // 2
CONV_HALO = 32
S5_P = 16
S5_GROUPS = W_MIX // S5_P
S5_N = 64
S5_CHUNK = 64
S5_HALF = S5_P // 2
S5_FLAT = S5_P * S5_CHUNK
S5_STATE = 8 * S5_N
D_FF = 2816
FF_CHUNK = 256
ALPHA = (2 * DEPTH) ** 0.25
LN_EPS = 1e-5
TILE = 256
MLSTM_CHUNK = 256
TM_MAIN = 3 * TILE
TM_LATENT = 2 * TILE
CONV_ROWS = max((TILE // GRID_W) * (GRID_W + CONV_HALO), CTX_LEN + CONV_HALO)
N_GATES = 4 * HEADS
GATE_PAD = 128
C_O = 3 * W_MIX
C_B = 4 * W_MIX
C_C = C_B + 2 * W_MIX
C_D = C_C + 2 * W_MIX
C_G = C_D + W_MIX
VMEM_LIMIT = 56 * 1024 * 1024


def _dot(a, b):
    return jnp.dot(a, b, preferred_element_type=F32)


def _dot_nt(a, b):
    return lax.dot_general(a, b, (((1,), (1,)), ((), ())), preferred_element_type=F32)


def _split3(x):
    hi = x.astype(BF16)
    r = x - hi.astype(F32)
    mid = r.astype(BF16)
    lo = (r - mid.astype(F32)).astype(BF16)
    return hi, mid, lo


def _layer_norm(x, w, b):
    mu = jnp.mean(x, axis=-1, keepdims=True)
    d = x - mu
    var = jnp.mean(d * d, axis=-1, keepdims=True)
    return d * lax.rsqrt(var + LN_EPS) * w + b


def _sigmoid(x):
    return jax.nn.sigmoid(x)


def _silu(x):
    return x * jax.nn.sigmoid(x)


def _gelu(x):
    return jax.nn.gelu(x, approximate=True)


def _log_sigmoid(x):
    return jnp.minimum(x, 0.0) - jnp.log1p(jnp.exp(-jnp.abs(x)))


def _params(*sem):
    return pltpu.CompilerParams(dimension_semantics=sem, vmem_limit_bytes=VMEM_LIMIT)


def _const_spec(shape):
    nd = len(shape)
    return pl.BlockSpec(shape, lambda *_: (0,) * nd, pipeline_mode=pl.Buffered(1))


def _layer_spec(stacked, l):
    nd = stacked.ndim
    return pl.BlockSpec((1,) + stacked.shape[1:], lambda *_: (l,) + (0,) * (nd - 1),
                        pipeline_mode=pl.Buffered(1))


def _mod_kernel(c_ref, w_ref, b_ref, o_ref):
    s = _silu(c_ref[...])
    s_hi = s.astype(BF16)
    s_lo = (s - s_hi.astype(F32)).astype(BF16)
    w = w_ref[0]
    w_hi = w.astype(BF16)
    w_lo = (w - w_hi.astype(F32)).astype(BF16)
    o_ref[0] = _dot(s_hi, w_hi) + _dot(s_hi, w_lo) + _dot(s_lo, w_hi) + b_ref[0]


def _modulation(cvec, w_mod, b_mod):
    depth, d, n = w_mod.shape
    rows = cvec.shape[0]
    tn = 1024
    return pl.pallas_call(
        _mod_kernel,
        grid=(depth, n // tn),
        in_specs=[pl.BlockSpec((rows, d), lambda l, j: (0, 0)),
                  pl.BlockSpec((1, d, tn), lambda l, j: (l, 0, j)),
                  pl.BlockSpec((1, 1, tn), lambda l, j: (l, 0, j))],
        out_specs=pl.BlockSpec((1, rows, tn), lambda l, j: (l, 0, j)),
        out_shape=jax.ShapeDtypeStruct((depth, rows, n), F32),
        compiler_params=_params("arbitrary", "arbitrary"),
        name="modulation",
    )(cvec, w_mod, b_mod.reshape(depth, 1, n))


def _tile_mod(modb_ref, modc_ref, i):
    last = pl.program_id(1) == pl.num_programs(1) - 1
    row_b = modb_ref[0, i:i + 1, :]
    return row_b, jnp.where(last, modc_ref[0, i:i + 1, :], row_b)


def _sgu_slab(zb, sgw_ref, sgb_ref, slnw_ref, slnb_ref):
    u = _gelu(zb[:, 0:W_MIX])
    vn = _layer_norm(_gelu(zb[:, W_MIX:2 * W_MIX]), slnw_ref[...], slnb_ref[...]).astype(BF16)
    lane_grp = lax.broadcasted_iota(jnp.int32, (1, W_MIX), 1) // SGU_GD
    out = []
    for cl in range(TILE // SGU_CHUNK):
        vc = vn[cl * SGU_CHUNK:(cl + 1) * SGU_CHUNK, :]
        acc = sgb_ref[...]
        for gi in range(SGU_GROUPS):
            acc = acc + _dot(sgw_ref[gi], jnp.where(lane_grp == gi, vc, jnp.zeros_like(vc)))
        out.append(acc)
    return u * jnp.concatenate(out, axis=0)


def _conv_slab(a, seg_len, cw_ref, sh_ref):
    n_seg = TILE // seg_len
    stride = seg_len + CONV_HALO
    zeros16 = jnp.zeros((16, W_MIX), F32)
    for sg in range(n_seg):
        base = sg * stride
        seg = a[sg * seg_len:(sg + 1) * seg_len, :]
        for r in range(8):
            sh_ref[r, pl.ds(base, 16), :] = zeros16
            sh_ref[r, pl.ds(base + seg_len + 8, 16), :] = zeros16
            sh_ref[r, pl.ds(base + 16 - r, seg_len), :] = seg
    out = []
    for sg in range(n_seg):
        base = sg * stride
        acc = jnp.zeros((seg_len, W_MIX), F32)
        for t in range(CONV_WIDTH):
            off = t - CONV_PAD
            r = off % 8
            acc = acc + cw_ref[t:t + 1, :] * sh_ref[r, pl.ds(base + (off - r) + 16, seg_len), :]
        out.append(acc)
    return out[0] if n_seg == 1 else jnp.concatenate(out, axis=0)


def _emit_s5_rows(zdt, ut_ref, chunk0):
    n_tok = zdt.shape[1]
    lo = lax.broadcasted_iota(jnp.int32, (1, 2 * S5_CHUNK), 1) < S5_CHUNK
    halves = zdt.reshape(S5_GROUPS, 2, S5_HALF, n_tok)
    first = halves[:, 0].reshape(S5_GROUPS * S5_HALF, n_tok)
    second = halves[:, 1].reshape(S5_GROUPS * S5_HALF, n_tok)
    for j in range(n_tok // (2 * S5_CHUNK)):
        a = first[:, j * 2 * S5_CHUNK:(j + 1) * 2 * S5_CHUNK]
        b = second[:, j * 2 * S5_CHUNK:(j + 1) * 2 * S5_CHUNK]
        pieces = (jnp.where(lo, a, pltpu.roll(b, S5_CHUNK, 1)),
                  jnp.where(lo, pltpu.roll(a, S5_CHUNK, 1), b))
        for cc, piece in enumerate(pieces):
            ut_ref[:, chunk0 + 2 * j + cc, 0, :, :] = piece.reshape(S5_GROUPS, S5_HALF, 2 * S5_CHUNK)


def _inproj_kernel(*refs, n_slabs, emit_stream):
    x_refs = refs[:n_slabs]
    (ctx_ref, modb_ref, modc_ref, w_ref, wgt_ref, sgw_ref, sgb_ref, slnw_ref, slnb_ref,
     cw_ref, cb_ref, clnw_ref, clnb_ref) = refs[n_slabs:n_slabs + 13]
    outs = refs[n_slabs + 13:]
    if emit_stream:
        xo_ref, outs = outs[0], outs[1:]
    k_ref, qt_ref, vt_ref, ot_ref, yb_ref, yc_ref, zd_ref, ut_ref, gt_ref, h_ref, sh_ref = outs
    last = pl.program_id(1) == pl.num_programs(1) - 1
    sh_b, sh_c = _tile_mod(modb_ref, modc_ref, 0)
    sc_b, sc_c = _tile_mod(modb_ref, modc_ref, 1)
    for r in range(n_slabs):
        rows = slice(r * TILE, (r + 1) * TILE)
        x = x_refs[r][0]
        sc, sh = sc_b, sh_b
        if r == n_slabs - 1:
            x = jnp.where(last, ctx_ref[0], x)
            sc, sh = sc_c, sh_c
        if emit_stream:
            xo_ref[0, rows, :] = x
        h_ref[rows, :] = (x * (1.0 + sc) + sh).astype(BF16)
    for r in range(n_slabs):
        rows = slice(r * TILE, (r + 1) * TILE)
        h = h_ref[rows, :]
        qt_ref[0, :, rows] = _dot(h, w_ref[0, :, 0:W_MIX]).T.astype(BF16)
        k_ref[0, rows, :] = (_dot(h, w_ref[0, :, W_MIX:2 * W_MIX]) * (HEAD_DIM ** -0.5)).astype(BF16)
        vt_ref[0, :, rows] = _dot(h, w_ref[0, :, 2 * W_MIX:3 * W_MIX]).T.astype(BF16)
        ot_ref[0, :, rows] = _dot(h, w_ref[0, :, C_O:C_B]).T
        zd = _dot(h, w_ref[0, :, C_D:C_G])
        zd_ref[0, rows, :] = zd
        _emit_s5_rows(zd.T, ut_ref, r * (TILE // S5_CHUNK))
        gt_ref[0, :, rows] = _dot_nt(wgt_ref[0], h)
        yb = _sgu_slab(_dot(h, w_ref[0, :, C_B:C_C]), sgw_ref, sgb_ref, slnw_ref, slnb_ref)
        yb_ref[0, rows, :] = yb.astype(yb_ref.dtype)
        zc = _dot(h, w_ref[0, :, C_C:C_D])
        a = zc[:, 0:W_MIX] * _sigmoid(zc[:, W_MIX:2 * W_MIX])

        def conv_out(seg_len, a=a, rows=rows, r=r):
            acc = _conv_slab(a, seg_len, cw_ref, sh_ref.at[r])
            yc = _silu(_layer_norm(acc + cb_ref[...], clnw_ref[...], clnb_ref[...]))
            yc_ref[0, rows, :] = yc.astype(yc_ref.dtype)

        if r < n_slabs - 1:
            conv_out(GRID_W)
        else:
            pl.when(jnp.logical_not(last))(functools.partial(conv_out, GRID_W))
            pl.when(last)(functools.partial(conv_out, CTX_LEN))


def _in_projection(lat, ctx, ctx_block, emit_stream, mod, layer, w_r, wg_t, sgu_w, sgu_b, sgu_ln_w,
                   sgu_ln_b, conv_w, conv_b, conv_ln_w, conv_ln_b):
    bsz, _, d = lat.shape
    s = SEQ_LATENT + CTX_LEN
    tm = TM_MAIN
    n_slabs = tm // TILE
    last_lat = SEQ_LATENT // TILE - 1
    slab = lambda r: pl.BlockSpec((1, TILE, d), lambda b, j: (b, jnp.minimum(n_slabs * j + r, last_lat), 0))
    nc5 = s // S5_CHUNK
    cpt = tm // S5_CHUNK
    bias_full = jnp.repeat(sgu_b.T, SGU_GD, axis=1)
    row = lambda a: a.reshape(1, W_MIX)
    tok = lambda w: pl.BlockSpec((1, tm, w), lambda b, j: (b, j, 0))
    chan = lambda c: pl.BlockSpec((1, c, tm), lambda b, j: (b, 0, j))
    out_shape = (
        jax.ShapeDtypeStruct((bsz, s, W_MIX), BF16),
        jax.ShapeDtypeStruct((bsz, W_MIX, s), BF16),
        jax.ShapeDtypeStruct((bsz, W_MIX, s), BF16),
        jax.ShapeDtypeStruct((bsz, W_MIX, s), F32),
        jax.ShapeDtypeStruct((bsz, s, W_MIX), BF16),
        jax.ShapeDtypeStruct((bsz, s, W_MIX), BF16),
        jax.ShapeDtypeStruct((bsz, s, W_MIX), F32),
        jax.ShapeDtypeStruct((S5_GROUPS, nc5, bsz, S5_HALF, 2 * S5_CHUNK), F32),
        jax.ShapeDtypeStruct((bsz, N_GATES, s), F32),
    )
    out_specs = (
        tok(W_MIX), chan(W_MIX), chan(W_MIX), chan(W_MIX), tok(W_MIX), tok(W_MIX), tok(W_MIX),
        pl.BlockSpec((S5_GROUPS, cpt, 1, S5_HALF, 2 * S5_CHUNK), lambda b, j: (0, j, b, 0, 0)),
        chan(N_GATES),
    )
    if emit_stream:
        out_shape = (jax.ShapeDtypeStruct((bsz, s, d), F32),) + out_shape
        out_specs = (tok(d),) + out_specs
    return pl.pallas_call(
        functools.partial(_inproj_kernel, n_slabs=n_slabs, emit_stream=emit_stream),
        grid=(bsz, s // tm),
        in_specs=[slab(r) for r in range(n_slabs)] + [
                  pl.BlockSpec((1, TILE, d), lambda b, j: (b, ctx_block, 0)),
                  pl.BlockSpec((1, 6, d), lambda b, j: (b, 0, 0)),
                  pl.BlockSpec((1, 6, d), lambda b, j: (bsz, 0, 0)),
                  _layer_spec(w_r, layer),
                  _layer_spec(wg_t, layer),
                  _const_spec((SGU_GROUPS, SGU_CHUNK, SGU_CHUNK)),
                  _const_spec((SGU_CHUNK, W_MIX)),
                  _const_spec((1, W_MIX)), _const_spec((1, W_MIX)),
                  _const_spec((CONV_WIDTH, W_MIX)), _const_spec((1, W_MIX)),
                  _const_spec((1, W_MIX)), _const_spec((1, W_MIX))],
        out_specs=out_specs,
        out_shape=out_shape,
        scratch_shapes=[pltpu.VMEM((tm, d), BF16),
                        pltpu.VMEM((tm // TILE, 8, CONV_ROWS, W_MIX), F32)],
        compiler_params=_params("arbitrary", "arbitrary"),
        name="in_projection",
    )(*([lat] * n_slabs), ctx, mod, mod, w_r, wg_t, sgu_w.astype(BF16), bias_full, row(sgu_ln_w),
      row(sgu_ln_b), conv_w, row(conv_b), row(conv_ln_w), row(conv_ln_b))


def _mlstm_kernel(k_ref, qt_ref, vt_ref, ot_ref, gt_ref, gbt_ref, nw_ref, y_ref,
                  ht_ref, src_ref, lgt_ref, cumt_ref, pm_ref, c_ref, n_ref, m_ref):
    L = MLSTM_CHUNK
    n_chunks = k_ref.shape[1] // L
    row = lax.broadcasted_iota(jnp.int32, (L, L), 0)
    col = lax.broadcasted_iota(jnp.int32, (L, L), 1)
    tri = (col <= row).astype(BF16)
    lane_g = lax.broadcasted_iota(jnp.int32, (1, GATE_PAD), 1)
    sub_g = lax.broadcasted_iota(jnp.int32, (N_GATES, 1), 0)
    is_f_sub = (sub_g % 8) >= HEADS
    lane_w = lax.broadcasted_iota(jnp.int32, (1, W_MIX), 1) // HEAD_DIM
    sub_w = lax.broadcasted_iota(jnp.int32, (W_MIX, 1), 0) // HEAD_DIM
    bd_mask = sub_w == lane_w
    n_mask = sub_g == lane_w
    row4 = lax.broadcasted_iota(jnp.int32, (HEADS * L, L), 0) % L
    col4 = lax.broadcasted_iota(jnp.int32, (HEADS * L, L), 1)
    mask4 = (row4 <= col4, row4 >= col4)

    for c in range(n_chunks):
        cols = slice(c * L, (c + 1) * L)
        grawt = gt_ref[0, :, cols] + gbt_ref[...]
        lgt = jnp.where(is_f_sub, _log_sigmoid(grawt), grawt)
        lgt_ref[:, cols] = lgt
        hi, mid, lo = _split3(lgt)
        pret = _dot_nt(hi, tri) + _dot_nt(mid, tri) + _dot_nt(lo, tri)
        cumt = jnp.where(sub_g >= 8, pret[:, L - 1:L] - pret + lgt, pret)
        cumt_ref[:, cols] = cumt
        srct = jnp.where(is_f_sub, -jnp.inf, lgt - pltpu.roll(cumt, N_GATES - HEADS, 0))
        pm_ref[:, cols] = srct
        padded = jnp.concatenate([srct, jnp.zeros((GATE_PAD - N_GATES, L), F32)], axis=0)
        src_ref[cols, :] = padded.T
    pos = lax.broadcasted_iota(jnp.int32, (1, pm_ref.shape[1]), 1) % L
    pm = pm_ref[...]
    sh = 1
    while sh < L:
        fwd_v = jnp.where(pos >= sh, pltpu.roll(pm, sh, 1), -jnp.inf)
        bwd_v = jnp.where(pos < L - sh, pltpu.roll(pm, pm.shape[1] - sh, 1), -jnp.inf)
        pm = jnp.maximum(pm, jnp.where(sub_g >= 8, bwd_v, fwd_v))
        sh *= 2
    pm_ref[...] = pm

    def chunk(c, d):
        r0 = pl.multiple_of(c * L, L)
        k = k_ref[0, pl.ds(r0, L), :]
        qt = qt_ref[0, :, pl.ds(r0, L)]
        vt = vt_ref[0, :, pl.ds(r0, L)]
        src_cols = src_ref[pl.ds(r0, L), :]
        lgt = lgt_ref[:, pl.ds(r0, L)]
        cumt = cumt_ref[:, pl.ds(r0, L)]
        pm = pm_ref[:, pl.ds(r0, L)]
        tot = cumt[:, L - 1:L] if d == 0 else cumt[:, 0:1]
        m_vec = m_ref[d]
        c_old = c_ref[d]
        n_old = n_ref[d]
        qc = _dot(c_old.astype(BF16), qt)
        qn = _dot(n_old.astype(BF16), qt)
        e_slabs = []
        e_small = jnp.zeros((N_GATES, L), F32)
        decay_row = jnp.zeros((1, W_MIX), F32)
        m_new_vec = m_vec
        srcs, col_terms, khs, w_inters, floors = [], [], [], [], []
        for h in range(HEADS):
            ic = d * 8 + h
            fc = d * 8 + HEADS + h
            m_h = m_vec[:, fc:fc + 1]
            cct = cumt[fc:fc + 1, :]
            inter = cct + m_h
            m_t = jnp.maximum(inter, cct + pm[ic:ic + 1, :])
            w_inters.append(jnp.exp(inter - m_t))
            floors.append(jnp.exp(-m_t))
            col_terms.append(jnp.broadcast_to(cct - m_t, (L, L)))
            srcs.append(src_cols[:, ic:ic + 1])
            khs.append(jnp.where(lane_w == h, k, jnp.zeros_like(k)))
        arg = jnp.concatenate(srcs, axis=0) + jnp.concatenate(col_terms, axis=0)
        arg = jnp.where(mask4[d], arg, -jnp.inf)
        sc = _dot(jnp.concatenate(khs, axis=0), qt) * jnp.exp(arg)
        scb = sc.astype(BF16)
        for h in range(HEADS):
            ic = d * 8 + h
            fc = d * 8 + HEADS + h
            head_lanes = lane_w == h
            sl = slice(h * HEAD_DIM, (h + 1) * HEAD_DIM)
            hrows = slice(h * L, (h + 1) * L)
            m_h = m_vec[:, fc:fc + 1]
            cct = cumt[fc:fc + 1, :]
            den = jnp.sum(sc[hrows, :], axis=0, keepdims=True) + w_inters[h] * qn[h:h + 1, :]
            num = _dot(vt[sl, :], scb[hrows, :]) + w_inters[h] * qc[sl, :]
            ht_ref[d, sl, pl.ds(r0, L)] = num / jnp.maximum(jnp.abs(den), floors[h])
            tot_h = tot[fc:fc + 1, :]
            logu = tot_h - cct + lgt[ic:ic + 1, :]
            m_new = jnp.maximum(tot_h + m_h, jnp.max(logu, axis=1, keepdims=True))
            e_h = jnp.exp(logu - m_new)
            e_slabs.append(jnp.broadcast_to(e_h, (HEAD_DIM, L)))
            e_small = jnp.where(sub_g == h, e_h, e_small)
            decay_row = jnp.where(head_lanes, jnp.exp(tot_h + m_h - m_new), decay_row)
            m_new_vec = jnp.where(lane_g == fc, m_new, m_new_vec)
        vet = (vt.astype(F32) * jnp.concatenate(e_slabs, axis=0)).astype(BF16)
        c_ref[d] = decay_row * c_old + jnp.where(bd_mask, _dot(vet, k), 0.0)
        n_ref[d] = decay_row * n_old + jnp.where(n_mask, _dot(e_small.astype(BF16), k), 0.0)
        m_ref[d] = m_new_vec

    c_ref[...] = jnp.zeros_like(c_ref)
    n_ref[...] = jnp.zeros_like(n_ref)
    m_ref[...] = jnp.zeros_like(m_ref)
    n_lat = n_chunks - CTX_LEN // L
    for j in range(n_chunks - n_lat):
        chunk(n_lat + j, 0)
        chunk(n_chunks - 1 - j, 1)

    def scan_body(i, carry):
        chunk(i, 0)
        chunk(n_lat - 1 - i, 1)
        return carry

    lax.fori_loop(0, n_lat, scan_body, 0)

    def readout(c, carry):
        r0 = pl.multiple_of(c * L, L)
        slabs = []
        for h in range(HEADS):
            sl = slice(h * HEAD_DIM, (h + 1) * HEAD_DIM)
            hs = ht_ref[0, sl, pl.ds(r0, L)] + ht_ref[1, sl, pl.ds(r0, L)]
            mu = jnp.mean(hs, axis=0, keepdims=True)
            dl = hs - mu
            var = jnp.mean(dl * dl, axis=0, keepdims=True)
            hn = dl * lax.rsqrt(var + LN_EPS) * nw_ref[sl, :]
            slabs.append(hn * _sigmoid(ot_ref[0, sl, pl.ds(r0, L)]))
        y_ref[0, pl.ds(r0, L), :] = jnp.concatenate(slabs, axis=0).T.astype(y_ref.dtype)
        return carry

    lax.fori_loop(0, n_chunks, readout, 0)


def _mlstm(k, qt, vt, ot, gt, gate_bias, norm_w):
    bsz, s, _ = k.shape
    gbt = gate_bias.reshape(N_GATES, 1)
    nw_full = jnp.broadcast_to(norm_w.reshape(W_MIX, 1), (W_MIX, MLSTM_CHUNK))
    tok = lambda w: pl.BlockSpec((1, s, w), lambda b: (b, 0, 0))
    chan = lambda c: pl.BlockSpec((1, c, s), lambda b: (b, 0, 0))
    return pl.pallas_call(
        _mlstm_kernel,
        grid=(bsz,),
        in_specs=[tok(W_MIX), chan(W_MIX), chan(W_MIX), chan(W_MIX), chan(N_GATES),
                  pl.BlockSpec((N_GATES, 1), lambda b: (0, 0)),
                  pl.BlockSpec((W_MIX, MLSTM_CHUNK), lambda b: (0, 0))],
        out_specs=tok(W_MIX),
        out_shape=jax.ShapeDtypeStruct((bsz, s, W_MIX), BF16),
        scratch_shapes=[pltpu.VMEM((2, W_MIX, s), F32),
                        pltpu.VMEM((s, GATE_PAD), F32),
                        pltpu.VMEM((N_GATES, s), F32),
                        pltpu.VMEM((N_GATES, s), F32),
                        pltpu.VMEM((N_GATES, s), F32),
                        pltpu.VMEM((2, W_MIX, W_MIX), F32),
                        pltpu.VMEM((2, N_GATES, W_MIX), F32),
                        pltpu.VMEM((2, 1, GATE_PAD), F32)],
        compiler_params=_params("arbitrary"),
        name="mlstm",
    )(k, qt, vt, ot, gt, gbt, nw_full)


def _cpow(ar, ai, ld, k):
    dt = jnp.exp(ld)
    mag = jnp.exp(k * (ar * dt))
    ang = k * (ai * dt)
    return mag * jnp.cos(ang), mag * jnp.sin(ang)


def _zoh_coef(ar, ai, ld):
    er, ei = _cpow(ar, ai, ld, 1.0)
    den = ar * ar + ai * ai
    return ((er - 1.0) * ar + ei * ai) / den, (ei * ar - (er - 1.0) * ai) / den


def _dot_x3(a, b):
    a_hi, a_mid, _ = _split3(a)
    b_hi, b_mid, _ = _split3(b)
    return _dot(a_hi, b_hi) + _dot(a_hi, b_mid) + _dot(a_mid, b_hi)


def _tap_table(lag, col, cf):
    fwd = lag >= 0
    k = jnp.abs(lag).astype(F32)
    sel = lambda a, b: jnp.where(fwd, a, b)
    pre, pim = _cpow(sel(col(0), col(3)), sel(col(1), col(4)), sel(col(2), col(5)), k)
    cre, cim = sel(cf[0][0], cf[1][0]), sel(cf[0][1], cf[1][1])
    both = lag == 0
    return (pre * cre - pim * cim + jnp.where(both, cf[1][0], 0.0),
            pre * cim + pim * cre + jnp.where(both, cf[1][1], 0.0))


def _s5_ops_kernel(cols_ref, rows_ref, btre_ref, btim_ref, bdre_ref, bdim_ref, cre_ref, cim_ref,
                   ctre_ref, ctim_ref, kb_ref, wloc_ref, wout_ref, ar_ref,
                   mt_ref, pt_ref, pw_ref, v_ref):
    L, H, N = S5_CHUNK, S5_HALF, S5_N
    lane = lax.broadcasted_iota(jnp.int32, (1, 2 * L), 1)
    lo = lane < L

    def tables():
        col = lambda k: cols_ref[0, :, k:k + 1]
        row = lambda k: rows_ref[0, k:k + 1, :]
        cf = [_zoh_coef(col(3 * d), col(3 * d + 1), col(3 * d + 2)) for d in range(2)]
        mt_ref[0], mt_ref[1] = _tap_table(jnp.where(lo, lane, lane - 2 * L), col, cf)
        mt_ref[2], mt_ref[3] = _tap_table(lane - L, col, cf)
        tt = lane % L
        ss = lax.broadcasted_iota(jnp.int32, (L, 1), 0)
        for d in range(2):
            kk = (tt + 1 if d == 0 else L - tt).astype(F32)
            pt_ref[d, 0], pt_ref[d, 1] = _cpow(col(3 * d), col(3 * d + 1), col(3 * d + 2), kk)
            ks = (L - 1 - ss if d == 0 else ss).astype(F32)
            qre, qim = _cpow(row(3 * d), row(3 * d + 1), row(3 * d + 2), ks)
            fre, fim = _zoh_coef(row(3 * d), row(3 * d + 1), row(3 * d + 2))
            pw_ref[d, 0] = qre * fre - qim * fim
            pw_ref[d, 1] = qre * fim + qim * fre
            ar_ref[0, 2 * d:2 * d + 1, :], ar_ref[0, 2 * d + 1:2 * d + 2, :] = _cpow(
                row(3 * d), row(3 * d + 1), row(3 * d + 2), float(L))
        ar_ref[0, 4:8, :] = jnp.zeros((4, 2 * N), F32)

    def lane_tile(i):
        tile = slice(i * 2 * L, (i + 1) * 2 * L)
        for half in range(2):
            p = i + half * H
            cre = cre_ref[0, p:p + 1, :]
            cim = cim_ref[0, p:p + 1, :]
            g_re = cre * btre_ref[0] - cim * btim_ref[0]
            g_im = cre * btim_ref[0] + cim * btre_ref[0]
            v_ref[i, half] = _dot_x3(g_re, mt_ref[2 * half]) - _dot_x3(g_im, mt_ref[2 * half + 1])
        for q in range(S5_P):
            r0 = (q % H) * 2 * L + (q // H) * L
            toe = [pltpu.roll(jnp.broadcast_to(v_ref[i, half, q:q + 1, :], (L, 2 * L)), 0, 1,
                              stride=1, stride_axis=0) for half in range(2)]
            kb_ref[0, r0:r0 + L, tile] = jnp.where(lo, toe[0], toe[1]).astype(BF16)
        ccre = jnp.where(lo, ctre_ref[0, :, i:i + 1], ctre_ref[0, :, i + H:i + H + 1])
        ccim = jnp.where(lo, ctim_ref[0, :, i:i + 1], ctim_ref[0, :, i + H:i + H + 1])
        pad = jnp.zeros((N, 2 * L), BF16)
        for d in range(2):
            pre, pim = pt_ref[d, 0], pt_ref[d, 1]
            base = d * 4 * N
            wout_ref[0, base:base + N, tile] = (ccre * pre - ccim * pim).astype(BF16)
            wout_ref[0, base + N:base + 2 * N, tile] = pad
            wout_ref[0, base + 2 * N:base + 3 * N, tile] = (-(ccre * pim + ccim * pre)).astype(BF16)
            wout_ref[0, base + 3 * N:base + 4 * N, tile] = pad
        for half in range(2):
            q = i + half * H
            bre = bdre_ref[0, q:q + 1, :]
            bim = bdim_ref[0, q:q + 1, :]
            rows = slice(i * 2 * L + half * L, i * 2 * L + (half + 1) * L)
            for d in range(2):
                base = d * 4 * N
                re = pw_ref[d, 0] * bre - pw_ref[d, 1] * bim
                im = pw_ref[d, 0] * bim + pw_ref[d, 1] * bre
                wloc_ref[0, rows, base:base + 2 * N] = jnp.where(lo, re, 0.0).astype(BF16)
                wloc_ref[0, rows, base + 2 * N:base + 4 * N] = jnp.where(lo, im, 0.0).astype(BF16)

    tables()
    for i in range(H):
        lane_tile(i)


def _s5_operators(a_re, a_im, log_dt, b_re, b_im, c_re, c_im):
    L, H, n2 = S5_CHUNK, S5_HALF, 2 * S5_N
    g = a_re.shape[0] * S5_GROUPS
    by_dir = lambda a: jnp.moveaxis(a, 1, 0).reshape((2, g) + a.shape[3:])
    a_re, a_im, log_dt = by_dir(a_re), by_dir(a_im), by_dir(log_dt)
    b_re, b_im, c_re, c_im = (a.reshape((g,) + a.shape[2:]) for a in (b_re, b_im, c_re, c_im))
    ld = jnp.broadcast_to(log_dt[:, :, None], a_re.shape)
    per_dir = [a_re[0], a_im[0], ld[0], a_re[1], a_im[1], ld[1]]
    zero = jnp.zeros_like(a_re[0])
    cols = jnp.stack(per_dir + [zero, zero], axis=-1)
    dup = lambda v: jnp.concatenate([v, v], axis=-1)
    rows = jnp.stack([dup(v) for v in per_dir + [zero, zero]], axis=1)
    bt_re, bt_im = b_re.transpose(0, 2, 1), b_im.transpose(0, 2, 1)
    ct_re, ct_im = c_re.transpose(0, 2, 1), c_im.transpose(0, 2, 1)
    per_g = lambda shape: pl.BlockSpec((1,) + tuple(shape[1:]), lambda gi: (gi, 0, 0))
    ins = (cols, rows, bt_re, bt_im, dup(bt_re), dup(bt_im), c_re, c_im, ct_re, ct_im)
    return pl.pallas_call(
        _s5_ops_kernel,
        grid=(g,),
        in_specs=[per_g(a.shape) for a in ins],
        out_specs=(per_g((g, S5_FLAT, S5_FLAT)), per_g((g, S5_FLAT, S5_STATE)),
                   per_g((g, S5_STATE, S5_FLAT)), per_g((g, 8, n2))),
        out_shape=(jax.ShapeDtypeStruct((g, S5_FLAT, S5_FLAT), BF16),
                   jax.ShapeDtypeStruct((g, S5_FLAT, S5_STATE), BF16),
                   jax.ShapeDtypeStruct((g, S5_STATE, S5_FLAT), BF16),
                   jax.ShapeDtypeStruct((g, 8, n2), F32)),
        scratch_shapes=[pltpu.VMEM((4, S5_N, 2 * L), F32),
                        pltpu.VMEM((2, 2, S5_N, 2 * L), F32),
                        pltpu.VMEM((2, 2, L, n2), F32),
                        pltpu.VMEM((H, 2, S5_P, 2 * L), F32)],
        compiler_params=_params("arbitrary"),
        name="s5_operators",
    )(*ins)


def _s5_kernel(u_ref, kb_ref, wloc_ref, wout_ref, ar_ref, y_ref, sin_ref, *, bsz, n_ctx_chunks):
    n_chunks = u_ref.shape[1] // bsz
    n2 = 2 * S5_N
    u = u_ref[0].astype(BF16)
    sloc = _dot(u, wloc_ref[0])
    ar = ar_ref[0]
    n_lat = n_chunks - n_ctx_chunks
    fwd_order = list(range(n_lat, n_chunks)) + list(range(n_lat))
    bwd_order = list(range(n_chunks - 1, n_lat - 1, -1)) + list(range(n_lat - 1, -1, -1))
    for d, order in ((0, fwd_order), (1, bwd_order)):
        a_re = ar[2 * d:2 * d + 1, :]
        a_im = ar[2 * d + 1:2 * d + 2, :]
        re_cols = slice(2 * d * n2, (2 * d + 1) * n2)
        im_cols = slice((2 * d + 1) * n2, (2 * d + 2) * n2)
        st_re = jnp.zeros((bsz, n2), F32)
        st_im = jnp.zeros((bsz, n2), F32)
        for c in order:
            rows = slice(c * bsz, (c + 1) * bsz)
            sin_ref[rows, re_cols] = st_re.astype(BF16)
            sin_ref[rows, im_cols] = st_im.astype(BF16)
            st_re, st_im = (a_re * st_re - a_im * st_im + sloc[rows, re_cols],
                            a_re * st_im + a_im * st_re + sloc[rows, im_cols])
    y_ref[0] = _dot(u, kb_ref[0]) + _dot(sin_ref[...], wout_ref[0])


def _s5_mix(ut, layer, kbig, w_loc, w_out, a_rows, bsz):
    g, rows, flat = ut.shape
    kern = functools.partial(_s5_kernel, bsz=bsz, n_ctx_chunks=CTX_LEN // S5_CHUNK)
    per_g = lambda a, off=layer * g: pl.BlockSpec((1,) + a.shape[1:], lambda gi: (off + gi, 0, 0))
    return pl.pallas_call(
        kern,
        grid=(g,),
        in_specs=[per_g(ut, 0)] + [per_g(a) for a in (kbig, w_loc, w_out, a_rows)],
        out_specs=pl.BlockSpec((1, rows, flat), lambda gi: (gi, 0, 0)),
        out_shape=jax.ShapeDtypeStruct((g, rows, flat), F32),
        scratch_shapes=[pltpu.VMEM((rows, S5_STATE), BF16)],
        compiler_params=_params("arbitrary"),
        name="s5_mix",
    )(ut, kbig, w_loc, w_out, a_rows)


def _s5_readout(y5_ref, zd_ref, d_ref, gw_ref, gb_ref, yd_ref):
    lo = lax.broadcasted_iota(jnp.int32, (1, 2 * S5_CHUNK), 1) < S5_CHUNK
    half_rows = (S5_GROUPS, 1, S5_HALF, 2 * S5_CHUNK)
    for j in range(zd_ref.shape[1] // (2 * S5_CHUNK)):
        rows = slice(j * 2 * S5_CHUNK, (j + 1) * 2 * S5_CHUNK)
        a = y5_ref[:, 2 * j, 0, :, :].reshape(S5_GROUPS * S5_HALF, 2 * S5_CHUNK)
        b = y5_ref[:, 2 * j + 1, 0, :, :].reshape(S5_GROUPS * S5_HALF, 2 * S5_CHUNK)
        first = jnp.where(lo, a, pltpu.roll(b, S5_CHUNK, 1))
        second = jnp.where(lo, pltpu.roll(a, S5_CHUNK, 1), b)
        yt = jnp.concatenate([first.reshape(half_rows), second.reshape(half_rows)],
                             axis=1).reshape(W_MIX, 2 * S5_CHUNK)
        y = _gelu(yt.T + d_ref[...] * zd_ref[0, rows, :])
        gate = _sigmoid(_dot(y.astype(BF16), gw_ref[...]) + gb_ref[...])
        yd_ref[rows, :] = (y * gate).astype(yd_ref.dtype)


def _out_ffn_kernel(x_ref, ya_ref, yb_ref, yc_ref, y5_ref, zd_ref, modb_ref, modc_ref, d5_ref,
                    gw_ref, gb_ref, wo_ref, l1w_ref, l1b_ref, wi_ref, wd_ref, l2w_ref, l2b_ref,
                    o_ref, yd_ref, x1_ref, h2_ref, a_ref, *, ctx_rows):
    tm = x_ref.shape[1]
    n_lat = tm - ctx_rows
    segs = [(0, n_lat, 0)] + ([(n_lat, tm, 1)] if ctx_rows else [])
    _s5_readout(y5_ref, zd_ref, d5_ref, gw_ref, gb_ref, yd_ref)
    y = (_dot(ya_ref[0], wo_ref[0, 0:W_MIX, :]) + _dot(yb_ref[0], wo_ref[0, W_MIX:2 * W_MIX, :]) +
         _dot(yc_ref[0], wo_ref[0, 2 * W_MIX:3 * W_MIX, :]) +
         _dot(yd_ref[...], wo_ref[0, 3 * W_MIX:4 * W_MIX, :]))
    for r0, r1, which in segs:
        g1 = _tile_mod(modb_ref, modc_ref, 2)[which]
        sh2 = _tile_mod(modb_ref, modc_ref, 3)[which]
        sc2 = _tile_mod(modb_ref, modc_ref, 4)[which]
        x1 = _layer_norm(ALPHA * x_ref[0, r0:r1, :] + g1 * y[r0:r1, :], l1w_ref[...], l1b_ref[...])
        x1_ref[r0:r1, :] = x1
        h2_ref[r0:r1, :] = (x1 * (1.0 + sc2) + sh2).astype(BF16)
    h2 = h2_ref[...]
    for kk in range(D_FF // FF_CHUNK):
        cols = slice(kk * FF_CHUNK, (kk + 1) * FF_CHUNK)
        gk = _dot(h2, wi_ref[0, :, cols])
        uk = _dot(h2, wi_ref[0, :, D_FF + kk * FF_CHUNK:D_FF + (kk + 1) * FF_CHUNK])
        a_ref[:, cols] = (_silu(gk) * uk).astype(BF16)
    f = _dot(a_ref[...], wd_ref[0])
    for r0, r1, which in segs:
        g2 = _tile_mod(modb_ref, modc_ref, 5)[which]
        o_ref[0, r0:r1, :] = _layer_norm(ALPHA * x1_ref[r0:r1, :] + g2 * f[r0:r1, :],
                                         l2w_ref[...], l2b_ref[...])


def _out_ffn(xcat, ya, yb, yc, y5, zd, mod, layer, s5_d, glu_w, glu_b, w_out, ln1_w, ln1_b, w_ffn_in,
             w_ffn_out, ln2_w, ln2_b, tm, n_out):
    bsz, s, d = xcat.shape
    ctx_rows = CTX_LEN if n_out == s else 0
    tok = lambda w: pl.BlockSpec((1, tm, w), lambda b, j: (b, j, 0))
    row = lambda a: a.reshape(1, -1)
    return pl.pallas_call(
        functools.partial(_out_ffn_kernel, ctx_rows=ctx_rows),
        grid=(bsz, n_out // tm),
        in_specs=[tok(d), tok(W_MIX), tok(W_MIX), tok(W_MIX),
                  pl.BlockSpec((S5_GROUPS, tm // S5_CHUNK, 1, S5_HALF, 2 * S5_CHUNK),
                               lambda b, j: (0, j, b, 0, 0)),
                  tok(W_MIX),
                  pl.BlockSpec((1, 6, d), lambda b, j: (b, 0, 0)),
                  pl.BlockSpec((1, 6, d), lambda b, j: (bsz, 0, 0)),
                  _const_spec((1, W_MIX)), _const_spec((W_MIX, W_MIX)), _const_spec((1, W_MIX)),
                  _layer_spec(w_out, layer), _const_spec((1, d)), _const_spec((1, d)),
                  _layer_spec(w_ffn_in, layer), _layer_spec(w_ffn_out, layer),
                  _const_spec((1, d)), _const_spec((1, d))],
        out_specs=tok(d),
        out_shape=jax.ShapeDtypeStruct((bsz, n_out, d), F32),
        scratch_shapes=[pltpu.VMEM((tm, W_MIX), BF16), pltpu.VMEM((tm, d), F32),
                        pltpu.VMEM((tm, d), BF16), pltpu.VMEM((tm, D_FF), BF16)],
        compiler_params=_params("arbitrary", "arbitrary"),
        name="out_ffn",
    )(xcat, ya, yb, yc, y5, zd, mod, mod, row(s5_d), glu_w.astype(BF16), row(glu_b), w_out,
      row(ln1_w), row(ln1_b), w_ffn_in, w_ffn_out, row(ln2_w), row(ln2_b))


_S5_LOCAL = np.arange(S5_P).reshape(S5_HALF, 2).T.reshape(-1)
_S5_ORDER = (np.arange(S5_GROUPS)[:, None] * S5_P + _S5_LOCAL[None, :]).reshape(-1)


def _reorder_w_in(w):
    off_b = 4 * W_MIX + N_GATES
    off_d = w.shape[-1] - W_MIX
    w_r = jnp.concatenate([w[..., :4 * W_MIX], w[..., off_b:off_d], w[..., off_d:][..., _S5_ORDER]], axis=-1)
    return w_r.astype(BF16), jnp.swapaxes(w[..., 4 * W_MIX:off_b], -1, -2).astype(BF16)


def kernel(x, c, ctx, c_ctx, w_mod, b_mod, w_in, mlstm_gate_bias, mlstm_norm_w, sgu_ln_w, sgu_ln_b, sgu_w, sgu_b, conv_w, conv_b, conv_ln_w, conv_ln_b, s5_a_re, s5_a_im, s5_log_dt, s5_b_re, s5_b_im, s5_c_re, s5_c_im, s5_d, s5_glu_w, s5_glu_b, w_out, ln1_w, ln1_b, w_ffn_in, w_ffn_out, ln2_w, ln2_b):
    bsz, seq, d = x.shape
    s = seq + CTX_LEN
    assert seq == SEQ_LATENT and ctx.shape[1] == CTX_LEN
    mod_rows = -(-(bsz + 1) // 8) * 8
    cvec = jnp.zeros((mod_rows, d), F32).at[:bsz].set(c).at[bsz].set(c_ctx)
    mod_all = _modulation(cvec, w_mod, b_mod).reshape(DEPTH, mod_rows, 6, d)
    nc5 = s // S5_CHUNK
    w_r, wg_t = _reorder_w_in(w_in)
    w_out_b = jnp.concatenate([w_out[:, :3 * W_MIX], w_out[:, 3 * W_MIX:][:, _S5_ORDER]], axis=1).astype(BF16)
    w_ffn_in_b, w_ffn_out_b = w_ffn_in.astype(BF16), w_ffn_out.astype(BF16)
    kbig, w_loc, w_so, a_rows = _s5_operators(
        s5_a_re, s5_a_im, s5_log_dt, s5_b_re[..., _S5_LOCAL], s5_b_im[..., _S5_LOCAL],
        s5_c_re[:, :, _S5_LOCAL, :], s5_c_im[:, :, _S5_LOCAL, :])
    for l in range(DEPTH):
        mod = mod_all[l]
        src = (x, ctx, 0, True) if l == 0 else (xcat, xcat, seq // TILE, False)
        outs = _in_projection(*src, mod, l, w_r, wg_t, sgu_w[l], sgu_b[l], sgu_ln_w[l], sgu_ln_b[l],
                              conv_w[l], conv_b[l], conv_ln_w[l], conv_ln_b[l])
        if l == 0:
            xcat, outs = outs[0], outs[1:]
        k, qt, vt, ot, yb, yc, zd, ut, gt = outs
        ya = _mlstm(k, qt, vt, ot, gt, mlstm_gate_bias[l], mlstm_norm_w[l])
        y5 = _s5_mix(ut.reshape(S5_GROUPS, nc5 * bsz, S5_FLAT), l, kbig, w_loc, w_so, a_rows, bsz)
        y5 = y5.reshape(S5_GROUPS, nc5, bsz, S5_HALF, 2 * S5_CHUNK)
        last = l == DEPTH - 1
        xcat = _out_ffn(xcat, ya, yb, yc, y5, zd, mod, l, s5_d[l][_S5_ORDER],
                        s5_glu_w[l][_S5_ORDER][:, _S5_ORDER], s5_glu_b[l][_S5_ORDER],
                        w_out_b, ln1_w[l], ln1_b[l], w_ffn_in_b, w_ffn_out_b, ln2_w[l], ln2_b[l],
                        tm=TM_LATENT if last else TM_MAIN, n_out=seq if last else s)
    return xcat
```

```python
import functools

import numpy as np
import jax
import jax.numpy as jnp
from jax import lax
from jax.experimental import pallas as pl
from jax.experimental.pallas import tpu as pltpu

F32 = jnp.float32
BF16 = jnp.bfloat16

DEPTH = 2
CTX_LEN = 256
SEQ_LATENT = 2048
GRID_W = 64
W_MIX = 256
HEADS = 4
HEAD_DIM = W_MIX // HEADS
SGU_GROUPS = 4
SGU_GD = W_MIX // SGU_GROUPS
SGU_CHUNK = 128
CONV_WIDTH = 31
CONV_PAD = CONV_WIDTH // 2
CONV_HALO = 32
S5_P = 16
S5_GROUPS = W_MIX // S5_P
S5_N = 64
S5_CHUNK = 64
S5_HALF = S5_P // 2
S5_FLAT = S5_P * S5_CHUNK
S5_STATE = 8 * S5_N
D_FF = 2816
FF_CHUNK = 256
ALPHA = (2 * DEPTH) ** 0.25
LN_EPS = 1e-5
TILE = 256
MLSTM_CHUNK = 256
TM_MAIN = 3 * TILE
TM_LATENT = 2 * TILE
CONV_ROWS = max((TILE // GRID_W) * (GRID_W + CONV_HALO), CTX_LEN + CONV_HALO)
N_GATES = 4 * HEADS
GATE_PAD = 128
C_O = 3 * W_MIX
C_B = 4 * W_MIX
C_C = C_B + 2 * W_MIX
C_D = C_C + 2 * W_MIX
C_G = C_D + W_MIX
VMEM_LIMIT = 56 * 1024 * 1024


def _dot(a, b):
    return jnp.dot(a, b, preferred_element_type=F32)


def _dot_nt(a, b):
    return lax.dot_general(a, b, (((1,), (1,)), ((), ())), preferred_element_type=F32)


def _split3(x):
    hi = x.astype(BF16)
    r = x - hi.astype(F32)
    mid = r.astype(BF16)
    lo = (r - mid.astype(F32)).astype(BF16)
    return hi, mid, lo


def _layer_norm(x, w, b):
    mu = jnp.mean(x, axis=-1, keepdims=True)
    d = x - mu
    var = jnp.mean(d * d, axis=-1, keepdims=True)
    return d * lax.rsqrt(var + LN_EPS) * w + b


def _sigmoid(x):
    return jax.nn.sigmoid(x)


def _silu(x):
    return x * jax.nn.sigmoid(x)


def _gelu(x):
    return jax.nn.gelu(x, approximate=True)


def _log_sigmoid(x):
    return jnp.minimum(x, 0.0) - jnp.log1p(jnp.exp(-jnp.abs(x)))


def _params(*sem):
    return pltpu.CompilerParams(dimension_semantics=sem, vmem_limit_bytes=VMEM_LIMIT)


def _const_spec(shape):
    nd = len(shape)
    return pl.BlockSpec(shape, lambda *_: (0,) * nd, pipeline_mode=pl.Buffered(1))


def _layer_spec(stacked, l):
    nd = stacked.ndim
    return pl.BlockSpec((1,) + stacked.shape[1:], lambda *_: (l,) + (0,) * (nd - 1),
                        pipeline_mode=pl.Buffered(1))


def _mod_kernel(c_ref, w_ref, b_ref, o_ref):
    s = _silu(c_ref[...])
    s_hi = s.astype(BF16)
    s_lo = (s - s_hi.astype(F32)).astype(BF16)
    w = w_ref[0]
    w_hi = w.astype(BF16)
    w_lo = (w - w_hi.astype(F32)).astype(BF16)
    o_ref[0] = _dot(s_hi, w_hi) + _dot(s_hi, w_lo) + _dot(s_lo, w_hi) + b_ref[0]


def _modulation(cvec, w_mod, b_mod):
    depth, d, n = w_mod.shape
    rows = cvec.shape[0]
    tn = 1024
    return pl.pallas_call(
        _mod_kernel,
        grid=(depth, n // tn),
        in_specs=[pl.BlockSpec((rows, d), lambda l, j: (0, 0)),
                  pl.BlockSpec((1, d, tn), lambda l, j: (l, 0, j)),
                  pl.BlockSpec((1, 1, tn), lambda l, j: (l, 0, j))],
        out_specs=pl.BlockSpec((1, rows, tn), lambda l, j: (l, 0, j)),
        out_shape=jax.ShapeDtypeStruct((depth, rows, n), F32),
        compiler_params=_params("arbitrary", "arbitrary"),
        name="modulation",
    )(cvec, w_mod, b_mod.reshape(depth, 1, n))


def _tile_mod(modb_ref, modc_ref, i):
    last = pl.program_id(1) == pl.num_programs(1) - 1
    row_b = modb_ref[0, i:i + 1, :]
    return row_b, jnp.where(last, modc_ref[0, i:i + 1, :], row_b)


def _sgu_slab(zb, sgw_ref, sgb_ref, slnw_ref, slnb_ref):
    u = _gelu(zb[:, 0:W_MIX])
    vn = _layer_norm(_gelu(zb[:, W_MIX:2 * W_MIX]), slnw_ref[...], slnb_ref[...]).astype(BF16)
    lane_grp = lax.broadcasted_iota(jnp.int32, (1, W_MIX), 1) // SGU_GD
    out = []
    for cl in range(TILE // SGU_CHUNK):
        vc = vn[cl * SGU_CHUNK:(cl + 1) * SGU_CHUNK, :]
        acc = sgb_ref[...]
        for gi in range(SGU_GROUPS):
            acc = acc + _dot(sgw_ref[gi], jnp.where(lane_grp == gi, vc, jnp.zeros_like(vc)))
        out.append(acc)
    return u * jnp.concatenate(out, axis=0)


def _conv_slab(a, seg_len, cw_ref, sh_ref):
    n_seg = TILE // seg_len
    stride = seg_len + CONV_HALO
    zeros16 = jnp.zeros((16, W_MIX), F32)
    for sg in range(n_seg):
        base = sg * stride
        seg = a[sg * seg_len:(sg + 1) * seg_len, :]
        for r in range(8):
            sh_ref[r, pl.ds(base, 16), :] = zeros16
            sh_ref[r, pl.ds(base + seg_len + 8, 16), :] = zeros16
            sh_ref[r, pl.ds(base + 16 - r, seg_len), :] = seg
    out = []
    for sg in range(n_seg):
        base = sg * stride
        acc = jnp.zeros((seg_len, W_MIX), F32)
        for t in range(CONV_WIDTH):
            off = t - CONV_PAD
            r = off % 8
            acc = acc + cw_ref[t:t + 1, :] * sh_ref[r, pl.ds(base + (off - r) + 16, seg_len), :]
        out.append(acc)
    return out[0] if n_seg == 1 else jnp.concatenate(out, axis=0)


def _emit_s5_rows(zdt, ut_ref, chunk0):
    n_tok = zdt.shape[1]
    lo = lax.broadcasted_iota(jnp.int32, (1, 2 * S5_CHUNK), 1) < S5_CHUNK
    halves = zdt.reshape(S5_GROUPS, 2, S5_HALF, n_tok)
    first = halves[:, 0].reshape(S5_GROUPS * S5_HALF, n_tok)
    second = halves[:, 1].reshape(S5_GROUPS * S5_HALF, n_tok)
    for j in range(n_tok // (2 * S5_CHUNK)):
        a = first[:, j * 2 * S5_CHUNK:(j + 1) * 2 * S5_CHUNK]
        b = second[:, j * 2 * S5_CHUNK:(j + 1) * 2 * S5_CHUNK]
        pieces = (jnp.where(lo, a, pltpu.roll(b, S5_CHUNK, 1)),
                  jnp.where(lo, pltpu.roll(a, S5_CHUNK, 1), b))
        for cc, piece in enumerate(pieces):
            ut_ref[:, chunk0 + 2 * j + cc, 0, :, :] = (
                piece.reshape(S5_GROUPS, S5_HALF, 2 * S5_CHUNK).astype(BF16))


def _inproj_kernel(*refs, n_slabs, emit_stream):
    x_refs = refs[:n_slabs]
    (ctx_ref, modb_ref, modc_ref, w_ref, wgt_ref, sgw_ref, sgb_ref, slnw_ref, slnb_ref,
     cw_ref, cb_ref, clnw_ref, clnb_ref) = refs[n_slabs:n_slabs + 13]
    outs = refs[n_slabs + 13:]
    if emit_stream:
        xo_ref, outs = outs[0], outs[1:]
    k_ref, qt_ref, vt_ref, ot_ref, yb_ref, yc_ref, zd_ref, ut_ref, gt_ref, h_ref, sh_ref = outs
    last = pl.program_id(1) == pl.num_programs(1) - 1
    sh_b, sh_c = _tile_mod(modb_ref, modc_ref, 0)
    sc_b, sc_c = _tile_mod(modb_ref, modc_ref, 1)
    for r in range(n_slabs):
        rows = slice(r * TILE, (r + 1) * TILE)
        x = x_refs[r][0]
        sc, sh = sc_b, sh_b
        if r == n_slabs - 1:
            x = jnp.where(last, ctx_ref[0], x)
            sc, sh = sc_c, sh_c
        if emit_stream:
            xo_ref[0, rows, :] = x
        h_ref[rows, :] = (x * (1.0 + sc) + sh).astype(BF16)
    for r in range(n_slabs):
        rows = slice(r * TILE, (r + 1) * TILE)
        h = h_ref[rows, :]
        qt_ref[0, :, rows] = _dot(h, w_ref[0, :, 0:W_MIX]).T.astype(BF16)
        k_ref[0, rows, :] = (_dot(h, w_ref[0, :, W_MIX:2 * W_MIX]) * (HEAD_DIM ** -0.5)).astype(BF16)
        vt_ref[0, :, rows] = _dot(h, w_ref[0, :, 2 * W_MIX:3 * W_MIX]).T.astype(BF16)
        ot_ref[0, :, rows] = _dot(h, w_ref[0, :, C_O:C_B]).T
        zd = _dot(h, w_ref[0, :, C_D:C_G])
        zd_ref[0, rows, :] = zd
        _emit_s5_rows(zd.T, ut_ref, r * (TILE // S5_CHUNK))
        gt_ref[0, :, rows] = _dot_nt(wgt_ref[0], h)
        yb = _sgu_slab(_dot(h, w_ref[0, :, C_B:C_C]), sgw_ref, sgb_ref, slnw_ref, slnb_ref)
        yb_ref[0, rows, :] = yb.astype(yb_ref.dtype)
        zc = _dot(h, w_ref[0, :, C_C:C_D])
        a = zc[:, 0:W_MIX] * _sigmoid(zc[:, W_MIX:2 * W_MIX])

        def conv_out(seg_len, a=a, rows=rows, r=r):
            acc = _conv_slab(a, seg_len, cw_ref, sh_ref.at[r])
            yc = _silu(_layer_norm(acc + cb_ref[...], clnw_ref[...], clnb_ref[...]))
            yc_ref[0, rows, :] = yc.astype(yc_ref.dtype)

        if r < n_slabs - 1:
            conv_out(GRID_W)
        else:
            pl.when(jnp.logical_not(last))(functools.partial(conv_out, GRID_W))
            pl.when(last)(functools.partial(conv_out, CTX_LEN))


def _in_projection(lat, ctx, ctx_block, emit_stream, mod, layer, w_r, wg_t, sgu_w, sgu_b, sgu_ln_w,
                   sgu_ln_b, conv_w, conv_b, conv_ln_w, conv_ln_b):
    bsz, _, d = lat.shape
    s = SEQ_LATENT + CTX_LEN
    tm = TM_MAIN
    n_slabs = tm // TILE
    last_lat = SEQ_LATENT // TILE - 1
    slab = lambda r: pl.BlockSpec((1, TILE, d), lambda b, j: (b, jnp.minimum(n_slabs * j + r, last_lat), 0))
    nc5 = s // S5_CHUNK
    cpt = tm // S5_CHUNK
    bias_full = jnp.repeat(sgu_b.T, SGU_GD, axis=1)
    row = lambda a: a.reshape(1, W_MIX)
    tok = lambda w: pl.BlockSpec((1, tm, w), lambda b, j: (b, j, 0))
    chan = lambda c: pl.BlockSpec((1, c, tm), lambda b, j: (b, 0, j))
    out_shape = (
        jax.ShapeDtypeStruct((bsz, s, W_MIX), BF16),
        jax.ShapeDtypeStruct((bsz, W_MIX, s), BF16),
        jax.ShapeDtypeStruct((bsz, W_MIX, s), BF16),
        jax.ShapeDtypeStruct((bsz, W_MIX, s), F32),
        jax.ShapeDtypeStruct((bsz, s, W_MIX), BF16),
        jax.ShapeDtypeStruct((bsz, s, W_MIX), BF16),
        jax.ShapeDtypeStruct((bsz, s, W_MIX), F32),
        jax.ShapeDtypeStruct((S5_GROUPS, nc5, bsz, S5_HALF, 2 * S5_CHUNK), BF16),
        jax.ShapeDtypeStruct((bsz, N_GATES, s), F32),
    )
    out_specs = (
        tok(W_MIX), chan(W_MIX), chan(W_MIX), chan(W_MIX), tok(W_MIX), tok(W_MIX), tok(W_MIX),
        pl.BlockSpec((S5_GROUPS, cpt, 1, S5_HALF, 2 * S5_CHUNK), lambda b, j: (0, j, b, 0, 0)),
        chan(N_GATES),
    )
    if emit_stream:
        out_shape = (jax.ShapeDtypeStruct((bsz, s, d), F32),) + out_shape
        out_specs = (tok(d),) + out_specs
    return pl.pallas_call(
        functools.partial(_inproj_kernel, n_slabs=n_slabs, emit_stream=emit_stream),
        grid=(bsz, s // tm),
        in_specs=[slab(r) for r in range(n_slabs)] + [
                  pl.BlockSpec((1, TILE, d), lambda b, j: (b, ctx_block, 0)),
                  pl.BlockSpec((1, 6, d), lambda b, j: (b, 0, 0)),
                  pl.BlockSpec((1, 6, d), lambda b, j: (bsz, 0, 0)),
                  _layer_spec(w_r, layer),
                  _layer_spec(wg_t, layer),
                  _const_spec((SGU_GROUPS, SGU_CHUNK, SGU_CHUNK)),
                  _const_spec((SGU_CHUNK, W_MIX)),
                  _const_spec((1, W_MIX)), _const_spec((1, W_MIX)),
                  _const_spec((CONV_WIDTH, W_MIX)), _const_spec((1, W_MIX)),
                  _const_spec((1, W_MIX)), _const_spec((1, W_MIX))],
        out_specs=out_specs,
        out_shape=out_shape,
        scratch_shapes=[pltpu.VMEM((tm, d), BF16),
                        pltpu.VMEM((tm // TILE, 8, CONV_ROWS, W_MIX), F32)],
        compiler_params=_params("arbitrary", "arbitrary"),
        name="in_projection",
    )(*([lat] * n_slabs), ctx, mod, mod, w_r, wg_t, sgu_w.astype(BF16), bias_full, row(sgu_ln_w),
      row(sgu_ln_b), conv_w, row(conv_b), row(conv_ln_w), row(conv_ln_b))


def _mlstm_kernel(k_ref, qt_ref, vt_ref, ot_ref, gt_ref, gbt_ref, nw_ref, y_ref,
                  ht_ref, src_ref, lgt_ref, cumt_ref, pm_ref, c_ref, n_ref, m_ref, *, need_ctx):
    L = MLSTM_CHUNK
    n_chunks = k_ref.shape[1] // L
    row = lax.broadcasted_iota(jnp.int32, (L, L), 0)
    col = lax.broadcasted_iota(jnp.int32, (L, L), 1)
    tri = (col <= row).astype(BF16)
    lane_g = lax.broadcasted_iota(jnp.int32, (1, GATE_PAD), 1)
    sub_g = lax.broadcasted_iota(jnp.int32, (N_GATES, 1), 0)
    is_f_sub = (sub_g % 8) >= HEADS
    lane_w = lax.broadcasted_iota(jnp.int32, (1, W_MIX), 1) // HEAD_DIM
    sub_w = lax.broadcasted_iota(jnp.int32, (W_MIX, 1), 0) // HEAD_DIM
    bd_mask = sub_w == lane_w
    n_mask = sub_g == lane_w
    row4 = lax.broadcasted_iota(jnp.int32, (HEADS * L, L), 0) % L
    col4 = lax.broadcasted_iota(jnp.int32, (HEADS * L, L), 1)
    mask4 = (row4 <= col4, row4 >= col4)

    for c in range(n_chunks):
        cols = slice(c * L, (c + 1) * L)
        grawt = gt_ref[0, :, cols] + gbt_ref[...]
        lgt = jnp.where(is_f_sub, _log_sigmoid(grawt), grawt)
        lgt_ref[:, cols] = lgt
        hi, mid, lo = _split3(lgt)
        pret = _dot_nt(hi, tri) + _dot_nt(mid, tri) + _dot_nt(lo, tri)
        cumt = jnp.where(sub_g >= 8, pret[:, L - 1:L] - pret + lgt, pret)
        cumt_ref[:, cols] = cumt
        srct = jnp.where(is_f_sub, -jnp.inf, lgt - pltpu.roll(cumt, N_GATES - HEADS, 0))
        pm_ref[:, cols] = srct
        padded = jnp.concatenate([srct, jnp.zeros((GATE_PAD - N_GATES, L), F32)], axis=0)
        src_ref[cols, :] = padded.T
    pos = lax.broadcasted_iota(jnp.int32, (1, pm_ref.shape[1]), 1) % L
    pm = pm_ref[...]
    sh = 1
    while sh < L:
        fwd_v = jnp.where(pos >= sh, pltpu.roll(pm, sh, 1), -jnp.inf)
        bwd_v = jnp.where(pos < L - sh, pltpu.roll(pm, pm.shape[1] - sh, 1), -jnp.inf)
        pm = jnp.maximum(pm, jnp.where(sub_g >= 8, bwd_v, fwd_v))
        sh *= 2
    pm_ref[...] = pm

    def chunk(c, d, emit_h=True):
        r0 = pl.multiple_of(c * L, L)
        k = k_ref[0, pl.ds(r0, L), :]
        qt = qt_ref[0, :, pl.ds(r0, L)]
        vt = vt_ref[0, :, pl.ds(r0, L)]
        src_cols = src_ref[pl.ds(r0, L), :]
        lgt = lgt_ref[:, pl.ds(r0, L)]
        cumt = cumt_ref[:, pl.ds(r0, L)]
        pm = pm_ref[:, pl.ds(r0, L)]
        tot = cumt[:, L - 1:L] if d == 0 else cumt[:, 0:1]
        m_vec = m_ref[d]
        c_old = c_ref[d]
        n_old = n_ref[d]
        qc = _dot(c_old.astype(BF16), qt)
        qn = _dot(n_old.astype(BF16), qt)
        e_slabs = []
        e_small = jnp.zeros((N_GATES, L), F32)
        decay_row = jnp.zeros((1, W_MIX), F32)
        m_new_vec = m_vec
        srcs, col_terms, khs, w_inters, floors = [], [], [], [], []
        for h in range(HEADS):
            ic = d * 8 + h
            fc = d * 8 + HEADS + h
            m_h = m_vec[:, fc:fc + 1]
            cct = cumt[fc:fc + 1, :]
            inter = cct + m_h
            m_t = jnp.maximum(inter, cct + pm[ic:ic + 1, :])
            w_inters.append(jnp.exp(inter - m_t))
            floors.append(jnp.exp(-m_t))
            col_terms.append(jnp.broadcast_to(cct - m_t, (L, L)))
            srcs.append(src_cols[:, ic:ic + 1])
            khs.append(jnp.where(lane_w == h, k, jnp.zeros_like(k)))
        arg = jnp.concatenate(srcs, axis=0) + jnp.concatenate(col_terms, axis=0)
        arg = jnp.where(mask4[d], arg, -jnp.inf)
        sc = _dot(jnp.concatenate(khs, axis=0), qt) * jnp.exp(arg)
        scb = sc.astype(BF16)
        for h in range(HEADS):
            ic = d * 8 + h
            fc = d * 8 + HEADS + h
            head_lanes = lane_w == h
            sl = slice(h * HEAD_DIM, (h + 1) * HEAD_DIM)
            hrows = slice(h * L, (h + 1) * L)
            m_h = m_vec[:, fc:fc + 1]
            cct = cumt[fc:fc + 1, :]
            if emit_h:
                den = jnp.sum(sc[hrows, :], axis=0, keepdims=True) + w_inters[h] * qn[h:h + 1, :]
                num = _dot(vt[sl, :], scb[hrows, :]) + w_inters[h] * qc[sl, :]
                ht_ref[d, sl, pl.ds(r0, L)] = num / jnp.maximum(jnp.abs(den), floors[h])
            tot_h = tot[fc:fc + 1, :]
            logu = tot_h - cct + lgt[ic:ic + 1, :]
            m_new = jnp.maximum(tot_h + m_h, jnp.max(logu, axis=1, keepdims=True))
            e_h = jnp.exp(logu - m_new)
            e_slabs.append(jnp.broadcast_to(e_h, (HEAD_DIM, L)))
            e_small = jnp.where(sub_g == h, e_h, e_small)
            decay_row = jnp.where(head_lanes, jnp.exp(tot_h + m_h - m_new), decay_row)
            m_new_vec = jnp.where(lane_g == fc, m_new, m_new_vec)
        vet = (vt.astype(F32) * jnp.concatenate(e_slabs, axis=0)).astype(BF16)
        c_ref[d] = decay_row * c_old + jnp.where(bd_mask, _dot(vet, k), 0.0)
        n_ref[d] = decay_row * n_old + jnp.where(n_mask, _dot(e_small.astype(BF16), k), 0.0)
        m_ref[d] = m_new_vec

    c_ref[...] = jnp.zeros_like(c_ref)
    n_ref[...] = jnp.zeros_like(n_ref)
    m_ref[...] = jnp.zeros_like(m_ref)
    n_lat = n_chunks - CTX_LEN // L
    for j in range(n_chunks - n_lat):
        chunk(n_lat + j, 0, need_ctx)
        chunk(n_chunks - 1 - j, 1, need_ctx)

    def scan_body(i, carry):
        chunk(i, 0)
        chunk(n_lat - 1 - i, 1)
        return carry

    lax.fori_loop(0, n_lat, scan_body, 0)

    def readout(c, carry):
        r0 = pl.multiple_of(c * L, L)
        slabs = []
        for h in range(HEADS):
            sl = slice(h * HEAD_DIM, (h + 1) * HEAD_DIM)
            hs = ht_ref[0, sl, pl.ds(r0, L)] + ht_ref[1, sl, pl.ds(r0, L)]
            mu = jnp.mean(hs, axis=0, keepdims=True)
            dl = hs - mu
            var = jnp.mean(dl * dl, axis=0, keepdims=True)
            hn = dl * lax.rsqrt(var + LN_EPS) * nw_ref[sl, :]
            slabs.append(hn * _sigmoid(ot_ref[0, sl, pl.ds(r0, L)]))
        y_ref[0, pl.ds(r0, L), :] = jnp.concatenate(slabs, axis=0).T.astype(y_ref.dtype)
        return carry

    lax.fori_loop(0, n_chunks if need_ctx else n_lat, readout, 0)
    if not need_ctx:
        y_ref[0, n_lat * L:n_chunks * L, :] = jnp.zeros(((n_chunks - n_lat) * L, W_MIX), y_ref.dtype)


def _mlstm(k, qt, vt, ot, gt, gate_bias, norm_w, need_ctx):
    bsz, s, _ = k.shape
    gbt = gate_bias.reshape(N_GATES, 1)
    nw_full = jnp.broadcast_to(norm_w.reshape(W_MIX, 1), (W_MIX, MLSTM_CHUNK))
    tok = lambda w: pl.BlockSpec((1, s, w), lambda b: (b, 0, 0))
    chan = lambda c: pl.BlockSpec((1, c, s), lambda b: (b, 0, 0))
    return pl.pallas_call(
        functools.partial(_mlstm_kernel, need_ctx=need_ctx),
        grid=(bsz,),
        in_specs=[tok(W_MIX), chan(W_MIX), chan(W_MIX), chan(W_MIX), chan(N_GATES),
                  pl.BlockSpec((N_GATES, 1), lambda b: (0, 0)),
                  pl.BlockSpec((W_MIX, MLSTM_CHUNK), lambda b: (0, 0))],
        out_specs=tok(W_MIX),
        out_shape=jax.ShapeDtypeStruct((bsz, s, W_MIX), BF16),
        scratch_shapes=[pltpu.VMEM((2, W_MIX, s), F32),
                        pltpu.VMEM((s, GATE_PAD), F32),
                        pltpu.VMEM((N_GATES, s), F32),
                        pltpu.VMEM((N_GATES, s), F32),
                        pltpu.VMEM((N_GATES, s), F32),
                        pltpu.VMEM((2, W_MIX, W_MIX), F32),
                        pltpu.VMEM((2, N_GATES, W_MIX), F32),
                        pltpu.VMEM((2, 1, GATE_PAD), F32)],
        compiler_params=_params("arbitrary"),
        name="mlstm",
    )(k, qt, vt, ot, gt, gbt, nw_full)


def _cpow(ar, ai, ld, k):
    dt = jnp.exp(ld)
    mag = jnp.exp(k * (ar * dt))
    ang = k * (ai * dt)
    return mag * jnp.cos(ang), mag * jnp.sin(ang)


def _zoh_coef(ar, ai, ld):
    er, ei = _cpow(ar, ai, ld, 1.0)
    den = ar * ar + ai * ai
    return ((er - 1.0) * ar + ei * ai) / den, (ei * ar - (er - 1.0) * ai) / den


def _dot_x3(a, b):
    a_hi, a_mid, _ = _split3(a)
    b_hi, b_mid, _ = _split3(b)
    return _dot(a_hi, b_hi) + _dot(a_hi, b_mid) + _dot(a_mid, b_hi)


def _tap_table(lag, col, cf):
    fwd = lag >= 0
    k = jnp.abs(lag).astype(F32)
    sel = lambda a, b: jnp.where(fwd, a, b)
    pre, pim = _cpow(sel(col(0), col(3)), sel(col(1), col(4)), sel(col(2), col(5)), k)
    cre, cim = sel(cf[0][0], cf[1][0]), sel(cf[0][1], cf[1][1])
    both = lag == 0
    return (pre * cre - pim * cim + jnp.where(both, cf[1][0], 0.0),
            pre * cim + pim * cre + jnp.where(both, cf[1][1], 0.0))


def _s5_ops_kernel(cols_ref, rows_ref, btre_ref, btim_ref, bdre_ref, bdim_ref, cre_ref, cim_ref,
                   ctre_ref, ctim_ref, kb_ref, wloc_ref, wout_ref, ar_ref,
                   mt_ref, pt_ref, pw_ref, v_ref):
    L, H, N = S5_CHUNK, S5_HALF, S5_N
    lane = lax.broadcasted_iota(jnp.int32, (1, 2 * L), 1)
    lo = lane < L

    def tables():
        col = lambda k: cols_ref[0, :, k:k + 1]
        row = lambda k: rows_ref[0, k:k + 1, :]
        cf = [_zoh_coef(col(3 * d), col(3 * d + 1), col(3 * d + 2)) for d in range(2)]
        mt_ref[0], mt_ref[1] = _tap_table(jnp.where(lo, lane, lane - 2 * L), col, cf)
        mt_ref[2], mt_ref[3] = _tap_table(lane - L, col, cf)
        tt = lane % L
        ss = lax.broadcasted_iota(jnp.int32, (L, 1), 0)
        for d in range(2):
            kk = (tt + 1 if d == 0 else L - tt).astype(F32)
            pt_ref[d, 0], pt_ref[d, 1] = _cpow(col(3 * d), col(3 * d + 1), col(3 * d + 2), kk)
            ks = (L - 1 - ss if d == 0 else ss).astype(F32)
            qre, qim = _cpow(row(3 * d), row(3 * d + 1), row(3 * d + 2), ks)
            fre, fim = _zoh_coef(row(3 * d), row(3 * d + 1), row(3 * d + 2))
            pw_ref[d, 0] = qre * fre - qim * fim
            pw_ref[d, 1] = qre * fim + qim * fre
            ar_ref[0, 2 * d:2 * d + 1, :], ar_ref[0, 2 * d + 1:2 * d + 2, :] = _cpow(
                row(3 * d), row(3 * d + 1), row(3 * d + 2), float(L))
        ar_ref[0, 4:8, :] = jnp.zeros((4, 2 * N), F32)

    def lane_tile(i):
        tile = slice(i * 2 * L, (i + 1) * 2 * L)
        for half in range(2):
            p = i + half * H
            cre = cre_ref[0, p:p + 1, :]
            cim = cim_ref[0, p:p + 1, :]
            g_re = cre * btre_ref[0] - cim * btim_ref[0]
            g_im = cre * btim_ref[0] + cim * btre_ref[0]
            v_ref[i, half] = _dot_x3(g_re, mt_ref[2 * half]) - _dot_x3(g_im, mt_ref[2 * half + 1])
        for q in range(S5_P):
            r0 = (q % H) * 2 * L + (q // H) * L
            toe = [pltpu.roll(jnp.broadcast_to(v_ref[i, half, q:q + 1, :], (L, 2 * L)), 0, 1,
                              stride=1, stride_axis=0) for half in range(2)]
            kb_ref[0, r0:r0 + L, tile] = jnp.where(lo, toe[0], toe[1]).astype(BF16)
        ccre = jnp.where(lo, ctre_ref[0, :, i:i + 1], ctre_ref[0, :, i + H:i + H + 1])
        ccim = jnp.where(lo, ctim_ref[0, :, i:i + 1], ctim_ref[0, :, i + H:i + H + 1])
        pad = jnp.zeros((N, 2 * L), BF16)
        for d in range(2):
            pre, pim = pt_ref[d, 0], pt_ref[d, 1]
            base = d * 4 * N
            wout_ref[0, base:base + N, tile] = (ccre * pre - ccim * pim).astype(BF16)
            wout_ref[0, base + N:base + 2 * N, tile] = pad
            wout_ref[0, base + 2 * N:base + 3 * N, tile] = (-(ccre * pim + ccim * pre)).astype(BF16)
            wout_ref[0, base + 3 * N:base + 4 * N, tile] = pad
        for half in range(2):
            q = i + half * H
            bre = bdre_ref[0, q:q + 1, :]
            bim = bdim_ref[0, q:q + 1, :]
            rows = slice(i * 2 * L + half * L, i * 2 * L + (half + 1) * L)
            for d in range(2):
                base = d * 4 * N
                re = pw_ref[d, 0] * bre - pw_ref[d, 1] * bim
                im = pw_ref[d, 0] * bim + pw_ref[d, 1] * bre
                wloc_ref[0, rows, base:base + 2 * N] = jnp.where(lo, re, 0.0).astype(BF16)
                wloc_ref[0, rows, base + 2 * N:base + 4 * N] = jnp.where(lo, im, 0.0).astype(BF16)

    tables()
    for i in range(H):
        lane_tile(i)


def _s5_operators(a_re, a_im, log_dt, b_re, b_im, c_re, c_im):
    L, H, n2 = S5_CHUNK, S5_HALF, 2 * S5_N
    g = a_re.shape[0] * S5_GROUPS
    by_dir = lambda a: jnp.moveaxis(a, 1, 0).reshape((2, g) + a.shape[3:])
    a_re, a_im, log_dt = by_dir(a_re), by_dir(a_im), by_dir(log_dt)
    b_re, b_im, c_re, c_im = (a.reshape((g,) + a.shape[2:]) for a in (b_re, b_im, c_re, c_im))
    ld = jnp.broadcast_to(log_dt[:, :, None], a_re.shape)
    per_dir = [a_re[0], a_im[0], ld[0], a_re[1], a_im[1], ld[1]]
    zero = jnp.zeros_like(a_re[0])
    cols = jnp.stack(per_dir + [zero, zero], axis=-1)
    dup = lambda v: jnp.concatenate([v, v], axis=-1)
    rows = jnp.stack([dup(v) for v in per_dir + [zero, zero]], axis=1)
    bt_re, bt_im = b_re.transpose(0, 2, 1), b_im.transpose(0, 2, 1)
    ct_re, ct_im = c_re.transpose(0, 2, 1), c_im.transpose(0, 2, 1)
    per_g = lambda shape: pl.BlockSpec((1,) + tuple(shape[1:]), lambda gi: (gi, 0, 0))
    ins = (cols, rows, bt_re, bt_im, dup(bt_re), dup(bt_im), c_re, c_im, ct_re, ct_im)
    return pl.pallas_call(
        _s5_ops_kernel,
        grid=(g,),
        in_specs=[per_g(a.shape) for a in ins],
        out_specs=(per_g((g, S5_FLAT, S5_FLAT)), per_g((g, S5_FLAT, S5_STATE)),
                   per_g((g, S5_STATE, S5_FLAT)), per_g((g, 8, n2))),
        out_shape=(jax.ShapeDtypeStruct((g, S5_FLAT, S5_FLAT), BF16),
                   jax.ShapeDtypeStruct((g, S5_FLAT, S5_STATE), BF16),
                   jax.ShapeDtypeStruct((g, S5_STATE, S5_FLAT), BF16),
                   jax.ShapeDtypeStruct((g, 8, n2), F32)),
        scratch_shapes=[pltpu.VMEM((4, S5_N, 2 * L), F32),
                        pltpu.VMEM((2, 2, S5_N, 2 * L), F32),
                        pltpu.VMEM((2, 2, L, n2), F32),
                        pltpu.VMEM((H, 2, S5_P, 2 * L), F32)],
        compiler_params=_params("arbitrary"),
        name="s5_operators",
    )(*ins)


def _s5_kernel(u_ref, kb_ref, wloc_ref, wout_ref, ar_ref, y_ref, sin_ref, *, bsz, n_ctx_chunks):
    n_chunks = u_ref.shape[1] // bsz
    n2 = 2 * S5_N
    u = u_ref[0]
    sloc = _dot(u, wloc_ref[0])
    ar = ar_ref[0]
    n_lat = n_chunks - n_ctx_chunks
    fwd_order = list(range(n_lat, n_chunks)) + list(range(n_lat))
    bwd_order = list(range(n_chunks - 1, n_lat - 1, -1)) + list(range(n_lat - 1, -1, -1))
    for d, order in ((0, fwd_order), (1, bwd_order)):
        a_re = ar[2 * d:2 * d + 1, :]
        a_im = ar[2 * d + 1:2 * d + 2, :]
        re_cols = slice(2 * d * n2, (2 * d + 1) * n2)
        im_cols = slice((2 * d + 1) * n2, (2 * d + 2) * n2)
        st_re = jnp.zeros((bsz, n2), F32)
        st_im = jnp.zeros((bsz, n2), F32)
        for c in order:
            rows = slice(c * bsz, (c + 1) * bsz)
            sin_ref[rows, re_cols] = st_re.astype(BF16)
            sin_ref[rows, im_cols] = st_im.astype(BF16)
            st_re, st_im = (a_re * st_re - a_im * st_im + sloc[rows, re_cols],
                            a_re * st_im + a_im * st_re + sloc[rows, im_cols])
    y_ref[0] = _dot(u, kb_ref[0]) + _dot(sin_ref[...], wout_ref[0])


def _s5_mix(ut, layer, kbig, w_loc, w_out, a_rows, bsz):
    g, rows, flat = ut.shape
    kern = functools.partial(_s5_kernel, bsz=bsz, n_ctx_chunks=CTX_LEN // S5_CHUNK)
    per_g = lambda a, off=layer * g: pl.BlockSpec((1,) + a.shape[1:], lambda gi: (off + gi, 0, 0))
    return pl.pallas_call(
        kern,
        grid=(g,),
        in_specs=[per_g(ut, 0)] + [per_g(a) for a in (kbig, w_loc, w_out, a_rows)],
        out_specs=pl.BlockSpec((1, rows, flat), lambda gi: (gi, 0, 0)),
        out_shape=jax.ShapeDtypeStruct((g, rows, flat), F32),
        scratch_shapes=[pltpu.VMEM((rows, S5_STATE), BF16)],
        compiler_params=_params("arbitrary"),
        name="s5_mix",
    )(ut, kbig, w_loc, w_out, a_rows)


def _s5_readout(y5_ref, zd_ref, d_ref, gw_ref, gb_ref, yd_ref):
    lo = lax.broadcasted_iota(jnp.int32, (1, 2 * S5_CHUNK), 1) < S5_CHUNK
    half_rows = (S5_GROUPS, 1, S5_HALF, 2 * S5_CHUNK)
    for j in range(zd_ref.shape[1] // (2 * S5_CHUNK)):
        rows = slice(j * 2 * S5_CHUNK, (j + 1) * 2 * S5_CHUNK)
        a = y5_ref[:, 2 * j, 0, :, :].reshape(S5_GROUPS * S5_HALF, 2 * S5_CHUNK)
        b = y5_ref[:, 2 * j + 1, 0, :, :].reshape(S5_GROUPS * S5_HALF, 2 * S5_CHUNK)
        first = jnp.where(lo, a, pltpu.roll(b, S5_CHUNK, 1))
        second = jnp.where(lo, pltpu.roll(a, S5_CHUNK, 1), b)
        yt = jnp.concatenate([first.reshape(half_rows), second.reshape(half_rows)],
                             axis=1).reshape(W_MIX, 2 * S5_CHUNK)
        y = _gelu(yt.T + d_ref[...] * zd_ref[0, rows, :])
        gate = _sigmoid(_dot(y.astype(BF16), gw_ref[...]) + gb_ref[...])
        yd_ref[rows, :] = (y * gate).astype(yd_ref.dtype)


def _out_ffn_kernel(x_ref, ya_ref, yb_ref, yc_ref, y5_ref, zd_ref, modb_ref, modc_ref, d5_ref,
                    gw_ref, gb_ref, wo_ref, l1w_ref, l1b_ref, wi_ref, wd_ref, l2w_ref, l2b_ref,
                    o_ref, yd_ref, x1_ref, h2_ref, a_ref, *, ctx_rows):
    tm = x_ref.shape[1]
    n_lat = tm - ctx_rows
    segs = [(0, n_lat, 0)] + ([(n_lat, tm, 1)] if ctx_rows else [])
    _s5_readout(y5_ref, zd_ref, d5_ref, gw_ref, gb_ref, yd_ref)
    y = (_dot(ya_ref[0], wo_ref[0, 0:W_MIX, :]) + _dot(yb_ref[0], wo_ref[0, W_MIX:2 * W_MIX, :]) +
         _dot(yc_ref[0], wo_ref[0, 2 * W_MIX:3 * W_MIX, :]) +
         _dot(yd_ref[...], wo_ref[0, 3 * W_MIX:4 * W_MIX, :]))
    for r0, r1, which in segs:
        g1 = _tile_mod(modb_ref, modc_ref, 2)[which]
        sh2 = _tile_mod(modb_ref, modc_ref, 3)[which]
        sc2 = _tile_mod(modb_ref, modc_ref, 4)[which]
        x1 = _layer_norm(ALPHA * x_ref[0, r0:r1, :] + g1 * y[r0:r1, :], l1w_ref[...], l1b_ref[...])
        x1_ref[r0:r1, :] = x1
        h2_ref[r0:r1, :] = (x1 * (1.0 + sc2) + sh2).astype(BF16)
    h2 = h2_ref[...]
    for kk in range(D_FF // FF_CHUNK):
        cols = slice(kk * FF_CHUNK, (kk + 1) * FF_CHUNK)
        gk = _dot(h2, wi_ref[0, :, cols])
        uk = _dot(h2, wi_ref[0, :, D_FF + kk * FF_CHUNK:D_FF + (kk + 1) * FF_CHUNK])
        a_ref[:, cols] = (_silu(gk) * uk).astype(BF16)
    f = _dot(a_ref[...], wd_ref[0])
    for r0, r1, which in segs:
        g2 = _tile_mod(modb_ref, modc_ref, 5)[which]
        o_ref[0, r0:r1, :] = _layer_norm(ALPHA * x1_ref[r0:r1, :] + g2 * f[r0:r1, :],
                                         l2w_ref[...], l2b_ref[...])


def _out_ffn(xcat, ya, yb, yc, y5, zd, mod, layer, s5_d, glu_w, glu_b, w_out, ln1_w, ln1_b, w_ffn_in,
             w_ffn_out, ln2_w, ln2_b, tm, n_out):
    bsz, s, d = xcat.shape
    ctx_rows = CTX_LEN if n_out == s else 0
    tok = lambda w: pl.BlockSpec((1, tm, w), lambda b, j: (b, j, 0))
    row = lambda a: a.reshape(1, -1)
    return pl.pallas_call(
        functools.partial(_out_ffn_kernel, ctx_rows=ctx_rows),
        grid=(bsz, n_out // tm),
        in_specs=[tok(d), tok(W_MIX), tok(W_MIX), tok(W_MIX),
                  pl.BlockSpec((S5_GROUPS, tm // S5_CHUNK, 1, S5_HALF, 2 * S5_CHUNK),
                               lambda b, j: (0, j, b, 0, 0)),
                  tok(W_MIX),
                  pl.BlockSpec((1, 6, d), lambda b, j: (b, 0, 0)),
                  pl.BlockSpec((1, 6, d), lambda b, j: (bsz, 0, 0)),
                  _const_spec((1, W_MIX)), _const_spec((W_MIX, W_MIX)), _const_spec((1, W_MIX)),
                  _layer_spec(w_out, layer), _const_spec((1, d)), _const_spec((1, d)),
                  _layer_spec(w_ffn_in, layer), _layer_spec(w_ffn_out, layer),
                  _const_spec((1, d)), _const_spec((1, d))],
        out_specs=tok(d),
        out_shape=jax.ShapeDtypeStruct((bsz, n_out, d), F32),
        scratch_shapes=[pltpu.VMEM((tm, W_MIX), BF16), pltpu.VMEM((tm, d), F32),
                        pltpu.VMEM((tm, d), BF16), pltpu.VMEM((tm, D_FF), BF16)],
        compiler_params=_params("arbitrary", "arbitrary"),
        name="out_ffn",
    )(xcat, ya, yb, yc, y5, zd, mod, mod, row(s5_d), glu_w.astype(BF16), row(glu_b), w_out,
      row(ln1_w), row(ln1_b), w_ffn_in, w_ffn_out, row(ln2_w), row(ln2_b))


_S5_LOCAL = np.arange(S5_P).reshape(S5_HALF, 2).T.reshape(-1)
_S5_ORDER = (np.arange(S5_GROUPS)[:, None] * S5_P + _S5_LOCAL[None, :]).reshape(-1)


def _reorder_w_in(w):
    off_b = 4 * W_MIX + N_GATES
    off_d = w.shape[-1] - W_MIX
    w_r = jnp.concatenate([w[..., :4 * W_MIX], w[..., off_b:off_d], w[..., off_d:][..., _S5_ORDER]], axis=-1)
    return w_r.astype(BF16), jnp.swapaxes(w[..., 4 * W_MIX:off_b], -1, -2).astype(BF16)


def kernel(x, c, ctx, c_ctx, w_mod, b_mod, w_in, mlstm_gate_bias, mlstm_norm_w, sgu_ln_w, sgu_ln_b, sgu_w, sgu_b, conv_w, conv_b, conv_ln_w, conv_ln_b, s5_a_re, s5_a_im, s5_log_dt, s5_b_re, s5_b_im, s5_c_re, s5_c_im, s5_d, s5_glu_w, s5_glu_b, w_out, ln1_w, ln1_b, w_ffn_in, w_ffn_out, ln2_w, ln2_b):
    bsz, seq, d = x.shape
    s = seq + CTX_LEN
    assert seq == SEQ_LATENT and ctx.shape[1] == CTX_LEN
    mod_rows = -(-(bsz + 1) // 8) * 8
    cvec = jnp.zeros((mod_rows, d), F32).at[:bsz].set(c).at[bsz].set(c_ctx)
    mod_all = _modulation(cvec, w_mod, b_mod).reshape(DEPTH, mod_rows, 6, d)
    nc5 = s // S5_CHUNK
    w_r, wg_t = _reorder_w_in(w_in)
    w_out_b = jnp.concatenate([w_out[:, :3 * W_MIX], w_out[:, 3 * W_MIX:][:, _S5_ORDER]], axis=1).astype(BF16)
    w_ffn_in_b, w_ffn_out_b = w_ffn_in.astype(BF16), w_ffn_out.astype(BF16)
    kbig, w_loc, w_so, a_rows = _s5_operators(
        s5_a_re, s5_a_im, s5_log_dt, s5_b_re[..., _S5_LOCAL], s5_b_im[..., _S5_LOCAL],
        s5_c_re[:, :, _S5_LOCAL, :], s5_c_im[:, :, _S5_LOCAL, :])
    for l in range(DEPTH):
        mod = mod_all[l]
        src = (x, ctx, 0, True) if l == 0 else (xcat, xcat, seq // TILE, False)
        outs = _in_projection(*src, mod, l, w_r, wg_t, sgu_w[l], sgu_b[l], sgu_ln_w[l], sgu_ln_b[l],
                              conv_w[l], conv_b[l], conv_ln_w[l], conv_ln_b[l])
        if l == 0:
            xcat, outs = outs[0], outs[1:]
        k, qt, vt, ot, yb, yc, zd, ut, gt = outs
        ya = _mlstm(k, qt, vt, ot, gt, mlstm_gate_bias[l], mlstm_norm_w[l], need_ctx=l < DEPTH - 1)
        y5 = _s5_mix(ut.reshape(S5_GROUPS, nc5 * bsz, S5_FLAT), l, kbig, w_loc, w_so, a_rows, bsz)
        y5 = y5.reshape(S5_GROUPS, nc5, bsz, S5_HALF, 2 * S5_CHUNK)
        last = l == DEPTH - 1
        xcat = _out_ffn(xcat, ya, yb, yc, y5, zd, mod, l, s5_d[l][_S5_ORDER],
                        s5_glu_w[l][_S5_ORDER][:, _S5_ORDER], s5_glu_b[l][_S5_ORDER],
                        w_out_b, ln1_w[l], ln1_b[l], w_ffn_in_b, w_ffn_out_b, ln2_w[l], ln2_b[l],
                        tm=TM_LATENT if last else TM_MAIN, n_out=seq if last else s)
    return xcat
```
